```python
import math
import jax, jax.numpy as jnp
from jax import lax
import numpy as np

D_MODEL = 1024
BATCH = 8
SEQ = 4096
DEPTH = 1

MIX_WIDTH = D_MODEL
ATTN_WIDTH = MIX_WIDTH // 2
SSM_WIDTH = MIX_WIDTH - ATTN_WIDTH
HEAD_DIM = 64
N_HEADS = ATTN_WIDTH // HEAD_DIM
N_KV = 2
GQA_R = N_HEADS // N_KV
CMP_LEN = 32
CMP_STRIDE = 16
CMP_HIDDEN = 256
SEL_BLOCK = 64
SEL_TOPK = 16
WINDOW = 512
Q_BLOCK = 128
SSM_GROUP = 16
SSM_GROUPS = SSM_WIDTH // SSM_GROUP
SSM_STATE = 64
D_FF = 2816
EPS = 1e-6

Q_COLS = N_HEADS * HEAD_DIM
KV_COLS = N_KV * HEAD_DIM
GATE_COLS = N_HEADS * 3
IN_SIZES = [Q_COLS] + [KV_COLS] * 6 + [GATE_COLS, SSM_WIDTH]
IN_COLS = sum(IN_SIZES)

kernel_name = "hymba_nsa_s5_macaron"


def rmsnorm(x, g):
    xf = x.astype(jnp.float32)
    r = lax.rsqrt(jnp.mean(xf * xf, axis=-1, keepdims=True) + EPS)
    return (xf * r).astype(x.dtype) * g


def swiglu(x, w_gate, w_up, w_down):
    return (jax.nn.silu(x @ w_gate) * (x @ w_up)) @ w_down


def masked_softmax(s, mask):
    s = jnp.where(mask, s.astype(jnp.float32), -jnp.inf)
    m = jnp.max(s, axis=-1, keepdims=True)
    m = jnp.where(jnp.isfinite(m), m, 0.0)
    e = jnp.where(mask, jnp.exp(s - m), 0.0)
    return e / jnp.maximum(jnp.sum(e, axis=-1, keepdims=True), 1e-30)


def compress_blocks(kv, pos, w1, b1, w2):
    b, t, g, d = kv.shape
    n_half = CMP_LEN // CMP_STRIDE
    nc = t // CMP_STRIDE - n_half + 1
    halves = kv.reshape(b, t // CMP_STRIDE, CMP_STRIDE, g, d)
    blocks = jnp.concatenate([halves[:, j:j + nc] for j in range(n_half)], axis=2)
    blocks = blocks + pos[:, None, :]
    blocks = blocks.transpose(0, 3, 1, 2, 4).reshape(b, g, nc, CMP_LEN * d)
    return jax.nn.gelu(blocks @ w1 + b1) @ w2


def nsa_attention(q, kc, vc, ks, vs, kw, vw, gates):
    b, t = q.shape[:2]
    d = HEAD_DIM
    nq = t // Q_BLOCK
    nb = t // SEL_BLOCK
    nc = kc.shape[2]
    topk = min(SEL_TOPK, nb)
    scale = HEAD_DIM ** -0.5

    q_blocks = q.reshape(b, nq, Q_BLOCK, N_KV, GQA_R, d).transpose(1, 0, 3, 4, 2, 5)
    g_blocks = gates.reshape(b, nq, Q_BLOCK, N_KV, GQA_R, 3).transpose(1, 0, 3, 4, 2, 5)
    ks_blocks = ks.reshape(b, nb, SEL_BLOCK, N_KV, d).transpose(0, 3, 1, 2, 4)
    vs_blocks = vs.reshape(b, nb, SEL_BLOCK, N_KV, d).transpose(0, 3, 1, 2, 4)
    pad = ((0, 0), (0, 0), (WINDOW, 0), (0, 0))
    kw_pad = jnp.pad(kw.transpose(0, 2, 1, 3), pad)
    vw_pad = jnp.pad(vw.transpose(0, 2, 1, 3), pad)

    cmp_start = jnp.arange(nc) * CMP_STRIDE
    cmp_end = cmp_start + CMP_LEN - 1
    sel_start = jnp.arange(nb) * SEL_BLOCK
    overlap = ((cmp_start[:, None] < sel_start[None, :] + SEL_BLOCK)
               & (cmp_start[:, None] + CMP_LEN > sel_start[None, :])).astype(jnp.float32)
    blk = jnp.arange(nb)
    b_idx = jnp.arange(b)[:, None, None, None]
    g_idx = jnp.arange(N_KV)[None, :, None, None]

    def one_block(args):
        c, qb, gb = args
        tq = c * Q_BLOCK + jnp.arange(Q_BLOCK)
        s_c = jnp.einsum('bgrqd,bgnd->bgrqn', qb, kc) * scale
        p_c = masked_softmax(s_c, cmp_end[None, :] <= tq[:, None])
        o_cmp = jnp.einsum('bgrqn,bgnd->bgrqd', p_c.astype(vc.dtype), vc)
        imp = jnp.einsum('bgrqn,nj->bgqj', p_c, overlap)
        cur = tq // SEL_BLOCK
        forced = (blk[None, :] == 0) | (blk[None, :] == cur[:, None]) | (blk[None, :] == cur[:, None] - 1)
        valid = sel_start[None, :] <= tq[:, None]
        score = jnp.where(forced, jnp.inf, jnp.where(valid, imp, -jnp.inf))
        _, idx = lax.top_k(score, topk)
        k_sel = ks_blocks[b_idx, g_idx, idx]
        v_sel = vs_blocks[b_idx, g_idx, idx]
        kpos = idx[..., None] * SEL_BLOCK + jnp.arange(SEL_BLOCK)
        sel_mask = (kpos <= tq[None, None, :, None, None]).reshape(b, N_KV, 1, Q_BLOCK, topk * SEL_BLOCK)
        s_s = jnp.einsum('bgrqd,bgqksd->bgrqks', qb, k_sel) * scale
        p_s = masked_softmax(s_s.reshape(b, N_KV, GQA_R, Q_BLOCK, topk * SEL_BLOCK), sel_mask)
        o_slc = jnp.einsum('bgrqn,bgqnd->bgrqd', p_s.astype(v_sel.dtype),
                           v_sel.reshape(b, N_KV, Q_BLOCK, topk * SEL_BLOCK, d))
        kw_b = lax.dynamic_slice_in_dim(kw_pad, c * Q_BLOCK, WINDOW + Q_BLOCK, axis=2)
        vw_b = lax.dynamic_slice_in_dim(vw_pad, c * Q_BLOCK, WINDOW + Q_BLOCK, axis=2)
        wpos = c * Q_BLOCK - WINDOW + jnp.arange(WINDOW + Q_BLOCK)
        w_mask = ((wpos[None, :] <= tq[:, None]) & (wpos[None, :] > tq[:, None] - WINDOW)
                  & (wpos[None, :] >= 0))
        s_w = jnp.einsum('bgrqd,bgkd->bgrqk', qb, kw_b) * scale
        p_w = masked_softmax(s_w, w_mask)
        o_win = jnp.einsum('bgrqk,bgkd->bgrqd', p_w.astype(vw_b.dtype), vw_b)
        return gb[..., 0:1] * o_cmp + gb[..., 1:2] * o_slc + gb[..., 2:3] * o_win

    out = lax.map(one_block, (jnp.arange(nq), q_blocks, g_blocks))
    return out.transpose(1, 0, 4, 2, 3, 5).reshape(b, t, N_HEADS * d)


def s5_mixer(u, lam_re, lam_im, log_dt, b_re, b_im, c_re, c_im, d_skip, w_glu, b_glu):
    bsz, t, _ = u.shape
    uf = u.astype(jnp.float32).reshape(bsz, t, SSM_GROUPS, SSM_GROUP)
    lr = lam_re.astype(jnp.float32)
    li = lam_im.astype(jnp.float32)
    dt = jnp.exp(log_dt.astype(jnp.float32))[:, None]
    mag = jnp.exp(lr * dt)
    ab_re = mag * jnp.cos(li * dt)
    ab_im = mag * jnp.sin(li * dt)
    nr = ab_re - 1.0
    den = lr * lr + li * li
    f_re = (nr * lr + ab_im * li) / den
    f_im = (ab_im * lr - nr * li) / den
    br = b_re.astype(jnp.float32)
    bi = b_im.astype(jnp.float32)
    bb_re = f_re[..., None] * br - f_im[..., None] * bi
    bb_im = f_re[..., None] * bi + f_im[..., None] * br
    bu_re = jnp.einsum('btgh,gph->btgp', uf, bb_re)
    bu_im = jnp.einsum('btgh,gph->btgp', uf, bb_im)
    a_re = jnp.broadcast_to(ab_re, (1, t, SSM_GROUPS, SSM_STATE))
    a_im = jnp.broadcast_to(ab_im, (1, t, SSM_GROUPS, SSM_STATE))

    def combine(e1, e2):
        a1r, a1i, b1r, b1i = e1
        a2r, a2i, b2r, b2i = e2
        return (a2r * a1r - a2i * a1i, a2r * a1i + a2i * a1r,
                a2r * b1r - a2i * b1i + b2r, a2r * b1i + a2i * b1r + b2i)

    _, _, xr, xi = lax.associative_scan(combine, (a_re, a_im, bu_re, bu_im), axis=1)
    y = (jnp.einsum('btgp,ghp->btgh', xr, c_re.astype(jnp.float32))
         - jnp.einsum('btgp,ghp->btgh', xi, c_im.astype(jnp.float32))
         + d_skip.astype(jnp.float32) * uf)
    y = jax.nn.gelu(y.reshape(bsz, t, SSM_WIDTH).astype(u.dtype))
    return y * jax.nn.sigmoid(y @ w_glu + b_glu)


def setup_inputs(seed: int = 0) -> dict:
    key = jax.random.key(seed)
    ks = jax.random.split(key, 40)
    L = DEPTH

    def nrm(k, shape, scale):
        return jax.random.normal(k, shape, jnp.float32) * scale

    def gain(k, shape):
        return 1.0 + 0.01 * jax.random.normal(k, shape, jnp.float32)

    n_idx = jnp.arange(SSM_STATE, dtype=jnp.float32)
    lam_re = -0.5 + 0.01 * jax.random.normal(ks[17], (L, SSM_GROUPS, SSM_STATE), jnp.float32)
    lam_im = math.pi * n_idx + 0.01 * jax.random.normal(ks[18], (L, SSM_GROUPS, SSM_STATE), jnp.float32)
    log_dt = jax.random.uniform(ks[19], (L, SSM_GROUPS), jnp.float32, math.log(1e-3), math.log(1e-1))
    return {
        "x": nrm(ks[0], (BATCH, SEQ, D_MODEL), 1.0),
        "ffn1_norm": gain(ks[1], (L, D_MODEL)),
        "ffn1_w_gate": nrm(ks[2], (L, D_MODEL, D_FF), D_MODEL ** -0.5),
        "ffn1_w_up": nrm(ks[3], (L, D_MODEL, D_FF), D_MODEL ** -0.5),
        "ffn1_w_down": nrm(ks[4], (L, D_FF, D_MODEL), D_FF ** -0.5),
        "mix_norm": gain(ks[5], (L, D_MODEL)),
        "w_in": nrm(ks[6], (L, D_MODEL, IN_COLS), D_MODEL ** -0.5),
        "gate_bias": nrm(ks[7], (L, N_HEADS, 3), 0.01),
        "cmp_pos_k": nrm(ks[8], (L, CMP_LEN, HEAD_DIM), 0.02),
        "cmp_pos_v": nrm(ks[9], (L, CMP_LEN, HEAD_DIM), 0.02),
        "cmp_k_w1": nrm(ks[10], (L, CMP_LEN * HEAD_DIM, CMP_HIDDEN), (CMP_LEN * HEAD_DIM) ** -0.5),
        "cmp_k_b1": nrm(ks[11], (L, CMP_HIDDEN), 0.01),
        "cmp_k_w2": nrm(ks[12], (L, CMP_HIDDEN, HEAD_DIM), CMP_HIDDEN ** -0.5),
        "cmp_v_w1": nrm(ks[13], (L, CMP_LEN * HEAD_DIM, CMP_HIDDEN), (CMP_LEN * HEAD_DIM) ** -0.5),
        "cmp_v_b1": nrm(ks[14], (L, CMP_HIDDEN), 0.01),
        "cmp_v_w2": nrm(ks[15], (L, CMP_HIDDEN, HEAD_DIM), CMP_HIDDEN ** -0.5),
        "s5_lambda_re": lam_re,
        "s5_lambda_im": lam_im,
        "s5_log_dt": log_dt,
        "s5_b_re": nrm(ks[20], (L, SSM_GROUPS, SSM_STATE, SSM_GROUP), (2 * SSM_GROUP) ** -0.5),
        "s5_b_im": nrm(ks[21], (L, SSM_GROUPS, SSM_STATE, SSM_GROUP), (2 * SSM_GROUP) ** -0.5),
        "s5_c_re": nrm(ks[22], (L, SSM_GROUPS, SSM_GROUP, SSM_STATE), (2 * SSM_STATE) ** -0.5),
        "s5_c_im": nrm(ks[23], (L, SSM_GROUPS, SSM_GROUP, SSM_STATE), (2 * SSM_STATE) ** -0.5),
        "s5_d": nrm(ks[24], (L, SSM_GROUPS, SSM_GROUP), 1.0),
        "s5_w_glu": nrm(ks[25], (L, SSM_WIDTH, SSM_WIDTH), SSM_WIDTH ** -0.5),
        "s5_b_glu": nrm(ks[26], (L, SSM_WIDTH), 0.01),
        "attn_out_norm": gain(ks[27], (L, ATTN_WIDTH)),
        "ssm_out_norm": gain(ks[28], (L, SSM_WIDTH)),
        "w_out": nrm(ks[29], (L, MIX_WIDTH, D_MODEL), MIX_WIDTH ** -0.5),
        "ffn2_norm": gain(ks[30], (L, D_MODEL)),
        "ffn2_w_gate": nrm(ks[31], (L, D_MODEL, D_FF), D_MODEL ** -0.5),
        "ffn2_w_up": nrm(ks[32], (L, D_MODEL, D_FF), D_MODEL ** -0.5),
        "ffn2_w_down": nrm(ks[33], (L, D_FF, D_MODEL), D_FF ** -0.5),
        "final_norm": gain(ks[34], (D_MODEL,)),
    }


def reference(x, ffn1_norm, ffn1_w_gate, ffn1_w_up, ffn1_w_down, mix_norm, w_in, gate_bias,
              cmp_pos_k, cmp_pos_v, cmp_k_w1, cmp_k_b1, cmp_k_w2, cmp_v_w1, cmp_v_b1, cmp_v_w2,
              s5_lambda_re, s5_lambda_im, s5_log_dt, s5_b_re, s5_b_im, s5_c_re, s5_c_im, s5_d,
              s5_w_glu, s5_b_glu, attn_out_norm, ssm_out_norm, w_out,
              ffn2_norm, ffn2_w_gate, ffn2_w_up, ffn2_w_down, final_norm):
    b, t, _ = x.shape
    split_at = [int(v) for v in np.cumsum(IN_SIZES)[:-1]]
    for l in range(DEPTH):
        x = x + 0.5 * swiglu(rmsnorm(x, ffn1_norm[l]), ffn1_w_gate[l], ffn1_w_up[l], ffn1_w_down[l])
        h = rmsnorm(x, mix_norm[l])
        proj = h @ w_in[l]
        pq, pkc, pvc, pks, pvs, pkw, pvw, pg, pu = jnp.split(proj, split_at, axis=-1)
        q = pq.reshape(b, t, N_HEADS, HEAD_DIM)
        kv_shape = (b, t, N_KV, HEAD_DIM)
        kc = compress_blocks(pkc.reshape(kv_shape), cmp_pos_k[l], cmp_k_w1[l], cmp_k_b1[l], cmp_k_w2[l])
        vc = compress_blocks(pvc.reshape(kv_shape), cmp_pos_v[l], cmp_v_w1[l], cmp_v_b1[l], cmp_v_w2[l])
        gates = jax.nn.sigmoid(pg.reshape(b, t, N_HEADS, 3) + gate_bias[l])
        attn = nsa_attention(q, kc, vc, pks.reshape(kv_shape), pvs.reshape(kv_shape),
                             pkw.reshape(kv_shape), pvw.reshape(kv_shape), gates)
        ssm = s5_mixer(pu, s5_lambda_re[l], s5_lambda_im[l], s5_log_dt[l], s5_b_re[l], s5_b_im[l],
                       s5_c_re[l], s5_c_im[l], s5_d[l], s5_w_glu[l], s5_b_glu[l])
        mixed = jnp.concatenate([rmsnorm(attn, attn_out_norm[l]), rmsnorm(ssm, ssm_out_norm[l])], axis=-1)
        x = x + mixed @ w_out[l]
        x = x + 0.5 * swiglu(rmsnorm(x, ffn2_norm[l]), ffn2_w_gate[l], ffn2_w_up[l], ffn2_w_down[l])
    return rmsnorm(x, final_norm)
```

```python
import functools
import math

import jax
import jax.numpy as jnp
from jax import lax
from jax.experimental import pallas as pl
from jax.experimental.pallas import tpu as pltpu

D_MODEL = 1024
HEAD_DIM = 64
N_HEADS = 8
N_KV = 2
GQA_R = N_HEADS // N_KV
ATTN_WIDTH = N_HEADS * HEAD_DIM
CMP_LEN = 32
CMP_STRIDE = 16
CMP_HIDDEN = 256
SEL_BLOCK = 64
SEL_SHIFT = 6
SEL_TOPK = 16
WINDOW = 512
Q_BLOCK = 128
SSM_WIDTH = 512
SSM_GROUP = 16
SSM_GROUPS = SSM_WIDTH // SSM_GROUP
SSM_STATE = 64
D_FF = 2816
EPS = 1e-6

LANES = 128
FF_CHUNK = 256
TOKEN_TILE = 512
KEY_TILE = 128
S5_TIME_CHUNK = 64
S5_LANE_CHUNK = 512
S5_HALF = SSM_WIDTH // 2
S5_HALF_STATE = (SSM_GROUPS // 2) * SSM_STATE
VMEM_LIMIT = 56 * 1024 * 1024
NEG_BIG = -1e30

Q_OFF = 0
KVC_OFF = 512
KVS_OFF = 768
KVW_OFF = 1024
GATE_OFF = 1280
U_OFF = 1536
IN_COLS_PADDED = 2048
N_GATE_COLS = N_HEADS * 3


def _dot(a, b):
    return jnp.dot(a, b, preferred_element_type=jnp.float32)


def _dot_nt(a, b):
    return lax.dot_general(a, b, (((1,), (1,)), ((), ())), preferred_element_type=jnp.float32)


def _rmsnorm(x, g):
    r = lax.rsqrt(jnp.mean(x * x, axis=-1, keepdims=True) + EPS)
    return (x * r) * g


def _swiglu_acc(h, wg_ref, wu_ref, wd_ref):
    acc = jnp.zeros((h.shape[0], D_MODEL), jnp.float32)
    for f in range(0, D_FF, FF_CHUNK):
        gate = _dot(h, wg_ref[:, f:f + FF_CHUNK])
        up = _dot(h, wu_ref[:, f:f + FF_CHUNK])
        act = (gate * jax.nn.sigmoid(gate) * up).astype(jnp.bfloat16)
        acc = acc + _dot(act, wd_ref[f:f + FF_CHUNK, :])
    return acc


def _resident(shape):
    nd = len(shape)
    return pl.BlockSpec(shape, lambda *_: (0,) * nd, pipeline_mode=pl.Buffered(1))


def _ffn1_proj_kernel(x_ref, g1_ref, wg_ref, wu_ref, wd_ref, gm_ref, win_ref, gb_ref,
                      x1_ref, q_ref, kvc_ref, kvs_ref, kvw_ref, gate_ref, u_ref):
    x = x_ref[...]
    h = _rmsnorm(x, g1_ref[...]).astype(jnp.bfloat16)
    x1 = x + 0.5 * _swiglu_acc(h, wg_ref, wu_ref, wd_ref)
    x1_ref[...] = x1
    h2 = _rmsnorm(x1, gm_ref[...]).astype(jnp.bfloat16)
    p = _dot(h2, win_ref[...])
    for hh in range(N_HEADS):
        lo = Q_OFF + hh * HEAD_DIM
        q_ref[hh] = (p[:, lo:lo + HEAD_DIM] * (HEAD_DIM ** -0.5)).astype(q_ref.dtype)
    for i in range(2 * N_KV):
        kvc_ref[i] = p[:, KVC_OFF + i * HEAD_DIM:KVC_OFF + (i + 1) * HEAD_DIM]
        kvs_ref[i] = p[:, KVS_OFF + i * HEAD_DIM:KVS_OFF + (i + 1) * HEAD_DIM].astype(kvs_ref.dtype)
        kvw_ref[i] = p[:, KVW_OFF + i * HEAD_DIM:KVW_OFF + (i + 1) * HEAD_DIM].astype(kvw_ref.dtype)
    for g in range(N_KV):
        lo = GATE_OFF + g * LANES
        gate_ref[g] = jax.nn.sigmoid(p[:, lo:lo + LANES] + gb_ref[:, g * LANES:(g + 1) * LANES])
    u_ref[...] = p[:, U_OFF:U_OFF + SSM_WIDTH]


def _ffn1_proj(x2d, g1, wg, wu, wd, gm, win, gb, bsz, t):
    n = bsz * t
    tm = min(TOKEN_TILE, t)
    nt = t // tm
    row = lambda i: (i, 0)
    lead = lambda i: (0, i, 0)
    return pl.pallas_call(
        _ffn1_proj_kernel,
        grid=(n // tm,),
        in_specs=[
            pl.BlockSpec((tm, D_MODEL), row),
            _resident((1, D_MODEL)),
            _resident((D_MODEL, D_FF)), _resident((D_MODEL, D_FF)), _resident((D_FF, D_MODEL)),
            _resident((1, D_MODEL)),
            _resident((D_MODEL, IN_COLS_PADDED)),
            _resident((1, N_KV * LANES)),
        ],
        out_specs=[
            pl.BlockSpec((tm, D_MODEL), row),
            pl.BlockSpec((N_HEADS, tm, HEAD_DIM), lead),
            pl.BlockSpec((2 * N_KV, tm, HEAD_DIM), lead),
            pl.BlockSpec((2 * N_KV, tm, HEAD_DIM), lead),
            pl.BlockSpec((2 * N_KV, tm, HEAD_DIM), lead),
            pl.BlockSpec((N_KV, tm, LANES), lead),
            pl.BlockSpec((tm, SSM_WIDTH), lambda i: (i % nt, i // nt)),
        ],
        out_shape=[
            jax.ShapeDtypeStruct((n, D_MODEL), jnp.float32),
            jax.ShapeDtypeStruct((N_HEADS, n, HEAD_DIM), jnp.bfloat16),
            jax.ShapeDtypeStruct((2 * N_KV, n, HEAD_DIM), jnp.float32),
            jax.ShapeDtypeStruct((2 * N_KV, n, HEAD_DIM), jnp.bfloat16),
            jax.ShapeDtypeStruct((2 * N_KV, n, HEAD_DIM), jnp.bfloat16),
            jax.ShapeDtypeStruct((N_KV, n, LANES), jnp.float32),
            jax.ShapeDtypeStruct((t, bsz * SSM_WIDTH), jnp.float32),
        ],
        compiler_params=pltpu.CompilerParams(
            dimension_semantics=("parallel",), vmem_limit_bytes=VMEM_LIMIT),
    )(x2d, g1, wg, wu, wd, gm, win, gb)


def _compress_kernel(h_ref, pos_ref, w1_ref, b1_ref, w2_ref, o_ref):
    half = CMP_STRIDE * HEAD_DIM
    hrows = h_ref[0, 0]
    top = (hrows + pos_ref[0, :, :half]).astype(jnp.bfloat16)
    bot = (hrows + pos_ref[0, :, half:]).astype(jnp.bfloat16)
    a = _dot(top, w1_ref[0, :half, :])
    b = _dot(bot, w1_ref[0, half:, :])
    nrow = hrows.shape[0]
    pre = a + pltpu.roll(b, nrow - 1, 0) + b1_ref[0]
    out = _dot(jax.nn.gelu(pre).astype(jnp.bfloat16), w2_ref[0])
    rid = lax.broadcasted_iota(jnp.int32, out.shape, 0)
    o_ref[0, 0] = jnp.where(rid < nrow - 1, out, 0.0)


def _compress(kvc4, pos, w1, b1, w2):
    _, bsz, nrow, half = kvc4.shape
    kind = lambda i, b: (i // N_KV, 0, 0)
    return pl.pallas_call(
        _compress_kernel,
        grid=(2 * N_KV, bsz),
        in_specs=[
            pl.BlockSpec((1, 1, nrow, half), lambda i, b: (i, b, 0, 0)),
            pl.BlockSpec((1, 1, CMP_LEN * HEAD_DIM), kind),
            pl.BlockSpec((1, CMP_LEN * HEAD_DIM, CMP_HIDDEN), kind),
            pl.BlockSpec((1, 1, CMP_HIDDEN), kind),
            pl.BlockSpec((1, CMP_HIDDEN, HEAD_DIM), kind),
        ],
        out_specs=pl.BlockSpec((1, 1, nrow, HEAD_DIM), lambda i, b: (i, b, 0, 0)),
        out_shape=jax.ShapeDtypeStruct((2 * N_KV, bsz, nrow, HEAD_DIM), jnp.float32),
        compiler_params=pltpu.CompilerParams(dimension_semantics=("parallel", "parallel")),
    )(kvc4, pos, w1, b1, w2)


def _split3_bf16(x):
    hi = x.astype(jnp.bfloat16)
    r1 = x - hi.astype(jnp.float32)
    mid = r1.astype(jnp.bfloat16)
    lo = (r1 - mid.astype(jnp.float32)).astype(jnp.bfloat16)
    return hi, mid, lo


def _sweep(q, k_ref, v_ref, first_tile, last_tile, mask_fn):
    rows = q.shape[0]

    def body(kt, carry):
        m, l, acc = carry
        start = pl.multiple_of(kt * KEY_TILE, KEY_TILE)
        k = k_ref[0, pl.ds(start, KEY_TILE), :]
        v = v_ref[0, pl.ds(start, KEY_TILE), :]
        mask = mask_fn(kt)
        s = _dot_nt(q, k).reshape(rows // Q_BLOCK, Q_BLOCK, KEY_TILE)
        s = jnp.where(mask[None], s, NEG_BIG).reshape(rows, KEY_TILE)
        m_new = jnp.maximum(m, jnp.max(s, axis=-1, keepdims=True))
        alpha = jnp.exp(m - m_new)
        p = jnp.exp(s - m_new)
        l = alpha * l + jnp.sum(p, axis=-1, keepdims=True)
        acc = alpha * acc + _dot(p.astype(v.dtype), v)
        return m_new, l, acc

    init = (jnp.full((rows, 1), NEG_BIG, jnp.float32), jnp.zeros((rows, 1), jnp.float32),
            jnp.zeros((rows, HEAD_DIM), jnp.float32))
    _, l, acc = lax.fori_loop(first_tile, last_tile + 1, body, init)
    return acc / l


def _nsa_kernel(q_ref, kc_ref, vc_ref, ks_ref, vs_ref, kw_ref, vw_ref, gate_ref, o_ref):
    c = pl.program_id(2)
    rows = GQA_R * Q_BLOCK
    q = q_ref[...].reshape(rows, HEAD_DIM)
    ncp = kc_ref.shape[2]

    kc = kc_ref[0, 0].astype(jnp.bfloat16)
    vc = vc_ref[0, 0].astype(jnp.bfloat16)
    s_c = _dot_nt(q, kc)
    tq_c = c * Q_BLOCK + (lax.broadcasted_iota(jnp.int32, (rows, ncp), 0) & (Q_BLOCK - 1))
    n_c = lax.broadcasted_iota(jnp.int32, (rows, ncp), 1)
    mask_c = n_c * CMP_STRIDE + (CMP_LEN - 1) <= tq_c
    s_c = jnp.where(mask_c, s_c, -jnp.inf)
    m_c = jnp.max(s_c, axis=-1, keepdims=True)
    m_c = jnp.where(m_c > -jnp.inf, m_c, 0.0)
    e_c = jnp.where(mask_c, jnp.exp(s_c - m_c), 0.0)
    p_c = e_c / jnp.maximum(jnp.sum(e_c, axis=-1, keepdims=True), 1e-30)
    o_cmp = _dot(p_c.astype(jnp.bfloat16), vc)

    p_sum = p_c[0:Q_BLOCK]
    for r in range(1, GQA_R):
        p_sum = p_sum + p_c[r * Q_BLOCK:(r + 1) * Q_BLOCK]
    nb_pad = LANES
    jb = lax.broadcasted_iota(jnp.int32, (nb_pad, ncp), 0)
    nn = lax.broadcasted_iota(jnp.int32, (nb_pad, ncp), 1)
    overlap_t = ((nn * CMP_STRIDE < jb * SEL_BLOCK + SEL_BLOCK)
                 & (nn * CMP_STRIDE + CMP_LEN > jb * SEL_BLOCK)
                 & (nn < ncp - 1))
    overlap_t = jnp.where(overlap_t, 1.0, 0.0).astype(jnp.bfloat16)
    hi, mid, lo = _split3_bf16(p_sum)
    imp_t = _dot_nt(overlap_t, hi) + _dot_nt(overlap_t, mid) + _dot_nt(overlap_t, lo)

    nb = ks_ref.shape[1] // SEL_BLOCK
    j_t = lax.broadcasted_iota(jnp.int32, (nb_pad, Q_BLOCK), 0)
    tq_t = c * Q_BLOCK + lax.broadcasted_iota(jnp.int32, (nb_pad, Q_BLOCK), 1)
    cur = tq_t >> SEL_SHIFT
    forced = (j_t == 0) | (j_t == cur) | (j_t == cur - 1)
    valid = j_t * SEL_BLOCK <= tq_t
    score = jnp.where(forced, jnp.inf, jnp.where(valid, imp_t, -jnp.inf))
    rank = jnp.zeros((nb_pad, Q_BLOCK), jnp.float32)
    for jp in range(nb):
        row = score[jp:jp + 1, :]
        beats = (row > score) | ((row == score) & (j_t > jp))
        rank = rank + jnp.where(beats, 1.0, 0.0)
    topk = min(SEL_TOPK, nb)
    sel_t = jnp.where((rank < topk) & (j_t < nb), 1.0, 0.0)
    sel = sel_t.T.astype(jnp.bfloat16)

    tq = c * Q_BLOCK + lax.broadcasted_iota(jnp.int32, (Q_BLOCK, KEY_TILE), 0)
    kk = lax.broadcasted_iota(jnp.int32, (Q_BLOCK, KEY_TILE), 1)
    eb = lax.broadcasted_iota(jnp.int32, (nb_pad, KEY_TILE), 0)
    ek = lax.broadcasted_iota(jnp.int32, (nb_pad, KEY_TILE), 1)
    blocks_per_tile = KEY_TILE // SEL_BLOCK

    def sel_mask(kt):
        expand = jnp.where(eb == kt * blocks_per_tile + (ek >> SEL_SHIFT), 1.0, 0.0).astype(jnp.bfloat16)
        chosen = _dot(sel, expand) > 0.5
        return chosen & (kt * KEY_TILE + kk <= tq)

    o_slc = _sweep(q, ks_ref, vs_ref, 0, c, sel_mask)

    def win_mask(kt):
        kpos = kt * KEY_TILE + kk
        return (kpos <= tq) & (kpos > tq - WINDOW)

    first_w = jnp.maximum(c - WINDOW // KEY_TILE, 0)
    o_win = _sweep(q, kw_ref, vw_ref, first_w, c, win_mask)

    gt = gate_ref[0]
    outs = []
    for r in range(GQA_R):
        sl = slice(r * Q_BLOCK, (r + 1) * Q_BLOCK)
        outs.append(gt[:, 3 * r:3 * r + 1] * o_cmp[sl]
                    + gt[:, 3 * r + 1:3 * r + 2] * o_slc[sl]
                    + gt[:, 3 * r + 2:3 * r + 3] * o_win[sl])
    o_ref[...] = jnp.concatenate(outs, axis=-1)


def _nsa(q, kvc, kvs, kvw, gates, bsz, t):
    nq = t // Q_BLOCK
    ncp = kvc.shape[2]
    kv_c = lambda off: pl.BlockSpec((1, 1, ncp, HEAD_DIM), lambda b, g, c: (off + g, b, 0, 0))
    kv_t = lambda off: pl.BlockSpec((1, t, HEAD_DIM), lambda b, g, c: (off + g, b, 0))
    return pl.pallas_call(
        _nsa_kernel,
        grid=(bsz, N_KV, nq),
        in_specs=[
            pl.BlockSpec((GQA_R, Q_BLOCK, HEAD_DIM), lambda b, g, c: (g, b * nq + c, 0)),
            kv_c(0), kv_c(N_KV),
            kv_t(0), kv_t(N_KV),
            kv_t(0), kv_t(N_KV),
            pl.BlockSpec((1, Q_BLOCK, LANES), lambda b, g, c: (g, b * nq + c, 0)),
        ],
        out_specs=pl.BlockSpec((Q_BLOCK, GQA_R * HEAD_DIM), lambda b, g, c: (b * nq + c, g)),
        out_shape=jax.ShapeDtypeStruct((bsz * t, ATTN_WIDTH), jnp.float32),
        compiler_params=pltpu.CompilerParams(
            dimension_semantics=("parallel", "parallel", "parallel")),
    )(q, kvc, kvc, kvs, kvs, kvw, kvw, gates)


def _s5_kernel(u_ref, bm_ref, cm_ref, ar_ref, ai_ref, d_ref, wglu_ref, bglu_ref, o_ref,
               bu_ref, state_ref, *, bsz):
    steps = u_ref.shape[0] // bsz

    @pl.when(pl.program_id(0) == 0)
    def _():
        state_ref[...] = jnp.zeros_like(state_ref)

    u = u_ref[...]
    ub = u.astype(jnp.bfloat16)
    ys = []
    for half in range(2):
        bu_ref[...] = _dot(ub[:, half * S5_HALF:(half + 1) * S5_HALF], bm_ref[half])
        for lc in range(0, S5_HALF_STATE, S5_LANE_CHUNK):
            re = slice(lc, lc + S5_LANE_CHUNK)
            im = slice(S5_HALF_STATE + lc, S5_HALF_STATE + lc + S5_LANE_CHUNK)
            a_re = jnp.broadcast_to(ar_ref[half, :, re], (bsz, S5_LANE_CHUNK))
            a_im = jnp.broadcast_to(ai_ref[half, :, re], (bsz, S5_LANE_CHUNK))

            def step(i, carry):
                xr, xi = carry
                rows = pl.ds(pl.multiple_of(i * bsz, bsz), bsz)
                nr = a_re * xr - a_im * xi + bu_ref[rows, re]
                ni = a_re * xi + a_im * xr + bu_ref[rows, im]
                bu_ref[rows, re] = nr
                bu_ref[rows, im] = ni
                return nr, ni

            xr, xi = lax.fori_loop(0, steps, step, (state_ref[half, :, re], state_ref[half, :, im]),
                                   unroll=4)
            state_ref[half, :, re] = xr
            state_ref[half, :, im] = xi
        ys.append(_dot(bu_ref[...].astype(jnp.bfloat16), cm_ref[half]))
    y = jnp.concatenate(ys, axis=-1) + d_ref[...] * u
    y = jax.nn.gelu(y)
    o_ref[...] = y * jax.nn.sigmoid(_dot(y.astype(jnp.bfloat16), wglu_ref[...]) + bglu_ref[...])


def _s5(u_tb, bm, cm, ar, ai, d, wglu, bglu, bsz):
    n = u_tb.shape[0]
    rows = S5_TIME_CHUNK * bsz
    const3 = lambda i: (0, 0, 0)
    const2 = lambda i: (0, 0)
    return pl.pallas_call(
        functools.partial(_s5_kernel, bsz=bsz),
        grid=(n // rows,),
        in_specs=[
            pl.BlockSpec((rows, SSM_WIDTH), lambda i: (i, 0)),
            pl.BlockSpec((2, S5_HALF, 2 * S5_HALF_STATE), const3),
            pl.BlockSpec((2, 2 * S5_HALF_STATE, S5_HALF), const3),
            pl.BlockSpec((2, 1, S5_HALF_STATE), const3),
            pl.BlockSpec((2, 1, S5_HALF_STATE), const3),
            pl.BlockSpec((1, SSM_WIDTH), const2),
            pl.BlockSpec((SSM_WIDTH, SSM_WIDTH), const2),
            pl.BlockSpec((1, SSM_WIDTH), const2),
        ],
        out_specs=pl.BlockSpec((rows, SSM_WIDTH), lambda i: (i, 0)),
        out_shape=jax.ShapeDtypeStruct((n, SSM_WIDTH), jnp.float32),
        scratch_shapes=[
            pltpu.VMEM((rows, 2 * S5_HALF_STATE), jnp.float32),
            pltpu.VMEM((2, bsz, 2 * S5_HALF_STATE), jnp.float32),
        ],
        compiler_params=pltpu.CompilerParams(
            dimension_semantics=("arbitrary",), vmem_limit_bytes=VMEM_LIMIT),
    )(u_tb, bm, cm, ar, ai, d, wglu, bglu)


def _s5_params(lam_re, lam_im, log_dt, b_re, b_im, c_re, c_im):
    dt = jnp.exp(log_dt)[:, None]
    mag = jnp.exp(lam_re * dt)
    ab_re = mag * jnp.cos(lam_im * dt)
    ab_im = mag * jnp.sin(lam_im * dt)
    nr = ab_re - 1.0
    den = lam_re * lam_re + lam_im * lam_im
    f_re = (nr * lam_re + ab_im * lam_im) / den
    f_im = (ab_im * lam_re - nr * lam_im) / den
    bb_re = f_re[..., None] * b_re - f_im[..., None] * b_im
    bb_im = f_re[..., None] * b_im + f_im[..., None] * b_re
    gh = SSM_GROUPS // 2
    eye = jnp.eye(gh, dtype=jnp.float32)

    def in_half(bb):
        return jnp.einsum('gph,gk->ghkp', bb, eye).reshape(S5_HALF, S5_HALF_STATE)

    def out_half(cc):
        return jnp.einsum('ghp,gk->gpkh', cc, eye).reshape(S5_HALF_STATE, S5_HALF)

    bm = jnp.stack([jnp.concatenate([in_half(bb_re[s]), in_half(bb_im[s])], axis=1)
                    for s in (slice(0, gh), slice(gh, None))])
    cm = jnp.stack([jnp.concatenate([out_half(c_re[s]), -out_half(c_im[s])], axis=0)
                    for s in (slice(0, gh), slice(gh, None))])
    ar = ab_re.reshape(2, 1, S5_HALF_STATE)
    ai = ab_im.reshape(2, 1, S5_HALF_STATE)
    return bm.astype(jnp.bfloat16), cm.astype(jnp.bfloat16), ar, ai


def _out_ffn2_kernel(x1_ref, a_ref, s_ref, ga_ref, gs_ref, wo_ref, g2_ref, wg_ref, wu_ref, wd_ref,
                     gf_ref, o_ref):
    a = _rmsnorm(a_ref[...], ga_ref[...]).astype(jnp.bfloat16)
    s = _rmsnorm(s_ref[...], gs_ref[...]).astype(jnp.bfloat16)
    x2 = x1_ref[...] + _dot(a, wo_ref[:ATTN_WIDTH, :]) + _dot(s, wo_ref[ATTN_WIDTH:, :])
    h = _rmsnorm(x2, g2_ref[...]).astype(jnp.bfloat16)
    x3 = x2 + 0.5 * _swiglu_acc(h, wg_ref, wu_ref, wd_ref)
    o_ref[...] = _rmsnorm(x3, gf_ref[...])


def _out_ffn2(x1, attn, ssm_t, ga, gs, wo, g2, wg, wu, wd, gf, bsz, t):
    n = bsz * t
    tm = min(TOKEN_TILE, t)
    nt = t // tm
    row = lambda i: (i, 0)
    return pl.pallas_call(
        _out_ffn2_kernel,
        grid=(n // tm,),
        in_specs=[
            pl.BlockSpec((tm, D_MODEL), row),
            pl.BlockSpec((tm, ATTN_WIDTH), row),
            pl.BlockSpec((tm, SSM_WIDTH), lambda i: (i % nt, i // nt)),
            _resident((1, ATTN_WIDTH)), _resident((1, SSM_WIDTH)),
            _resident((D_MODEL, D_MODEL)),
            _resident((1, D_MODEL)),
            _resident((D_MODEL, D_FF)), _resident((D_MODEL, D_FF)), _resident((D_FF, D_MODEL)),
            _resident((1, D_MODEL)),
        ],
        out_specs=pl.BlockSpec((tm, D_MODEL), row),
        out_shape=jax.ShapeDtypeStruct((n, D_MODEL), jnp.float32),
        compiler_params=pltpu.CompilerParams(
            dimension_semantics=("parallel",), vmem_limit_bytes=VMEM_LIMIT),
    )(x1, attn, ssm_t, ga, gs, wo, g2, wg, wu, wd, gf)


def _reorder_w_in(w_in, gate_bias):
    per_g = N_GATE_COLS // N_KV
    g_lo = GATE_OFF
    pieces = [w_in[:, :g_lo]]
    bias = []
    gb = gate_bias.reshape(1, N_GATE_COLS)
    for g in range(N_KV):
        cols = w_in[:, g_lo + g * per_g:g_lo + (g + 1) * per_g]
        pieces.append(jnp.pad(cols, ((0, 0), (0, LANES - per_g))))
        bias.append(jnp.pad(gb[:, g * per_g:(g + 1) * per_g], ((0, 0), (0, LANES - per_g))))
    pieces.append(w_in[:, g_lo + N_GATE_COLS:])
    return jnp.concatenate(pieces, axis=1).astype(jnp.bfloat16), jnp.concatenate(bias, axis=1)


def kernel(x, ffn1_norm, ffn1_w_gate, ffn1_w_up, ffn1_w_down, mix_norm, w_in, gate_bias, cmp_pos_k, cmp_pos_v, cmp_k_w1, cmp_k_b1, cmp_k_w2, cmp_v_w1, cmp_v_b1, cmp_v_w2, s5_lambda_re, s5_lambda_im, s5_log_dt, s5_b_re, s5_b_im, s5_c_re, s5_c_im, s5_d, s5_w_glu, s5_b_glu, attn_out_norm, ssm_out_norm, w_out, ffn2_norm, ffn2_w_gate, ffn2_w_up, ffn2_w_down, final_norm):
    bsz, t, _ = x.shape
    n = bsz * t
    bf = jnp.bfloat16
    assert ffn1_norm.shape[0] == 1, "the final rmsnorm is fused into the single layer's last kernel"
    assert t % max(TOKEN_TILE, WINDOW) == 0 and x.shape[2] == D_MODEL
    xc = x.reshape(n, D_MODEL)
    for l in range(1):
        win, gb = _reorder_w_in(w_in[l], gate_bias[l])
        x1, q, kvc, kvs, kvw, gates, u_t = _ffn1_proj(
            xc, ffn1_norm[l][None], ffn1_w_gate[l].astype(bf), ffn1_w_up[l].astype(bf),
            ffn1_w_down[l].astype(bf), mix_norm[l][None], win, gb, bsz, t)

        kvc4 = kvc.reshape(2 * N_KV, bsz, t // CMP_STRIDE, CMP_STRIDE * HEAD_DIM)
        pos = jnp.stack([cmp_pos_k[l], cmp_pos_v[l]]).reshape(2, 1, CMP_LEN * HEAD_DIM)
        w1 = jnp.stack([cmp_k_w1[l], cmp_v_w1[l]]).astype(bf)
        b1 = jnp.stack([cmp_k_b1[l], cmp_v_b1[l]])[:, None, :]
        w2 = jnp.stack([cmp_k_w2[l], cmp_v_w2[l]]).astype(bf)
        kvcmp = _compress(kvc4, pos, w1, b1, w2)

        attn = _nsa(q, kvcmp, kvs, kvw, gates, bsz, t)

        bm, cm, ar, ai = _s5_params(s5_lambda_re[l], s5_lambda_im[l], s5_log_dt[l], s5_b_re[l],
                                    s5_b_im[l], s5_c_re[l], s5_c_im[l])
        ssm_t = _s5(u_t.reshape(t * bsz, SSM_WIDTH), bm, cm, ar, ai,
                    s5_d[l].reshape(1, SSM_WIDTH), s5_w_glu[l].astype(bf), s5_b_glu[l][None], bsz)

        xc = _out_ffn2(x1, attn, ssm_t.reshape(t, bsz * SSM_WIDTH), attn_out_norm[l][None],
                       ssm_out_norm[l][None], w_out[l].astype(bf), ffn2_norm[l][None],
                       ffn2_w_gate[l].astype(bf), ffn2_w_up[l].astype(bf), ffn2_w_down[l].astype(bf),
                       final_norm[None], bsz, t)
    return xc.reshape(bsz, t, D_MODEL)
```

```python
import functools
import math

import jax
import jax.numpy as jnp
from jax import lax
from jax.experimental import pallas as pl
from jax.experimental.pallas import tpu as pltpu

D_MODEL = 1024
HEAD_DIM = 64
N_HEADS = 8
N_KV = 2
GQA_R = N_HEADS // N_KV
ATTN_WIDTH = N_HEADS * HEAD_DIM
CMP_LEN = 32
CMP_STRIDE = 16
CMP_HIDDEN = 256
SEL_BLOCK = 64
SEL_SHIFT = 6
SEL_TOPK = 16
WINDOW = 512
Q_BLOCK = 128
SSM_WIDTH = 512
SSM_GROUP = 16
SSM_GROUPS = SSM_WIDTH // SSM_GROUP
SSM_STATE = 64
D_FF = 2816
EPS = 1e-6

LANES = 128
FF_CHUNK = 256
TOKEN_TILE = 512
SEL_KEY_TILE = 256
S5_TIME_CHUNK = 64
S5_LANE_CHUNK = 512
S5_HALF = SSM_WIDTH // 2
S5_HALF_STATE = (SSM_GROUPS // 2) * SSM_STATE
VMEM_LIMIT = 56 * 1024 * 1024
NEG_BIG = -1e30

Q_OFF = 0
KVC_OFF = 512
KVS_OFF = 768
KVW_OFF = 1024
GATE_OFF = 1280
U_OFF = 1536
IN_COLS_PADDED = 2048
N_GATE_COLS = N_HEADS * 3


def _dot(a, b):
    return jnp.dot(a, b, preferred_element_type=jnp.float32)


def _dot_nt(a, b):
    return lax.dot_general(a, b, (((1,), (1,)), ((), ())), preferred_element_type=jnp.float32)


def _rmsnorm(x, g):
    r = lax.rsqrt(jnp.mean(x * x, axis=-1, keepdims=True) + EPS)
    return (x * r) * g


def _swiglu_acc(h, wg_ref, wu_ref, wd_ref):
    acc = jnp.zeros((h.shape[0], D_MODEL), jnp.float32)
    for f in range(0, D_FF, FF_CHUNK):
        gate = _dot(h, wg_ref[:, f:f + FF_CHUNK])
        up = _dot(h, wu_ref[:, f:f + FF_CHUNK])
        act = (gate * jax.nn.sigmoid(gate) * up).astype(jnp.bfloat16)
        acc = acc + _dot(act, wd_ref[f:f + FF_CHUNK, :])
    return acc


def _resident(shape):
    nd = len(shape)
    return pl.BlockSpec(shape, lambda *_: (0,) * nd, pipeline_mode=pl.Buffered(1))


def _ffn1_proj_kernel(x_ref, g1_ref, wg_ref, wu_ref, wd_ref, gm_ref, win_ref, gb_ref,
                      x1_ref, q_ref, kvc_ref, ks_ref, vst_ref, kw_ref, vwt_ref, gate_ref, u_ref):
    x = x_ref[...]
    h = _rmsnorm(x, g1_ref[...]).astype(jnp.bfloat16)
    x1 = x + 0.5 * _swiglu_acc(h, wg_ref, wu_ref, wd_ref)
    x1_ref[...] = x1
    h2 = _rmsnorm(x1, gm_ref[...]).astype(jnp.bfloat16)
    p = _dot(h2, win_ref[...])
    for hh in range(N_HEADS):
        lo = Q_OFF + hh * HEAD_DIM
        q_ref[hh] = (p[:, lo:lo + HEAD_DIM] * (HEAD_DIM ** -0.5)).astype(q_ref.dtype)
    for i in range(2 * N_KV):
        kvc_ref[i] = p[:, KVC_OFF + i * HEAD_DIM:KVC_OFF + (i + 1) * HEAD_DIM]
    for g in range(N_KV):
        for off, k_ref, vt_ref in ((KVS_OFF, ks_ref, vst_ref), (KVW_OFF, kw_ref, vwt_ref)):
            k_lo = off + g * HEAD_DIM
            v_lo = off + (N_KV + g) * HEAD_DIM
            k_ref[g] = p[:, k_lo:k_lo + HEAD_DIM].astype(k_ref.dtype)
            vt_ref[g, 0] = p[:, v_lo:v_lo + HEAD_DIM].T.astype(vt_ref.dtype)
    for g in range(N_KV):
        lo = GATE_OFF + g * LANES
        gate_ref[g] = jax.nn.sigmoid(p[:, lo:lo + LANES] + gb_ref[:, g * LANES:(g + 1) * LANES])
    u_ref[...] = p[:, U_OFF:U_OFF + SSM_WIDTH]


def _ffn1_proj(x2d, g1, wg, wu, wd, gm, win, gb, bsz, t):
    n = bsz * t
    tm = min(TOKEN_TILE, t)
    nt = t // tm
    row = lambda i: (i, 0)
    lead = lambda i: (0, i, 0)
    k_spec = pl.BlockSpec((N_KV, tm, HEAD_DIM), lead)
    vt_spec = pl.BlockSpec((N_KV, 1, HEAD_DIM, tm), lambda i: (0, i // nt, 0, i % nt))
    k_shape = jax.ShapeDtypeStruct((N_KV, n, HEAD_DIM), jnp.bfloat16)
    vt_shape = jax.ShapeDtypeStruct((N_KV, bsz, HEAD_DIM, t), jnp.bfloat16)
    return pl.pallas_call(
        _ffn1_proj_kernel,
        grid=(n // tm,),
        in_specs=[
            pl.BlockSpec((tm, D_MODEL), row),
            _resident((1, D_MODEL)),
            _resident((D_MODEL, D_FF)), _resident((D_MODEL, D_FF)), _resident((D_FF, D_MODEL)),
            _resident((1, D_MODEL)),
            _resident((D_MODEL, IN_COLS_PADDED)),
            _resident((1, N_KV * LANES)),
        ],
        out_specs=[
            pl.BlockSpec((tm, D_MODEL), row),
            pl.BlockSpec((N_HEADS, tm, HEAD_DIM), lead),
            pl.BlockSpec((2 * N_KV, tm, HEAD_DIM), lead),
            k_spec, vt_spec, k_spec, vt_spec,
            pl.BlockSpec((N_KV, tm, LANES), lead),
            pl.BlockSpec((tm, SSM_WIDTH), lambda i: (i % nt, i // nt)),
        ],
        out_shape=[
            jax.ShapeDtypeStruct((n, D_MODEL), jnp.float32),
            jax.ShapeDtypeStruct((N_HEADS, n, HEAD_DIM), jnp.bfloat16),
            jax.ShapeDtypeStruct((2 * N_KV, n, HEAD_DIM), jnp.float32),
            k_shape, vt_shape, k_shape, vt_shape,
            jax.ShapeDtypeStruct((N_KV, n, LANES), jnp.float32),
            jax.ShapeDtypeStruct((t, bsz * SSM_WIDTH), jnp.float32),
        ],
        compiler_params=pltpu.CompilerParams(
            dimension_semantics=("parallel",), vmem_limit_bytes=VMEM_LIMIT),
    )(x2d, g1, wg, wu, wd, gm, win, gb)


def _compress_kernel(h_ref, pos_ref, w1_ref, b1_ref, w2_ref, o_ref, ot_ref):
    half = CMP_STRIDE * HEAD_DIM
    hrows = h_ref[0, 0]
    top = (hrows + pos_ref[0, :, :half]).astype(jnp.bfloat16)
    bot = (hrows + pos_ref[0, :, half:]).astype(jnp.bfloat16)
    a = _dot(top, w1_ref[0, :half, :])
    b = _dot(bot, w1_ref[0, half:, :])
    nrow = hrows.shape[0]
    pre = a + pltpu.roll(b, nrow - 1, 0) + b1_ref[0]
    out = _dot(jax.nn.gelu(pre).astype(jnp.bfloat16), w2_ref[0])
    rid = lax.broadcasted_iota(jnp.int32, out.shape, 0)
    out = jnp.where(rid < nrow - 1, out, 0.0)
    o_ref[0, 0] = out
    ot_ref[0, 0] = out.T


def _compress(kvc4, pos, w1, b1, w2):
    _, bsz, nrow, half = kvc4.shape
    kind = lambda i, b: (i // N_KV, 0, 0)
    return pl.pallas_call(
        _compress_kernel,
        grid=(2 * N_KV, bsz),
        in_specs=[
            pl.BlockSpec((1, 1, nrow, half), lambda i, b: (i, b, 0, 0)),
            pl.BlockSpec((1, 1, CMP_LEN * HEAD_DIM), kind),
            pl.BlockSpec((1, CMP_LEN * HEAD_DIM, CMP_HIDDEN), kind),
            pl.BlockSpec((1, 1, CMP_HIDDEN), kind),
            pl.BlockSpec((1, CMP_HIDDEN, HEAD_DIM), kind),
        ],
        out_specs=[pl.BlockSpec((1, 1, nrow, HEAD_DIM), lambda i, b: (i, b, 0, 0)),
                   pl.BlockSpec((1, 1, HEAD_DIM, nrow), lambda i, b: (i, b, 0, 0))],
        out_shape=[jax.ShapeDtypeStruct((2 * N_KV, bsz, nrow, HEAD_DIM), jnp.float32),
                   jax.ShapeDtypeStruct((2 * N_KV, bsz, HEAD_DIM, nrow), jnp.float32)],
        compiler_params=pltpu.CompilerParams(dimension_semantics=("parallel", "parallel")),
    )(kvc4, pos, w1, b1, w2)


def _split3_bf16(x):
    hi = x.astype(jnp.bfloat16)
    r1 = x - hi.astype(jnp.float32)
    mid = r1.astype(jnp.bfloat16)
    lo = (r1 - mid.astype(jnp.float32)).astype(jnp.bfloat16)
    return hi, mid, lo


def _sweep_t(q, k_ref, vt_ref, n_tiles, tile, mask_fn):
    rows = q.shape[0]

    def body(kt, carry):
        m, l, acc = carry
        start = pl.multiple_of(kt * tile, tile)
        k = k_ref[0, pl.ds(start, tile), :]
        vt = vt_ref[0, 0, :, pl.ds(start, tile)]
        mask = mask_fn(kt)
        s = _dot_nt(k, q)
        s = jnp.concatenate([jnp.where(mask, s[:, r * Q_BLOCK:(r + 1) * Q_BLOCK], NEG_BIG)
                             for r in range(rows // Q_BLOCK)], axis=1)
        m_new = jnp.maximum(m, jnp.max(s, axis=0, keepdims=True))
        alpha = jnp.exp(m - m_new)
        p = jnp.exp(s - m_new)
        l = alpha * l + jnp.sum(p, axis=0, keepdims=True)
        acc = alpha * acc + _dot(vt, p.astype(vt.dtype))
        return m_new, l, acc

    init = (jnp.full((1, rows), NEG_BIG, jnp.float32), jnp.zeros((1, rows), jnp.float32),
            jnp.zeros((HEAD_DIM, rows), jnp.float32))
    _, l, acc = lax.fori_loop(0, n_tiles, body, init)
    return acc / l


def _nsa_kernel(q_ref, kc_ref, vct_ref, ks_ref, vst_ref, kw_ref, vwt_ref, gate_ref, o_ref):
    c = pl.program_id(2)
    rows = GQA_R * Q_BLOCK
    q = q_ref[...].reshape(rows, HEAD_DIM)
    ncp = kc_ref.shape[2]
    t = ks_ref.shape[1]
    nb = t // SEL_BLOCK

    kc = kc_ref[0, 0].astype(jnp.bfloat16)
    vct = vct_ref[0, 0].astype(jnp.bfloat16)
    s_c = _dot_nt(kc, q)
    n_c = lax.broadcasted_iota(jnp.int32, (ncp, rows), 0)
    tq_c = c * Q_BLOCK + (lax.broadcasted_iota(jnp.int32, (ncp, rows), 1) & (Q_BLOCK - 1))
    mask_c = n_c * CMP_STRIDE + (CMP_LEN - 1) <= tq_c
    s_c = jnp.where(mask_c, s_c, -jnp.inf)
    m_c = jnp.max(s_c, axis=0, keepdims=True)
    m_c = jnp.where(m_c > -jnp.inf, m_c, 0.0)
    e_c = jnp.where(mask_c, jnp.exp(s_c - m_c), 0.0)
    p_c = e_c / jnp.maximum(jnp.sum(e_c, axis=0, keepdims=True), 1e-30)
    o_cmp = _dot(vct, p_c.astype(jnp.bfloat16))

    p_sum = p_c[:, 0:Q_BLOCK]
    for r in range(1, GQA_R):
        p_sum = p_sum + p_c[:, r * Q_BLOCK:(r + 1) * Q_BLOCK]
    jb = lax.broadcasted_iota(jnp.int32, (nb, ncp), 0)
    nn = lax.broadcasted_iota(jnp.int32, (nb, ncp), 1)
    overlap_t = ((nn * CMP_STRIDE < jb * SEL_BLOCK + SEL_BLOCK)
                 & (nn * CMP_STRIDE + CMP_LEN > jb * SEL_BLOCK)
                 & (nn < ncp - 1))
    overlap_t = jnp.where(overlap_t, 1.0, 0.0).astype(jnp.bfloat16)
    hi, mid, lo = _split3_bf16(p_sum)
    imp = _dot(overlap_t, hi) + _dot(overlap_t, mid) + _dot(overlap_t, lo)

    j_t = lax.broadcasted_iota(jnp.int32, (nb, Q_BLOCK), 0)
    tq_t = c * Q_BLOCK + lax.broadcasted_iota(jnp.int32, (nb, Q_BLOCK), 1)
    cur = tq_t >> SEL_SHIFT
    forced = (j_t == 0) | (j_t == cur) | (j_t == cur - 1)
    valid = j_t * SEL_BLOCK <= tq_t
    score = jnp.where(forced, jnp.inf, jnp.where(valid, imp, -jnp.inf))
    rank = jnp.zeros((nb, Q_BLOCK), jnp.float32)
    for jp in range(nb):
        row = score[jp:jp + 1, :]
        beats = (row > score) | ((row == score) & (j_t > jp))
        rank = rank + jnp.where(beats, 1.0, 0.0)
    sel = jnp.where(rank < min(SEL_TOPK, nb), 1.0, 0.0).astype(jnp.bfloat16)

    kpos_s = lax.broadcasted_iota(jnp.int32, (SEL_KEY_TILE, Q_BLOCK), 0)
    tq_s = c * Q_BLOCK + lax.broadcasted_iota(jnp.int32, (SEL_KEY_TILE, Q_BLOCK), 1)
    ek = lax.broadcasted_iota(jnp.int32, (SEL_KEY_TILE, nb), 0)
    eb = lax.broadcasted_iota(jnp.int32, (SEL_KEY_TILE, nb), 1)
    blocks_per_tile = SEL_KEY_TILE // SEL_BLOCK

    def sel_mask(kt):
        expand = jnp.where(eb == kt * blocks_per_tile + (ek >> SEL_SHIFT), 1.0, 0.0).astype(jnp.bfloat16)
        chosen = _dot(expand, sel) > 0.5
        return chosen & (kt * SEL_KEY_TILE + kpos_s <= tq_s)

    n_sel_tiles = ((c + 1) * Q_BLOCK + SEL_KEY_TILE - 1) // SEL_KEY_TILE
    o_slc = _sweep_t(q, ks_ref, vst_ref, n_sel_tiles, SEL_KEY_TILE, sel_mask)

    band = WINDOW + Q_BLOCK
    w0 = pl.multiple_of(jnp.maximum(c * Q_BLOCK - WINDOW, 0), Q_BLOCK)
    kw = kw_ref[0, pl.ds(w0, band), :]
    vwt = vwt_ref[0, 0, :, pl.ds(w0, band)]
    kpos_w = w0 + lax.broadcasted_iota(jnp.int32, (band, Q_BLOCK), 0)
    tq_w = c * Q_BLOCK + lax.broadcasted_iota(jnp.int32, (band, Q_BLOCK), 1)
    mask_w = (kpos_w <= tq_w) & (kpos_w > tq_w - WINDOW)
    s_w = _dot_nt(kw, q)
    s_w = jnp.concatenate([jnp.where(mask_w, s_w[:, r * Q_BLOCK:(r + 1) * Q_BLOCK], NEG_BIG)
                           for r in range(GQA_R)], axis=1)
    p_w = jnp.exp(s_w - jnp.max(s_w, axis=0, keepdims=True))
    o_win = _dot(vwt, p_w.astype(jnp.bfloat16)) / jnp.sum(p_w, axis=0, keepdims=True)

    gt = gate_ref[0].T
    outs = []
    for r in range(GQA_R):
        sl = slice(r * Q_BLOCK, (r + 1) * Q_BLOCK)
        o_t = (gt[3 * r:3 * r + 1] * o_cmp[:, sl] + gt[3 * r + 1:3 * r + 2] * o_slc[:, sl]
               + gt[3 * r + 2:3 * r + 3] * o_win[:, sl])
        outs.append(o_t.T)
    o_ref[...] = jnp.concatenate(outs, axis=-1)


def _nsa(q, kcmp, kcmp_t, ks, vst, kw, vwt, gates, bsz, t):
    nq = t // Q_BLOCK
    ncp = kcmp.shape[2]
    k_spec = pl.BlockSpec((1, t, HEAD_DIM), lambda b, g, c: (g, b, 0))
    vt_spec = pl.BlockSpec((1, 1, HEAD_DIM, t), lambda b, g, c: (g, b, 0, 0))
    return pl.pallas_call(
        _nsa_kernel,
        grid=(bsz, N_KV, nq),
        in_specs=[
            pl.BlockSpec((GQA_R, Q_BLOCK, HEAD_DIM), lambda b, g, c: (g, b * nq + c, 0)),
            pl.BlockSpec((1, 1, ncp, HEAD_DIM), lambda b, g, c: (g, b, 0, 0)),
            pl.BlockSpec((1, 1, HEAD_DIM, ncp), lambda b, g, c: (N_KV + g, b, 0, 0)),
            k_spec, vt_spec, k_spec, vt_spec,
            pl.BlockSpec((1, Q_BLOCK, LANES), lambda b, g, c: (g, b * nq + c, 0)),
        ],
        out_specs=pl.BlockSpec((Q_BLOCK, GQA_R * HEAD_DIM), lambda b, g, c: (b * nq + c, g)),
        out_shape=jax.ShapeDtypeStruct((bsz * t, ATTN_WIDTH), jnp.float32),
        compiler_params=pltpu.CompilerParams(
            dimension_semantics=("parallel", "parallel", "parallel")),
    )(q, kcmp, kcmp_t, ks, vst, kw, vwt, gates)


def _s5_kernel(u_ref, bm_ref, cm_ref, ar_ref, ai_ref, d_ref, wglu_ref, bglu_ref, o_ref,
               bu_ref, state_ref, *, bsz):
    steps = u_ref.shape[0] // bsz

    @pl.when(pl.program_id(0) == 0)
    def _():
        state_ref[...] = jnp.zeros_like(state_ref)

    u = u_ref[...]
    ub = u.astype(jnp.bfloat16)
    ys = []
    for half in range(2):
        bu_ref[...] = _dot(ub[:, half * S5_HALF:(half + 1) * S5_HALF], bm_ref[half])
        for lc in range(0, S5_HALF_STATE, S5_LANE_CHUNK):
            re = slice(lc, lc + S5_LANE_CHUNK)
            im = slice(S5_HALF_STATE + lc, S5_HALF_STATE + lc + S5_LANE_CHUNK)
            a_re = jnp.broadcast_to(ar_ref[half, :, re], (bsz, S5_LANE_CHUNK))
            a_im = jnp.broadcast_to(ai_ref[half, :, re], (bsz, S5_LANE_CHUNK))

            def step(i, carry):
                xr, xi = carry
                rows = pl.ds(pl.multiple_of(i * bsz, bsz), bsz)
                nr = a_re * xr - a_im * xi + bu_ref[rows, re]
                ni = a_re * xi + a_im * xr + bu_ref[rows, im]
                bu_ref[rows, re] = nr
                bu_ref[rows, im] = ni
                return nr, ni

            xr, xi = lax.fori_loop(0, steps, step, (state_ref[half, :, re], state_ref[half, :, im]),
                                   unroll=4)
            state_ref[half, :, re] = xr
            state_ref[half, :, im] = xi
        ys.append(_dot(bu_ref[...].astype(jnp.bfloat16), cm_ref[half]))
    y = jnp.concatenate(ys, axis=-1) + d_ref[...] * u
    y = jax.nn.gelu(y)
    o_ref[...] = y * jax.nn.sigmoid(_dot(y.astype(jnp.bfloat16), wglu_ref[...]) + bglu_ref[...])


def _s5(u_tb, bm, cm, ar, ai, d, wglu, bglu, bsz):
    n = u_tb.shape[0]
    rows = S5_TIME_CHUNK * bsz
    const3 = lambda i: (0, 0, 0)
    const2 = lambda i: (0, 0)
    return pl.pallas_call(
        functools.partial(_s5_kernel, bsz=bsz),
        grid=(n // rows,),
        in_specs=[
            pl.BlockSpec((rows, SSM_WIDTH), lambda i: (i, 0)),
            pl.BlockSpec((2, S5_HALF, 2 * S5_HALF_STATE), const3),
            pl.BlockSpec((2, 2 * S5_HALF_STATE, S5_HALF), const3),
            pl.BlockSpec((2, 1, S5_HALF_STATE), const3),
            pl.BlockSpec((2, 1, S5_HALF_STATE), const3),
            pl.BlockSpec((1, SSM_WIDTH), const2),
            pl.BlockSpec((SSM_WIDTH, SSM_WIDTH), const2),
            pl.BlockSpec((1, SSM_WIDTH), const2),
        ],
        out_specs=pl.BlockSpec((rows, SSM_WIDTH), lambda i: (i, 0)),
        out_shape=jax.ShapeDtypeStruct((n, SSM_WIDTH), jnp.float32),
        scratch_shapes=[
            pltpu.VMEM((rows, 2 * S5_HALF_STATE), jnp.float32),
            pltpu.VMEM((2, bsz, 2 * S5_HALF_STATE), jnp.float32),
        ],
        compiler_params=pltpu.CompilerParams(
            dimension_semantics=("arbitrary",), vmem_limit_bytes=VMEM_LIMIT),
    )(u_tb, bm, cm, ar, ai, d, wglu, bglu)


def _s5_params(lam_re, lam_im, log_dt, b_re, b_im, c_re, c_im):
    dt = jnp.exp(log_dt)[:, None]
    mag = jnp.exp(lam_re * dt)
    ab_re = mag * jnp.cos(lam_im * dt)
    ab_im = mag * jnp.sin(lam_im * dt)
    nr = ab_re - 1.0
    den = lam_re * lam_re + lam_im * lam_im
    f_re = (nr * lam_re + ab_im * lam_im) / den
    f_im = (ab_im * lam_re - nr * lam_im) / den
    bb_re = f_re[..., None] * b_re - f_im[..., None] * b_im
    bb_im = f_re[..., None] * b_im + f_im[..., None] * b_re
    gh = SSM_GROUPS // 2
    eye = jnp.eye(gh, dtype=jnp.float32)

    def in_half(bb):
        return jnp.einsum('gph,gk->ghkp', bb, eye).reshape(S5_HALF, S5_HALF_STATE)

    def out_half(cc):
        return jnp.einsum('ghp,gk->gpkh', cc, eye).reshape(S5_HALF_STATE, S5_HALF)

    bm = jnp.stack([jnp.concatenate([in_half(bb_re[s]), in_half(bb_im[s])], axis=1)
                    for s in (slice(0, gh), slice(gh, None))])
    cm = jnp.stack([jnp.concatenate([out_half(c_re[s]), -out_half(c_im[s])], axis=0)
                    for s in (slice(0, gh), slice(gh, None))])
    ar = ab_re.reshape(2, 1, S5_HALF_STATE)
    ai = ab_im.reshape(2, 1, S5_HALF_STATE)
    return bm.astype(jnp.bfloat16), cm.astype(jnp.bfloat16), ar, ai


def _out_ffn2_kernel(x1_ref, a_ref, s_ref, ga_ref, gs_ref, wo_ref, g2_ref, wg_ref, wu_ref, wd_ref,
                     gf_ref, o_ref):
    a = _rmsnorm(a_ref[...], ga_ref[...]).astype(jnp.bfloat16)
    s = _rmsnorm(s_ref[...], gs_ref[...]).astype(jnp.bfloat16)
    x2 = x1_ref[...] + _dot(a, wo_ref[:ATTN_WIDTH, :]) + _dot(s, wo_ref[ATTN_WIDTH:, :])
    h = _rmsnorm(x2, g2_ref[...]).astype(jnp.bfloat16)
    x3 = x2 + 0.5 * _swiglu_acc(h, wg_ref, wu_ref, wd_ref)
    o_ref[...] = _rmsnorm(x3, gf_ref[...])


def _out_ffn2(x1, attn, ssm_t, ga, gs, wo, g2, wg, wu, wd, gf, bsz, t):
    n = bsz * t
    tm = min(TOKEN_TILE, t)
    nt = t // tm
    row = lambda i: (i, 0)
    return pl.pallas_call(
        _out_ffn2_kernel,
        grid=(n // tm,),
        in_specs=[
            pl.BlockSpec((tm, D_MODEL), row),
            pl.BlockSpec((tm, ATTN_WIDTH), row),
            pl.BlockSpec((tm, SSM_WIDTH), lambda i: (i % nt, i // nt)),
            _resident((1, ATTN_WIDTH)), _resident((1, SSM_WIDTH)),
            _resident((D_MODEL, D_MODEL)),
            _resident((1, D_MODEL)),
            _resident((D_MODEL, D_FF)), _resident((D_MODEL, D_FF)), _resident((D_FF, D_MODEL)),
            _resident((1, D_MODEL)),
        ],
        out_specs=pl.BlockSpec((tm, D_MODEL), row),
        out_shape=jax.ShapeDtypeStruct((n, D_MODEL), jnp.float32),
        compiler_params=pltpu.CompilerParams(
            dimension_semantics=("parallel",), vmem_limit_bytes=VMEM_LIMIT),
    )(x1, attn, ssm_t, ga, gs, wo, g2, wg, wu, wd, gf)


def _reorder_w_in(w_in, gate_bias):
    per_g = N_GATE_COLS // N_KV
    g_lo = GATE_OFF
    pieces = [w_in[:, :g_lo]]
    bias = []
    gb = gate_bias.reshape(1, N_GATE_COLS)
    for g in range(N_KV):
        cols = w_in[:, g_lo + g * per_g:g_lo + (g + 1) * per_g]
        pieces.append(jnp.pad(cols, ((0, 0), (0, LANES - per_g))))
        bias.append(jnp.pad(gb[:, g * per_g:(g + 1) * per_g], ((0, 0), (0, LANES - per_g))))
    pieces.append(w_in[:, g_lo + N_GATE_COLS:])
    return jnp.concatenate(pieces, axis=1).astype(jnp.bfloat16), jnp.concatenate(bias, axis=1)


def kernel(x, ffn1_norm, ffn1_w_gate, ffn1_w_up, ffn1_w_down, mix_norm, w_in, gate_bias, cmp_pos_k, cmp_pos_v, cmp_k_w1, cmp_k_b1, cmp_k_w2, cmp_v_w1, cmp_v_b1, cmp_v_w2, s5_lambda_re, s5_lambda_im, s5_log_dt, s5_b_re, s5_b_im, s5_c_re, s5_c_im, s5_d, s5_w_glu, s5_b_glu, attn_out_norm, ssm_out_norm, w_out, ffn2_norm, ffn2_w_gate, ffn2_w_up, ffn2_w_down, final_norm):
    bsz, t, _ = x.shape
    n = bsz * t
    bf = jnp.bfloat16
    assert ffn1_norm.shape[0] == 1, "the final rmsnorm is fused into the single layer's last kernel"
    assert t % max(TOKEN_TILE, WINDOW) == 0 and t >= WINDOW + Q_BLOCK and x.shape[2] == D_MODEL
    xc = x.reshape(n, D_MODEL)
    for l in range(1):
        win, gb = _reorder_w_in(w_in[l], gate_bias[l])
        x1, q, kvc, ks, vst, kw, vwt, gates, u_t = _ffn1_proj(
            xc, ffn1_norm[l][None], ffn1_w_gate[l].astype(bf), ffn1_w_up[l].astype(bf),
            ffn1_w_down[l].astype(bf), mix_norm[l][None], win, gb, bsz, t)

        kvc4 = kvc.reshape(2 * N_KV, bsz, t // CMP_STRIDE, CMP_STRIDE * HEAD_DIM)
        pos = jnp.stack([cmp_pos_k[l], cmp_pos_v[l]]).reshape(2, 1, CMP_LEN * HEAD_DIM)
        w1 = jnp.stack([cmp_k_w1[l], cmp_v_w1[l]]).astype(bf)
        b1 = jnp.stack([cmp_k_b1[l], cmp_v_b1[l]])[:, None, :]
        w2 = jnp.stack([cmp_k_w2[l], cmp_v_w2[l]]).astype(bf)
        kcmp, kcmp_t = _compress(kvc4, pos, w1, b1, w2)

        attn = _nsa(q, kcmp, kcmp_t, ks, vst, kw, vwt, gates, bsz, t)

        bm, cm, ar, ai = _s5_params(s5_lambda_re[l], s5_lambda_im[l], s5_log_dt[l], s5_b_re[l],
                                    s5_b_im[l], s5_c_re[l], s5_c_im[l])
        ssm_t = _s5(u_t.reshape(t * bsz, SSM_WIDTH), bm, cm, ar, ai,
                    s5_d[l].reshape(1, SSM_WIDTH), s5_w_glu[l].astype(bf), s5_b_glu[l][None], bsz)

        xc = _out_ffn2(x1, attn, ssm_t.reshape(t, bsz * SSM_WIDTH), attn_out_norm[l][None],
                       ssm_out_norm[l][None], w_out[l].astype(bf), ffn2_norm[l][None],
                       ffn2_w_gate[l].astype(bf), ffn2_w_up[l].astype(bf), ffn2_w_down[l].astype(bf),
                       final_norm[None], bsz, t)
    return xc.reshape(bsz, t, D_MODEL)
```

```python
import functools
import math

import jax
import jax.numpy as jnp
from jax import lax
from jax.experimental import pallas as pl
from jax.experimental.pallas import tpu as pltpu

D_MODEL = 1024
HEAD_DIM = 64
N_HEADS = 8
N_KV = 2
GQA_R = N_HEADS // N_KV
ATTN_WIDTH = N_HEADS * HEAD_DIM
CMP_LEN = 32
CMP_STRIDE = 16
CMP_HIDDEN = 256
SEL_BLOCK = 64
SEL_SHIFT = 6
SEL_TOPK = 16
WINDOW = 512
Q_BLOCK = 128
SSM_WIDTH = 512
SSM_GROUP = 16
SSM_GROUPS = SSM_WIDTH // SSM_GROUP
SSM_STATE = 64
D_FF = 2816
EPS = 1e-6

LANES = 128
FF_CHUNK = 256
TOKEN_TILE = 512
SEL_KEY_TILE = 256
S5_TIME_CHUNK = 64
S5_LANE_CHUNK = 512
S5_HALF = SSM_WIDTH // 2
S5_HALF_STATE = (SSM_GROUPS // 2) * SSM_STATE
VMEM_LIMIT = 56 * 1024 * 1024
NEG_BIG = -1e30
Q_SCALE = HEAD_DIM ** -0.5 * math.log2(math.e)

Q_OFF = 0
KVC_OFF = 512
KVS_OFF = 768
KVW_OFF = 1024
GATE_OFF = 1280
U_OFF = 1536
IN_COLS_PADDED = 2048
N_GATE_COLS = N_HEADS * 3


def _dot(a, b):
    return jnp.dot(a, b, preferred_element_type=jnp.float32)


def _dot_nt(a, b):
    return lax.dot_general(a, b, (((1,), (1,)), ((), ())), preferred_element_type=jnp.float32)


def _rmsnorm(x, g):
    r = lax.rsqrt(jnp.mean(x * x, axis=-1, keepdims=True) + EPS)
    return (x * r) * g


def _swiglu_acc(h, wg_ref, wu_ref, wd_ref):
    acc = jnp.zeros((h.shape[0], D_MODEL), jnp.float32)
    for f in range(0, D_FF, FF_CHUNK):
        gate = _dot(h, wg_ref[:, f:f + FF_CHUNK])
        up = _dot(h, wu_ref[:, f:f + FF_CHUNK])
        act = (gate * jax.nn.sigmoid(gate) * up).astype(jnp.bfloat16)
        acc = acc + _dot(act, wd_ref[f:f + FF_CHUNK, :])
    return acc


def _resident(shape):
    nd = len(shape)
    return pl.BlockSpec(shape, lambda *_: (0,) * nd, pipeline_mode=pl.Buffered(1))


def _ffn1_proj_kernel(x_ref, g1_ref, wg_ref, wu_ref, wd_ref, gm_ref, win_ref, gb_ref,
                      x1_ref, q_ref, kvc_ref, ks_ref, vst_ref, kw_ref, vwt_ref, gate_ref, u_ref):
    x = x_ref[...]
    h = _rmsnorm(x, g1_ref[...]).astype(jnp.bfloat16)
    x1 = x + 0.5 * _swiglu_acc(h, wg_ref, wu_ref, wd_ref)
    x1_ref[...] = x1
    h2 = _rmsnorm(x1, gm_ref[...]).astype(jnp.bfloat16)
    p = _dot(h2, win_ref[...])
    for hh in range(N_HEADS):
        lo = Q_OFF + hh * HEAD_DIM
        q_ref[hh] = (p[:, lo:lo + HEAD_DIM] * Q_SCALE).astype(q_ref.dtype)
    for i in range(2 * N_KV):
        kvc_ref[i] = p[:, KVC_OFF + i * HEAD_DIM:KVC_OFF + (i + 1) * HEAD_DIM]
    for g in range(N_KV):
        for off, k_ref, vt_ref in ((KVS_OFF, ks_ref, vst_ref), (KVW_OFF, kw_ref, vwt_ref)):
            k_lo = off + g * HEAD_DIM
            v_lo = off + (N_KV + g) * HEAD_DIM
            k_ref[g] = p[:, k_lo:k_lo + HEAD_DIM].astype(k_ref.dtype)
            vt_ref[g, 0] = p[:, v_lo:v_lo + HEAD_DIM].T.astype(vt_ref.dtype)
    for g in range(N_KV):
        lo = GATE_OFF + g * LANES
        gate_ref[g] = jax.nn.sigmoid(p[:, lo:lo + LANES] + gb_ref[:, g * LANES:(g + 1) * LANES])
    u_ref[...] = p[:, U_OFF:U_OFF + SSM_WIDTH]


def _ffn1_proj(x2d, g1, wg, wu, wd, gm, win, gb, bsz, t):
    n = bsz * t
    tm = min(TOKEN_TILE, t)
    nt = t // tm
    row = lambda i: (i, 0)
    lead = lambda i: (0, i, 0)
    k_spec = pl.BlockSpec((N_KV, tm, HEAD_DIM), lead)
    vt_spec = pl.BlockSpec((N_KV, 1, HEAD_DIM, tm), lambda i: (0, i // nt, 0, i % nt))
    k_shape = jax.ShapeDtypeStruct((N_KV, n, HEAD_DIM), jnp.bfloat16)
    vt_shape = jax.ShapeDtypeStruct((N_KV, bsz, HEAD_DIM, t), jnp.bfloat16)
    return pl.pallas_call(
        _ffn1_proj_kernel,
        grid=(n // tm,),
        in_specs=[
            pl.BlockSpec((tm, D_MODEL), row),
            _resident((1, D_MODEL)),
            _resident((D_MODEL, D_FF)), _resident((D_MODEL, D_FF)), _resident((D_FF, D_MODEL)),
            _resident((1, D_MODEL)),
            _resident((D_MODEL, IN_COLS_PADDED)),
            _resident((1, N_KV * LANES)),
        ],
        out_specs=[
            pl.BlockSpec((tm, D_MODEL), row),
            pl.BlockSpec((N_HEADS, tm, HEAD_DIM), lead),
            pl.BlockSpec((2 * N_KV, tm, HEAD_DIM), lead),
            k_spec, vt_spec, k_spec, vt_spec,
            pl.BlockSpec((N_KV, tm, LANES), lead),
            pl.BlockSpec((tm, SSM_WIDTH), lambda i: (i % nt, i // nt)),
        ],
        out_shape=[
            jax.ShapeDtypeStruct((n, D_MODEL), jnp.float32),
            jax.ShapeDtypeStruct((N_HEADS, n, HEAD_DIM), jnp.bfloat16),
            jax.ShapeDtypeStruct((2 * N_KV, n, HEAD_DIM), jnp.float32),
            k_shape, vt_shape, k_shape, vt_shape,
            jax.ShapeDtypeStruct((N_KV, n, LANES), jnp.float32),
            jax.ShapeDtypeStruct((t, bsz * SSM_WIDTH), jnp.float32),
        ],
        compiler_params=pltpu.CompilerParams(
            dimension_semantics=("parallel",), vmem_limit_bytes=VMEM_LIMIT),
    )(x2d, g1, wg, wu, wd, gm, win, gb)


def _compress_kernel(h_ref, pos_ref, w1_ref, b1_ref, w2_ref, o_ref, ot_ref):
    half = CMP_STRIDE * HEAD_DIM
    hrows = h_ref[0, 0]
    top = (hrows + pos_ref[0, :, :half]).astype(jnp.bfloat16)
    bot = (hrows + pos_ref[0, :, half:]).astype(jnp.bfloat16)
    a = _dot(top, w1_ref[0, :half, :])
    b = _dot(bot, w1_ref[0, half:, :])
    nrow = hrows.shape[0]
    pre = a + pltpu.roll(b, nrow - 1, 0) + b1_ref[0]
    out = _dot(jax.nn.gelu(pre).astype(jnp.bfloat16), w2_ref[0])
    rid = lax.broadcasted_iota(jnp.int32, out.shape, 0)
    out = jnp.where(rid < nrow - 1, out, 0.0)
    o_ref[0, 0] = out
    ot_ref[0, 0] = out.T


def _compress(kvc4, pos, w1, b1, w2):
    _, bsz, nrow, half = kvc4.shape
    kind = lambda i, b: (i // N_KV, 0, 0)
    return pl.pallas_call(
        _compress_kernel,
        grid=(2 * N_KV, bsz),
        in_specs=[
            pl.BlockSpec((1, 1, nrow, half), lambda i, b: (i, b, 0, 0)),
            pl.BlockSpec((1, 1, CMP_LEN * HEAD_DIM), kind),
            pl.BlockSpec((1, CMP_LEN * HEAD_DIM, CMP_HIDDEN), kind),
            pl.BlockSpec((1, 1, CMP_HIDDEN), kind),
            pl.BlockSpec((1, CMP_HIDDEN, HEAD_DIM), kind),
        ],
        out_specs=[pl.BlockSpec((1, 1, nrow, HEAD_DIM), lambda i, b: (i, b, 0, 0)),
                   pl.BlockSpec((1, 1, HEAD_DIM, nrow), lambda i, b: (i, b, 0, 0))],
        out_shape=[jax.ShapeDtypeStruct((2 * N_KV, bsz, nrow, HEAD_DIM), jnp.float32),
                   jax.ShapeDtypeStruct((2 * N_KV, bsz, HEAD_DIM, nrow), jnp.float32)],
        compiler_params=pltpu.CompilerParams(dimension_semantics=("parallel", "parallel")),
    )(kvc4, pos, w1, b1, w2)


def _split3_bf16(x):
    hi = x.astype(jnp.bfloat16)
    r1 = x - hi.astype(jnp.float32)
    mid = r1.astype(jnp.bfloat16)
    lo = (r1 - mid.astype(jnp.float32)).astype(jnp.bfloat16)
    return hi, mid, lo


def _sweep_t(q, k_ref, vt_ref, n_tiles, tile, bias_fn, s_refs, p_refs):
    rows = q.shape[0]
    last_tile = k_ref.shape[1] // tile - 1

    def k_tile(i):
        return k_ref[0, pl.ds(pl.multiple_of(i * tile, tile), tile), :]

    def vt_tile(i):
        return vt_ref[0, 0, :, pl.ds(pl.multiple_of(i * tile, tile), tile)]

    def scores_into(i, s_ref):
        s = _dot_nt(k_tile(i), q)
        bias = bias_fn(i)
        s = jnp.concatenate([s[:, r * Q_BLOCK:(r + 1) * Q_BLOCK] + bias
                             for r in range(rows // Q_BLOCK)], axis=1)
        s_ref[...] = s
        return jnp.max(s, axis=0, keepdims=True)

    def phase(i, carry, s_cur, s_nxt, p_cur, p_prv):
        m, l, acc, s_max = carry
        s_max_next = scores_into(jnp.minimum(i + 1, last_tile), s_nxt)
        acc = acc + _dot(vt_tile(jnp.maximum(i - 1, 0)), p_prv[...])
        m_new = jnp.maximum(m, s_max)
        alpha = jnp.exp2(m - m_new)
        p = jnp.exp2(s_cur[...] - m_new)
        p_cur[...] = p.astype(p_cur.dtype)
        l = alpha * l + jnp.sum(p, axis=0, keepdims=True)
        return m_new, l, alpha * acc, s_max_next

    def body(j, carry):
        carry = phase(2 * j, carry, s_refs[0], s_refs[1], p_refs[0], p_refs[1])
        return phase(2 * j + 1, carry, s_refs[1], s_refs[0], p_refs[1], p_refs[0])

    p_refs[1][...] = jnp.zeros_like(p_refs[1])
    init = (jnp.full((1, rows), NEG_BIG, jnp.float32), jnp.zeros((1, rows), jnp.float32),
            jnp.zeros((HEAD_DIM, rows), jnp.float32), scores_into(0, s_refs[0]))
    n_pairs = (n_tiles + 1) // 2
    _, l, acc, _ = lax.fori_loop(0, n_pairs, body, init)
    return (acc + _dot(vt_tile(2 * n_pairs - 1), p_refs[1][...])) / l


def _nsa_kernel(q_ref, kc_ref, vct_ref, ks_ref, vst_ref, kw_ref, vwt_ref, gate_ref, o_ref,
                bias_ref, s0_ref, s1_ref, p0_ref, p1_ref):
    c = pl.program_id(2)
    rows = GQA_R * Q_BLOCK
    q = q_ref[...].reshape(rows, HEAD_DIM)
    ncp = kc_ref.shape[2]
    t = ks_ref.shape[1]
    nb = t // SEL_BLOCK

    kc = kc_ref[0, 0].astype(jnp.bfloat16)
    vct = vct_ref[0, 0].astype(jnp.bfloat16)
    s_c = _dot_nt(kc, q)
    n_c = lax.broadcasted_iota(jnp.int32, (ncp, rows), 0)
    tq_c = c * Q_BLOCK + (lax.broadcasted_iota(jnp.int32, (ncp, rows), 1) & (Q_BLOCK - 1))
    mask_c = n_c * CMP_STRIDE + (CMP_LEN - 1) <= tq_c
    s_c = jnp.where(mask_c, s_c, -jnp.inf)
    m_c = jnp.max(s_c, axis=0, keepdims=True)
    m_c = jnp.where(m_c > -jnp.inf, m_c, 0.0)
    e_c = jnp.where(mask_c, jnp.exp2(s_c - m_c), 0.0)
    p_c = e_c / jnp.maximum(jnp.sum(e_c, axis=0, keepdims=True), 1e-30)
    o_cmp = _dot(vct, p_c.astype(jnp.bfloat16))

    p_sum = p_c[:, 0:Q_BLOCK]
    for r in range(1, GQA_R):
        p_sum = p_sum + p_c[:, r * Q_BLOCK:(r + 1) * Q_BLOCK]
    jb = lax.broadcasted_iota(jnp.int32, (nb, ncp), 0)
    nn = lax.broadcasted_iota(jnp.int32, (nb, ncp), 1)
    overlap_t = ((nn * CMP_STRIDE < jb * SEL_BLOCK + SEL_BLOCK)
                 & (nn * CMP_STRIDE + CMP_LEN > jb * SEL_BLOCK)
                 & (nn < ncp - 1))
    overlap_t = jnp.where(overlap_t, 1.0, 0.0).astype(jnp.bfloat16)
    hi, mid, lo = _split3_bf16(p_sum)
    imp = _dot(overlap_t, hi) + _dot(overlap_t, mid) + _dot(overlap_t, lo)

    j_t = lax.broadcasted_iota(jnp.int32, (nb, Q_BLOCK), 0)
    tq_t = c * Q_BLOCK + lax.broadcasted_iota(jnp.int32, (nb, Q_BLOCK), 1)
    cur = tq_t >> SEL_SHIFT
    forced = (j_t == 0) | (j_t == cur) | (j_t == cur - 1)
    valid = j_t * SEL_BLOCK <= tq_t
    score = jnp.where(forced, jnp.inf, jnp.where(valid, imp, -jnp.inf))
    rank = jnp.zeros((nb, Q_BLOCK), jnp.float32)
    for jp in range(nb):
        row = score[jp:jp + 1, :]
        beats = (row > score) | ((row == score) & (j_t > jp))
        rank = rank + jnp.where(beats, 1.0, 0.0)
    bias_ref[...] = jnp.where(rank < min(SEL_TOPK, nb), 0.0, NEG_BIG)

    kpos_s = lax.broadcasted_iota(jnp.int32, (SEL_KEY_TILE, Q_BLOCK), 0)
    tq_s = c * Q_BLOCK + lax.broadcasted_iota(jnp.int32, (SEL_KEY_TILE, Q_BLOCK), 1)
    blocks_per_tile = SEL_KEY_TILE // SEL_BLOCK

    def sel_bias(kt):
        blocks = [jnp.broadcast_to(bias_ref[pl.ds(kt * blocks_per_tile + jj, 1), :], (SEL_BLOCK, Q_BLOCK))
                  for jj in range(blocks_per_tile)]
        return jnp.where(kt * SEL_KEY_TILE + kpos_s <= tq_s, jnp.concatenate(blocks, axis=0), NEG_BIG)

    n_sel_tiles = ((c + 1) * Q_BLOCK + SEL_KEY_TILE - 1) // SEL_KEY_TILE
    o_slc = _sweep_t(q, ks_ref, vst_ref, n_sel_tiles, SEL_KEY_TILE, sel_bias,
                     (s0_ref, s1_ref), (p0_ref, p1_ref))

    band = WINDOW + Q_BLOCK
    w0 = pl.multiple_of(jnp.maximum(c * Q_BLOCK - WINDOW, 0), Q_BLOCK)
    kw = kw_ref[0, pl.ds(w0, band), :]
    vwt = vwt_ref[0, 0, :, pl.ds(w0, band)]
    kpos_w = w0 + lax.broadcasted_iota(jnp.int32, (band, Q_BLOCK), 0)
    tq_w = c * Q_BLOCK + lax.broadcasted_iota(jnp.int32, (band, Q_BLOCK), 1)
    mask_w = (kpos_w <= tq_w) & (kpos_w > tq_w - WINDOW)
    s_w = _dot_nt(kw, q)
    s_w = jnp.concatenate([jnp.where(mask_w, s_w[:, r * Q_BLOCK:(r + 1) * Q_BLOCK], NEG_BIG)
                           for r in range(GQA_R)], axis=1)
    p_w = jnp.exp2(s_w - jnp.max(s_w, axis=0, keepdims=True))
    o_win = _dot(vwt, p_w.astype(jnp.bfloat16)) / jnp.sum(p_w, axis=0, keepdims=True)

    gt = gate_ref[0].T
    outs = []
    for r in range(GQA_R):
        sl = slice(r * Q_BLOCK, (r + 1) * Q_BLOCK)
        o_t = (gt[3 * r:3 * r + 1] * o_cmp[:, sl] + gt[3 * r + 1:3 * r + 2] * o_slc[:, sl]
               + gt[3 * r + 2:3 * r + 3] * o_win[:, sl])
        outs.append(o_t.T)
    o_ref[...] = jnp.concatenate(outs, axis=-1)


def _nsa(q, kcmp, kcmp_t, ks, vst, kw, vwt, gates, bsz, t):
    nq = t // Q_BLOCK
    ncp = kcmp.shape[2]
    rows = GQA_R * Q_BLOCK
    assert t % (2 * SEL_KEY_TILE) == 0, "the selected-block sweep runs pairs of key tiles"
    k_spec = pl.BlockSpec((1, t, HEAD_DIM), lambda b, g, c: (g, b, 0))
    vt_spec = pl.BlockSpec((1, 1, HEAD_DIM, t), lambda b, g, c: (g, b, 0, 0))
    return pl.pallas_call(
        _nsa_kernel,
        grid=(bsz, N_KV, nq),
        in_specs=[
            pl.BlockSpec((GQA_R, Q_BLOCK, HEAD_DIM), lambda b, g, c: (g, b * nq + c, 0)),
            pl.BlockSpec((1, 1, ncp, HEAD_DIM), lambda b, g, c: (g, b, 0, 0)),
            pl.BlockSpec((1, 1, HEAD_DIM, ncp), lambda b, g, c: (N_KV + g, b, 0, 0)),
            k_spec, vt_spec, k_spec, vt_spec,
            pl.BlockSpec((1, Q_BLOCK, LANES), lambda b, g, c: (g, b * nq + c, 0)),
        ],
        out_specs=pl.BlockSpec((Q_BLOCK, GQA_R * HEAD_DIM), lambda b, g, c: (b * nq + c, g)),
        out_shape=jax.ShapeDtypeStruct((bsz * t, ATTN_WIDTH), jnp.float32),
        scratch_shapes=[pltpu.VMEM((t // SEL_BLOCK, Q_BLOCK), jnp.float32),
                        pltpu.VMEM((SEL_KEY_TILE, rows), jnp.float32),
                        pltpu.VMEM((SEL_KEY_TILE, rows), jnp.float32),
                        pltpu.VMEM((SEL_KEY_TILE, rows), jnp.bfloat16),
                        pltpu.VMEM((SEL_KEY_TILE, rows), jnp.bfloat16)],
        compiler_params=pltpu.CompilerParams(
            dimension_semantics=("parallel", "parallel", "parallel")),
    )(q, kcmp, kcmp_t, ks, vst, kw, vwt, gates)


def _s5_kernel(u_ref, bm_ref, cm_ref, ar_ref, ai_ref, d_ref, wglu_ref, bglu_ref, o_ref,
               bu_ref, state_ref, *, bsz):
    steps = u_ref.shape[0] // bsz

    @pl.when(pl.program_id(0) == 0)
    def _():
        state_ref[...] = jnp.zeros_like(state_ref)

    u = u_ref[...]
    ub = u.astype(jnp.bfloat16)
    ys = []
    for half in range(2):
        bu_ref[...] = _dot(ub[:, half * S5_HALF:(half + 1) * S5_HALF], bm_ref[half])
        for lc in range(0, S5_HALF_STATE, S5_LANE_CHUNK):
            re = slice(lc, lc + S5_LANE_CHUNK)
            im = slice(S5_HALF_STATE + lc, S5_HALF_STATE + lc + S5_LANE_CHUNK)
            a_re = jnp.broadcast_to(ar_ref[half, :, re], (bsz, S5_LANE_CHUNK))
            a_im = jnp.broadcast_to(ai_ref[half, :, re], (bsz, S5_LANE_CHUNK))

            def step(i, carry):
                xr, xi = carry
                rows = pl.ds(pl.multiple_of(i * bsz, bsz), bsz)
                nr = a_re * xr - a_im * xi + bu_ref[rows, re]
                ni = a_re * xi + a_im * xr + bu_ref[rows, im]
                bu_ref[rows, re] = nr
                bu_ref[rows, im] = ni
                return nr, ni

            xr, xi = lax.fori_loop(0, steps, step, (state_ref[half, :, re], state_ref[half, :, im]),
                                   unroll=4)
            state_ref[half, :, re] = xr
            state_ref[half, :, im] = xi
        ys.append(_dot(bu_ref[...].astype(jnp.bfloat16), cm_ref[half]))
    y = jnp.concatenate(ys, axis=-1) + d_ref[...] * u
    y = jax.nn.gelu(y)
    o_ref[...] = y * jax.nn.sigmoid(_dot(y.astype(jnp.bfloat16), wglu_ref[...]) + bglu_ref[...])


def _s5(u_tb, bm, cm, ar, ai, d, wglu, bglu, bsz):
    n = u_tb.shape[0]
    rows = S5_TIME_CHUNK * bsz
    const3 = lambda i: (0, 0, 0)
    const2 = lambda i: (0, 0)
    return pl.pallas_call(
        functools.partial(_s5_kernel, bsz=bsz),
        grid=(n // rows,),
        in_specs=[
            pl.BlockSpec((rows, SSM_WIDTH), lambda i: (i, 0)),
            pl.BlockSpec((2, S5_HALF, 2 * S5_HALF_STATE), const3),
            pl.BlockSpec((2, 2 * S5_HALF_STATE, S5_HALF), const3),
            pl.BlockSpec((2, 1, S5_HALF_STATE), const3),
            pl.BlockSpec((2, 1, S5_HALF_STATE), const3),
            pl.BlockSpec((1, SSM_WIDTH), const2),
            pl.BlockSpec((SSM_WIDTH, SSM_WIDTH), const2),
            pl.BlockSpec((1, SSM_WIDTH), const2),
        ],
        out_specs=pl.BlockSpec((rows, SSM_WIDTH), lambda i: (i, 0)),
        out_shape=jax.ShapeDtypeStruct((n, SSM_WIDTH), jnp.float32),
        scratch_shapes=[
            pltpu.VMEM((rows, 2 * S5_HALF_STATE), jnp.float32),
            pltpu.VMEM((2, bsz, 2 * S5_HALF_STATE), jnp.float32),
        ],
        compiler_params=pltpu.CompilerParams(
            dimension_semantics=("arbitrary",), vmem_limit_bytes=VMEM_LIMIT),
    )(u_tb, bm, cm, ar, ai, d, wglu, bglu)


def _s5_params(lam_re, lam_im, log_dt, b_re, b_im, c_re, c_im):
    dt = jnp.exp(log_dt)[:, None]
    mag = jnp.exp(lam_re * dt)
    ab_re = mag * jnp.cos(lam_im * dt)
    ab_im = mag * jnp.sin(lam_im * dt)
    nr = ab_re - 1.0
    den = lam_re * lam_re + lam_im * lam_im
    f_re = (nr * lam_re + ab_im * lam_im) / den
    f_im = (ab_im * lam_re - nr * lam_im) / den
    bb_re = f_re[..., None] * b_re - f_im[..., None] * b_im
    bb_im = f_re[..., None] * b_im + f_im[..., None] * b_re
    gh = SSM_GROUPS // 2
    eye = jnp.eye(gh, dtype=jnp.float32)

    def in_half(bb):
        return jnp.einsum('gph,gk->ghkp', bb, eye).reshape(S5_HALF, S5_HALF_STATE)

    def out_half(cc):
        return jnp.einsum('ghp,gk->gpkh', cc, eye).reshape(S5_HALF_STATE, S5_HALF)

    bm = jnp.stack([jnp.concatenate([in_half(bb_re[s]), in_half(bb_im[s])], axis=1)
                    for s in (slice(0, gh), slice(gh, None))])
    cm = jnp.stack([jnp.concatenate([out_half(c_re[s]), -out_half(c_im[s])], axis=0)
                    for s in (slice(0, gh), slice(gh, None))])
    ar = ab_re.reshape(2, 1, S5_HALF_STATE)
    ai = ab_im.reshape(2, 1, S5_HALF_STATE)
    return bm.astype(jnp.bfloat16), cm.astype(jnp.bfloat16), ar, ai


def _out_ffn2_kernel(x1_ref, a_ref, s_ref, ga_ref, gs_ref, wo_ref, g2_ref, wg_ref, wu_ref, wd_ref,
                     gf_ref, o_ref):
    a = _rmsnorm(a_ref[...], ga_ref[...]).astype(jnp.bfloat16)
    s = _rmsnorm(s_ref[...], gs_ref[...]).astype(jnp.bfloat16)
    x2 = x1_ref[...] + _dot(a, wo_ref[:ATTN_WIDTH, :]) + _dot(s, wo_ref[ATTN_WIDTH:, :])
    h = _rmsnorm(x2, g2_ref[...]).astype(jnp.bfloat16)
    x3 = x2 + 0.5 * _swiglu_acc(h, wg_ref, wu_ref, wd_ref)
    o_ref[...] = _rmsnorm(x3, gf_ref[...])


def _out_ffn2(x1, attn, ssm_t, ga, gs, wo, g2, wg, wu, wd, gf, bsz, t):
    n = bsz * t
    tm = min(TOKEN_TILE, t)
    nt = t // tm
    row = lambda i: (i, 0)
    return pl.pallas_call(
        _out_ffn2_kernel,
        grid=(n // tm,),
        in_specs=[
            pl.BlockSpec((tm, D_MODEL), row),
            pl.BlockSpec((tm, ATTN_WIDTH), row),
            pl.BlockSpec((tm, SSM_WIDTH), lambda i: (i % nt, i // nt)),
            _resident((1, ATTN_WIDTH)), _resident((1, SSM_WIDTH)),
            _resident((D_MODEL, D_MODEL)),
            _resident((1, D_MODEL)),
            _resident((D_MODEL, D_FF)), _resident((D_MODEL, D_FF)), _resident((D_FF, D_MODEL)),
            _resident((1, D_MODEL)),
        ],
        out_specs=pl.BlockSpec((tm, D_MODEL), row),
        out_shape=jax.ShapeDtypeStruct((n, D_MODEL), jnp.float32),
        compiler_params=pltpu.CompilerParams(
            dimension_semantics=("parallel",), vmem_limit_bytes=VMEM_LIMIT),
    )(x1, attn, ssm_t, ga, gs, wo, g2, wg, wu, wd, gf)


def _reorder_w_in(w_in, gate_bias):
    per_g = N_GATE_COLS // N_KV
    g_lo = GATE_OFF
    pieces = [w_in[:, :g_lo]]
    bias = []
    gb = gate_bias.reshape(1, N_GATE_COLS)
    for g in range(N_KV):
        cols = w_in[:, g_lo + g * per_g:g_lo + (g + 1) * per_g]
        pieces.append(jnp.pad(cols, ((0, 0), (0, LANES - per_g))))
        bias.append(jnp.pad(gb[:, g * per_g:(g + 1) * per_g], ((0, 0), (0, LANES - per_g))))
    pieces.append(w_in[:, g_lo + N_GATE_COLS:])
    return jnp.concatenate(pieces, axis=1).astype(jnp.bfloat16), jnp.concatenate(bias, axis=1)


def kernel(x, ffn1_norm, ffn1_w_gate, ffn1_w_up, ffn1_w_down, mix_norm, w_in, gate_bias, cmp_pos_k, cmp_pos_v, cmp_k_w1, cmp_k_b1, cmp_k_w2, cmp_v_w1, cmp_v_b1, cmp_v_w2, s5_lambda_re, s5_lambda_im, s5_log_dt, s5_b_re, s5_b_im, s5_c_re, s5_c_im, s5_d, s5_w_glu, s5_b_glu, attn_out_norm, ssm_out_norm, w_out, ffn2_norm, ffn2_w_gate, ffn2_w_up, ffn2_w_down, final_norm):
    bsz, t, _ = x.shape
    n = bsz * t
    bf = jnp.bfloat16
    assert ffn1_norm.shape[0] == 1, "the final rmsnorm is fused into the single layer's last kernel"
    assert t % max(TOKEN_TILE, WINDOW) == 0 and t >= WINDOW + Q_BLOCK and x.shape[2] == D_MODEL
    xc = x.reshape(n, D_MODEL)
    for l in range(1):
        win, gb = _reorder_w_in(w_in[l], gate_bias[l])
        x1, q, kvc, ks, vst, kw, vwt, gates, u_t = _ffn1_proj(
            xc, ffn1_norm[l][None], ffn1_w_gate[l].astype(bf), ffn1_w_up[l].astype(bf),
            ffn1_w_down[l].astype(bf), mix_norm[l][None], win, gb, bsz, t)

        kvc4 = kvc.reshape(2 * N_KV, bsz, t // CMP_STRIDE, CMP_STRIDE * HEAD_DIM)
        pos = jnp.stack([cmp_pos_k[l], cmp_pos_v[l]]).reshape(2, 1, CMP_LEN * HEAD_DIM)
        w1 = jnp.stack([cmp_k_w1[l], cmp_v_w1[l]]).astype(bf)
        b1 = jnp.stack([cmp_k_b1[l], cmp_v_b1[l]])[:, None, :]
        w2 = jnp.stack([cmp_k_w2[l], cmp_v_w2[l]]).astype(bf)
        kcmp, kcmp_t = _compress(kvc4, pos, w1, b1, w2)

        attn = _nsa(q, kcmp, kcmp_t, ks, vst, kw, vwt, gates, bsz, t)

        bm, cm, ar, ai = _s5_params(s5_lambda_re[l], s5_lambda_im[l], s5_log_dt[l], s5_b_re[l],
                                    s5_b_im[l], s5_c_re[l], s5_c_im[l])
        ssm_t = _s5(u_t.reshape(t * bsz, SSM_WIDTH), bm, cm, ar, ai,
                    s5_d[l].reshape(1, SSM_WIDTH), s5_w_glu[l].astype(bf), s5_b_glu[l][None], bsz)

        xc = _out_ffn2(x1, attn, ssm_t.reshape(t, bsz * SSM_WIDTH), attn_out_norm[l][None],
                       ssm_out_norm[l][None], w_out[l].astype(bf), ffn2_norm[l][None],
                       ffn2_w_gate[l].astype(bf), ffn2_w_up[l].astype(bf), ffn2_w_down[l].astype(bf),
                       final_norm[None], bsz, t)
    return xc.reshape(bsz, t, D_MODEL)
```

```python
import functools
import math

import jax
import jax.numpy as jnp
from jax import lax
from jax.experimental import pallas as pl
from jax.experimental.pallas import tpu as pltpu

D_MODEL = 1024
HEAD_DIM = 64
N_HEADS = 8
N_KV = 2
GQA_R = N_HEADS // N_KV
ATTN_WIDTH = N_HEADS * HEAD_DIM
CMP_LEN = 32
CMP_STRIDE = 16
CMP_HIDDEN = 256
SEL_BLOCK = 64
SEL_SHIFT = 6
SEL_TOPK = 16
WINDOW = 512
Q_BLOCK = 128
SSM_WIDTH = 512
SSM_GROUP = 16
SSM_GROUPS = SSM_WIDTH // SSM_GROUP
SSM_STATE = 64
D_FF = 2816
EPS = 1e-6

LANES = 128
SUBLANES = 8
FF_CHUNK = 256
TOKEN_TILE = 512
SEL_KEY_TILE = 256
S5_TIME_CHUNK = 64
S5_LANE_CHUNK = 512
S5_HALF = SSM_WIDTH // 2
S5_HALF_STATE = (SSM_GROUPS // 2) * SSM_STATE
VMEM_LIMIT = 56 * 1024 * 1024
NEG_BIG = -1e30
Q_SCALE = HEAD_DIM ** -0.5 * math.log2(math.e)

Q_OFF = 0
KVC_OFF = 512
KVS_OFF = 768
KVW_OFF = 1024
GATE_OFF = 1280
U_OFF = 1536
IN_COLS_PADDED = 2048
N_GATE_COLS = N_HEADS * 3


def _dot(a, b):
    return jnp.dot(a, b, preferred_element_type=jnp.float32)


def _dot_nt(a, b):
    return lax.dot_general(a, b, (((1,), (1,)), ((), ())), preferred_element_type=jnp.float32)


def _rmsnorm(x, g):
    r = lax.rsqrt(jnp.mean(x * x, axis=-1, keepdims=True) + EPS)
    return (x * r) * g


def _swiglu_acc(h, wg_ref, wu_ref, wd_ref):
    acc = jnp.zeros((h.shape[0], D_MODEL), jnp.float32)
    for f in range(0, D_FF, FF_CHUNK):
        gate = _dot(h, wg_ref[:, f:f + FF_CHUNK])
        up = _dot(h, wu_ref[:, f:f + FF_CHUNK])
        act = (gate * jax.nn.sigmoid(gate) * up).astype(jnp.bfloat16)
        acc = acc + _dot(act, wd_ref[f:f + FF_CHUNK, :])
    return acc


def _resident(shape):
    nd = len(shape)
    return pl.BlockSpec(shape, lambda *_: (0,) * nd, pipeline_mode=pl.Buffered(1))


def _ffn1_proj_kernel(x_ref, g1_ref, wg_ref, wu_ref, wd_ref, gm_ref, win_ref, gb_ref,
                      x1_ref, q_ref, kvc_ref, ks_ref, vst_ref, kw_ref, vwt_ref, gate_ref, u_ref):
    x = x_ref[...]
    h = _rmsnorm(x, g1_ref[...]).astype(jnp.bfloat16)
    x1 = x + 0.5 * _swiglu_acc(h, wg_ref, wu_ref, wd_ref)
    x1_ref[...] = x1
    h2 = _rmsnorm(x1, gm_ref[...]).astype(jnp.bfloat16)
    p = _dot(h2, win_ref[...])
    for hh in range(N_HEADS):
        lo = Q_OFF + hh * HEAD_DIM
        q_ref[hh] = (p[:, lo:lo + HEAD_DIM] * Q_SCALE).astype(q_ref.dtype)
    for i in range(2 * N_KV):
        kvc_ref[i] = p[:, KVC_OFF + i * HEAD_DIM:KVC_OFF + (i + 1) * HEAD_DIM]
    for g in range(N_KV):
        for off, k_ref, vt_ref in ((KVS_OFF, ks_ref, vst_ref), (KVW_OFF, kw_ref, vwt_ref)):
            k_lo = off + g * HEAD_DIM
            v_lo = off + (N_KV + g) * HEAD_DIM
            k_ref[g] = p[:, k_lo:k_lo + HEAD_DIM].astype(k_ref.dtype)
            vt_ref[g, 0] = p[:, v_lo:v_lo + HEAD_DIM].T.astype(vt_ref.dtype)
    for g in range(N_KV):
        lo = GATE_OFF + g * LANES
        gate = jax.nn.sigmoid(p[:, lo:lo + LANES] + gb_ref[:, g * LANES:(g + 1) * LANES])
        gate_ref[g, 0] = gate.T
    u_ref[...] = p[:, U_OFF:U_OFF + SSM_WIDTH]


def _ffn1_proj(x2d, g1, wg, wu, wd, gm, win, gb, bsz, t):
    n = bsz * t
    tm = min(TOKEN_TILE, t)
    nt = t // tm
    row = lambda i: (i, 0)
    lead = lambda i: (0, i, 0)
    k_spec = pl.BlockSpec((N_KV, tm, HEAD_DIM), lead)
    vt_spec = pl.BlockSpec((N_KV, 1, HEAD_DIM, tm), lambda i: (0, i // nt, 0, i % nt))
    k_shape = jax.ShapeDtypeStruct((N_KV, n, HEAD_DIM), jnp.bfloat16)
    vt_shape = jax.ShapeDtypeStruct((N_KV, bsz, HEAD_DIM, t), jnp.bfloat16)
    return pl.pallas_call(
        _ffn1_proj_kernel,
        grid=(n // tm,),
        in_specs=[
            pl.BlockSpec((tm, D_MODEL), row),
            _resident((1, D_MODEL)),
            _resident((D_MODEL, D_FF)), _resident((D_MODEL, D_FF)), _resident((D_FF, D_MODEL)),
            _resident((1, D_MODEL)),
            _resident((D_MODEL, IN_COLS_PADDED)),
            _resident((1, N_KV * LANES)),
        ],
        out_specs=[
            pl.BlockSpec((tm, D_MODEL), row),
            pl.BlockSpec((N_HEADS, tm, HEAD_DIM), lead),
            pl.BlockSpec((2 * N_KV, tm, HEAD_DIM), lead),
            k_spec, vt_spec, k_spec, vt_spec,
            pl.BlockSpec((N_KV, 1, LANES, tm), lambda i: (0, i // nt, 0, i % nt)),
            pl.BlockSpec((tm, SSM_WIDTH), row),
        ],
        out_shape=[
            jax.ShapeDtypeStruct((n, D_MODEL), jnp.float32),
            jax.ShapeDtypeStruct((N_HEADS, n, HEAD_DIM), jnp.bfloat16),
            jax.ShapeDtypeStruct((2 * N_KV, n, HEAD_DIM), jnp.float32),
            k_shape, vt_shape, k_shape, vt_shape,
            jax.ShapeDtypeStruct((N_KV, bsz, LANES, t), jnp.float32),
            jax.ShapeDtypeStruct((n, SSM_WIDTH), jnp.float32),
        ],
        compiler_params=pltpu.CompilerParams(
            dimension_semantics=("parallel",), vmem_limit_bytes=VMEM_LIMIT),
    )(x2d, g1, wg, wu, wd, gm, win, gb)


def _compress_kernel(h_ref, pos_ref, w1_ref, b1_ref, w2_ref, o_ref, ot_ref):
    half = CMP_STRIDE * HEAD_DIM
    hrows = h_ref[0, 0]
    top = (hrows + pos_ref[0, :, :half]).astype(jnp.bfloat16)
    bot = (hrows + pos_ref[0, :, half:]).astype(jnp.bfloat16)
    a = _dot(top, w1_ref[0, :half, :])
    b = _dot(bot, w1_ref[0, half:, :])
    nrow = hrows.shape[0]
    pre = a + pltpu.roll(b, nrow - 1, 0) + b1_ref[0]
    out = _dot(jax.nn.gelu(pre).astype(jnp.bfloat16), w2_ref[0])
    rid = lax.broadcasted_iota(jnp.int32, out.shape, 0)
    out = jnp.where(rid < nrow - 1, out, 0.0)
    o_ref[0, 0] = out
    ot_ref[0, 0] = out.T


def _compress(kvc4, pos, w1, b1, w2):
    _, bsz, nrow, half = kvc4.shape
    kind = lambda i, b: (i // N_KV, 0, 0)
    return pl.pallas_call(
        _compress_kernel,
        grid=(2 * N_KV, bsz),
        in_specs=[
            pl.BlockSpec((1, 1, nrow, half), lambda i, b: (i, b, 0, 0)),
            pl.BlockSpec((1, 1, CMP_LEN * HEAD_DIM), kind),
            pl.BlockSpec((1, CMP_LEN * HEAD_DIM, CMP_HIDDEN), kind),
            pl.BlockSpec((1, 1, CMP_HIDDEN), kind),
            pl.BlockSpec((1, CMP_HIDDEN, HEAD_DIM), kind),
        ],
        out_specs=[pl.BlockSpec((1, 1, nrow, HEAD_DIM), lambda i, b: (i, b, 0, 0)),
                   pl.BlockSpec((1, 1, HEAD_DIM, nrow), lambda i, b: (i, b, 0, 0))],
        out_shape=[jax.ShapeDtypeStruct((2 * N_KV, bsz, nrow, HEAD_DIM), jnp.float32),
                   jax.ShapeDtypeStruct((2 * N_KV, bsz, HEAD_DIM, nrow), jnp.float32)],
        compiler_params=pltpu.CompilerParams(dimension_semantics=("parallel", "parallel")),
    )(kvc4, pos, w1, b1, w2)


def _split3_bf16(x):
    hi = x.astype(jnp.bfloat16)
    r1 = x - hi.astype(jnp.float32)
    mid = r1.astype(jnp.bfloat16)
    lo = (r1 - mid.astype(jnp.float32)).astype(jnp.bfloat16)
    return hi, mid, lo


def _sweep_t(q, k_ref, vt_ref, n_tiles, tile, bias_fn, s_refs, p_refs):
    rows = q.shape[0]
    last_tile = k_ref.shape[1] // tile - 1

    def k_tile(i):
        return k_ref[0, pl.ds(pl.multiple_of(i * tile, tile), tile), :]

    def vt_tile(i):
        return vt_ref[0, 0, :, pl.ds(pl.multiple_of(i * tile, tile), tile)]

    def scores_into(i, s_ref):
        s = _dot_nt(k_tile(i), q)
        bias = bias_fn(i)
        s = jnp.concatenate([s[:, r * Q_BLOCK:(r + 1) * Q_BLOCK] + bias
                             for r in range(rows // Q_BLOCK)], axis=1)
        s_ref[...] = s
        return jnp.max(s, axis=0, keepdims=True)

    def phase(i, carry, s_cur, s_nxt, p_cur, p_prv):
        m, l, acc, s_max = carry
        s_max_next = scores_into(jnp.minimum(i + 1, last_tile), s_nxt)
        acc = acc + _dot(vt_tile(jnp.maximum(i - 1, 0)), p_prv[...])
        m_new = jnp.maximum(m, s_max)
        alpha = jnp.exp2(m - m_new)
        p = jnp.exp2(s_cur[...] - m_new)
        p_cur[...] = p.astype(p_cur.dtype)
        l = alpha * l + jnp.sum(p, axis=0, keepdims=True)
        return m_new, l, alpha * acc, s_max_next

    def body(j, carry):
        carry = phase(2 * j, carry, s_refs[0], s_refs[1], p_refs[0], p_refs[1])
        return phase(2 * j + 1, carry, s_refs[1], s_refs[0], p_refs[1], p_refs[0])

    p_refs[1][...] = jnp.zeros_like(p_refs[1])
    init = (jnp.full((1, rows), NEG_BIG, jnp.float32), jnp.zeros((1, rows), jnp.float32),
            jnp.zeros((HEAD_DIM, rows), jnp.float32), scores_into(0, s_refs[0]))
    n_pairs = (n_tiles + 1) // 2
    _, l, acc, _ = lax.fori_loop(0, n_pairs, body, init)
    return (acc + _dot(vt_tile(2 * n_pairs - 1), p_refs[1][...])) / l


def _nsa_kernel(q_ref, kc_ref, vct_ref, ks_ref, vst_ref, kw_ref, vwt_ref, gate_ref, o_ref,
                bias_ref, s0_ref, s1_ref, p0_ref, p1_ref):
    c = pl.program_id(2)
    rows = GQA_R * Q_BLOCK
    q = q_ref[...].reshape(rows, HEAD_DIM)
    ncp = kc_ref.shape[2]
    t = ks_ref.shape[1]
    nb = t // SEL_BLOCK

    kc = kc_ref[0, 0].astype(jnp.bfloat16)
    vct = vct_ref[0, 0].astype(jnp.bfloat16)
    s_c = _dot_nt(kc, q)
    n_c = lax.broadcasted_iota(jnp.int32, (ncp, Q_BLOCK), 0)
    tq_c = c * Q_BLOCK + lax.broadcasted_iota(jnp.int32, (ncp, Q_BLOCK), 1)
    bias_c = jnp.where(n_c * CMP_STRIDE + (CMP_LEN - 1) <= tq_c, 0.0, NEG_BIG)
    s_c = jnp.concatenate([s_c[:, r * Q_BLOCK:(r + 1) * Q_BLOCK] + bias_c for r in range(GQA_R)], axis=1)
    e_c = jnp.exp2(s_c - jnp.max(s_c, axis=0, keepdims=True))
    tq_row = c * Q_BLOCK + (lax.broadcasted_iota(jnp.int32, (1, rows), 1) & (Q_BLOCK - 1))
    inv_c = jnp.where(tq_row >= CMP_LEN - 1, 1.0 / jnp.sum(e_c, axis=0, keepdims=True), 0.0)
    p_c = e_c * inv_c
    o_cmp = _dot(vct, p_c.astype(jnp.bfloat16))

    band = WINDOW + Q_BLOCK
    w0 = pl.multiple_of(jnp.maximum(c * Q_BLOCK - WINDOW, 0), Q_BLOCK)
    kw = kw_ref[0, pl.ds(w0, band), :]
    vwt = vwt_ref[0, 0, :, pl.ds(w0, band)]
    kpos_w = w0 + lax.broadcasted_iota(jnp.int32, (band, Q_BLOCK), 0)
    tq_w = c * Q_BLOCK + lax.broadcasted_iota(jnp.int32, (band, Q_BLOCK), 1)
    bias_w = jnp.where((kpos_w <= tq_w) & (kpos_w > tq_w - WINDOW), 0.0, NEG_BIG)
    s_w = _dot_nt(kw, q)
    s_w = jnp.concatenate([s_w[:, r * Q_BLOCK:(r + 1) * Q_BLOCK] + bias_w for r in range(GQA_R)], axis=1)
    p_w = jnp.exp2(s_w - jnp.max(s_w, axis=0, keepdims=True))
    o_win = _dot(vwt, p_w.astype(jnp.bfloat16)) / jnp.sum(p_w, axis=0, keepdims=True)

    p_sum = p_c[:, 0:Q_BLOCK]
    for r in range(1, GQA_R):
        p_sum = p_sum + p_c[:, r * Q_BLOCK:(r + 1) * Q_BLOCK]
    jb = lax.broadcasted_iota(jnp.int32, (nb, ncp), 0)
    nn = lax.broadcasted_iota(jnp.int32, (nb, ncp), 1)
    overlap_t = ((nn * CMP_STRIDE < jb * SEL_BLOCK + SEL_BLOCK)
                 & (nn * CMP_STRIDE + CMP_LEN > jb * SEL_BLOCK)
                 & (nn < ncp - 1))
    overlap_t = jnp.where(overlap_t, 1.0, 0.0).astype(jnp.bfloat16)
    hi, mid, lo = _split3_bf16(p_sum)
    imp = _dot(overlap_t, hi) + _dot(overlap_t, mid) + _dot(overlap_t, lo)

    j_t = lax.broadcasted_iota(jnp.int32, (nb, Q_BLOCK), 0)
    tq_t = c * Q_BLOCK + lax.broadcasted_iota(jnp.int32, (nb, Q_BLOCK), 1)
    cur = tq_t >> SEL_SHIFT
    forced = (j_t == 0) | (j_t == cur) | (j_t == cur - 1)
    valid = j_t * SEL_BLOCK <= tq_t
    score = jnp.where(forced, jnp.inf, jnp.where(valid, imp, -jnp.inf))
    groups = [slice(v, v + SUBLANES) for v in range(0, nb, SUBLANES)]
    ranks = [jnp.zeros((SUBLANES, Q_BLOCK), jnp.float32) for _ in groups]
    for jp in range(nb):
        row = jnp.broadcast_to(score[jp:jp + 1, :], (SUBLANES, Q_BLOCK))
        for v, grp in enumerate(groups):
            if jp < grp.start:
                beats = row >= score[grp]
            elif jp >= grp.stop:
                beats = row > score[grp]
            else:
                beats = (row > score[grp]) | ((row == score[grp]) & (j_t[grp] > jp))
            ranks[v] = ranks[v] + jnp.where(beats, 1.0, 0.0)
    rank = jnp.concatenate(ranks, axis=0)
    bias_ref[...] = jnp.where(rank < min(SEL_TOPK, nb), 0.0, NEG_BIG)

    kpos_s = lax.broadcasted_iota(jnp.int32, (SEL_KEY_TILE, Q_BLOCK), 0)
    tq_s = c * Q_BLOCK + lax.broadcasted_iota(jnp.int32, (SEL_KEY_TILE, Q_BLOCK), 1)
    blocks_per_tile = SEL_KEY_TILE // SEL_BLOCK

    def sel_bias(kt):
        blocks = [jnp.broadcast_to(bias_ref[pl.ds(kt * blocks_per_tile + jj, 1), :], (SEL_BLOCK, Q_BLOCK))
                  for jj in range(blocks_per_tile)]
        return jnp.where(kt * SEL_KEY_TILE + kpos_s <= tq_s, jnp.concatenate(blocks, axis=0), NEG_BIG)

    n_sel_tiles = ((c + 1) * Q_BLOCK + SEL_KEY_TILE - 1) // SEL_KEY_TILE
    o_slc = _sweep_t(q, ks_ref, vst_ref, n_sel_tiles, SEL_KEY_TILE, sel_bias,
                     (s0_ref, s1_ref), (p0_ref, p1_ref))

    gt = gate_ref[0, 0]
    for r in range(GQA_R):
        sl = slice(r * Q_BLOCK, (r + 1) * Q_BLOCK)
        o_ref[0, r * HEAD_DIM:(r + 1) * HEAD_DIM, :] = (
            gt[3 * r:3 * r + 1] * o_cmp[:, sl] + gt[3 * r + 1:3 * r + 2] * o_slc[:, sl]
            + gt[3 * r + 2:3 * r + 3] * o_win[:, sl])


def _nsa(q, kcmp, kcmp_t, ks, vst, kw, vwt, gates, bsz, t):
    nq = t // Q_BLOCK
    ncp = kcmp.shape[2]
    rows = GQA_R * Q_BLOCK
    assert t % (2 * SEL_KEY_TILE) == 0, "the selected-block sweep runs pairs of key tiles"
    k_spec = pl.BlockSpec((1, t, HEAD_DIM), lambda b, g, c: (g, b, 0))
    vt_spec = pl.BlockSpec((1, 1, HEAD_DIM, t), lambda b, g, c: (g, b, 0, 0))
    return pl.pallas_call(
        _nsa_kernel,
        grid=(bsz, N_KV, nq),
        in_specs=[
            pl.BlockSpec((GQA_R, Q_BLOCK, HEAD_DIM), lambda b, g, c: (g, b * nq + c, 0)),
            pl.BlockSpec((1, 1, ncp, HEAD_DIM), lambda b, g, c: (g, b, 0, 0)),
            pl.BlockSpec((1, 1, HEAD_DIM, ncp), lambda b, g, c: (N_KV + g, b, 0, 0)),
            k_spec, vt_spec, k_spec, vt_spec,
            pl.BlockSpec((1, 1, LANES, Q_BLOCK), lambda b, g, c: (g, b, 0, c)),
        ],
        out_specs=pl.BlockSpec((1, GQA_R * HEAD_DIM, Q_BLOCK), lambda b, g, c: (b, g, c)),
        out_shape=jax.ShapeDtypeStruct((bsz, ATTN_WIDTH, t), jnp.float32),
        scratch_shapes=[pltpu.VMEM((t // SEL_BLOCK, Q_BLOCK), jnp.float32),
                        pltpu.VMEM((SEL_KEY_TILE, rows), jnp.float32),
                        pltpu.VMEM((SEL_KEY_TILE, rows), jnp.float32),
                        pltpu.VMEM((SEL_KEY_TILE, rows), jnp.bfloat16),
                        pltpu.VMEM((SEL_KEY_TILE, rows), jnp.bfloat16)],
        compiler_params=pltpu.CompilerParams(
            dimension_semantics=("parallel", "parallel", "parallel")),
    )(q, kcmp, kcmp_t, ks, vst, kw, vwt, gates)


def _s5_kernel(u_ref, bm_ref, cm_ref, ar_ref, ai_ref, d_ref, wglu_ref, bglu_ref, o_ref,
               utb_ref, bu_ref, state_ref):
    bsz, steps, _ = u_ref.shape

    @pl.when(pl.program_id(0) == 0)
    def _():
        state_ref[...] = jnp.zeros_like(state_ref)

    n_lane_tiles = utb_ref.shape[0]
    for b in range(bsz):
        for j in range(n_lane_tiles):
            utb_ref[j, pl.ds(b, steps, stride=bsz), :] = u_ref[b, :, j * LANES:(j + 1) * LANES]
    u = jnp.concatenate([utb_ref[j] for j in range(n_lane_tiles)], axis=1)
    ub = u.astype(jnp.bfloat16)
    ys = []
    for half in range(2):
        bu_ref[...] = _dot(ub[:, half * S5_HALF:(half + 1) * S5_HALF], bm_ref[half])
        for lc in range(0, S5_HALF_STATE, S5_LANE_CHUNK):
            re = slice(lc, lc + S5_LANE_CHUNK)
            im = slice(S5_HALF_STATE + lc, S5_HALF_STATE + lc + S5_LANE_CHUNK)
            a_re = jnp.broadcast_to(ar_ref[half, :, re], (bsz, S5_LANE_CHUNK))
            a_im = jnp.broadcast_to(ai_ref[half, :, re], (bsz, S5_LANE_CHUNK))

            def step(i, carry):
                xr, xi = carry
                rows = pl.ds(pl.multiple_of(i * bsz, bsz), bsz)
                nr = a_re * xr - a_im * xi + bu_ref[rows, re]
                ni = a_re * xi + a_im * xr + bu_ref[rows, im]
                bu_ref[rows, re] = nr
                bu_ref[rows, im] = ni
                return nr, ni

            xr, xi = lax.fori_loop(0, steps, step, (state_ref[half, :, re], state_ref[half, :, im]),
                                   unroll=4)
            state_ref[half, :, re] = xr
            state_ref[half, :, im] = xi
        ys.append(_dot(bu_ref[...].astype(jnp.bfloat16), cm_ref[half]))
    y = jnp.concatenate(ys, axis=-1) + d_ref[...] * u
    y = jax.nn.gelu(y)
    out = y * jax.nn.sigmoid(_dot(y.astype(jnp.bfloat16), wglu_ref[...]) + bglu_ref[...])
    for j in range(n_lane_tiles):
        utb_ref[j] = out[:, j * LANES:(j + 1) * LANES]
    for b in range(bsz):
        for j in range(n_lane_tiles):
            o_ref[b, :, j * LANES:(j + 1) * LANES] = utb_ref[j, pl.ds(b, steps, stride=bsz), :]


def _s5(u, bm, cm, ar, ai, d, wglu, bglu):
    bsz, t, _ = u.shape
    rows = S5_TIME_CHUNK * bsz
    const3 = lambda i: (0, 0, 0)
    const2 = lambda i: (0, 0)
    io_spec = pl.BlockSpec((bsz, S5_TIME_CHUNK, SSM_WIDTH), lambda i: (0, i, 0))
    return pl.pallas_call(
        _s5_kernel,
        grid=(t // S5_TIME_CHUNK,),
        in_specs=[
            io_spec,
            pl.BlockSpec((2, S5_HALF, 2 * S5_HALF_STATE), const3),
            pl.BlockSpec((2, 2 * S5_HALF_STATE, S5_HALF), const3),
            pl.BlockSpec((2, 1, S5_HALF_STATE), const3),
            pl.BlockSpec((2, 1, S5_HALF_STATE), const3),
            pl.BlockSpec((1, SSM_WIDTH), const2),
            pl.BlockSpec((SSM_WIDTH, SSM_WIDTH), const2),
            pl.BlockSpec((1, SSM_WIDTH), const2),
        ],
        out_specs=io_spec,
        out_shape=jax.ShapeDtypeStruct((bsz, t, SSM_WIDTH), jnp.float32),
        scratch_shapes=[
            pltpu.VMEM((SSM_WIDTH // LANES, rows, LANES), jnp.float32),
            pltpu.VMEM((rows, 2 * S5_HALF_STATE), jnp.float32),
            pltpu.VMEM((2, bsz, 2 * S5_HALF_STATE), jnp.float32),
        ],
        compiler_params=pltpu.CompilerParams(
            dimension_semantics=("arbitrary",), vmem_limit_bytes=VMEM_LIMIT),
    )(u, bm, cm, ar, ai, d, wglu, bglu)


def _s5_params(lam_re, lam_im, log_dt, b_re, b_im, c_re, c_im):
    dt = jnp.exp(log_dt)[:, None]
    mag = jnp.exp(lam_re * dt)
    ab_re = mag * jnp.cos(lam_im * dt)
    ab_im = mag * jnp.sin(lam_im * dt)
    nr = ab_re - 1.0
    den = lam_re * lam_re + lam_im * lam_im
    f_re = (nr * lam_re + ab_im * lam_im) / den
    f_im = (ab_im * lam_re - nr * lam_im) / den
    bb_re = f_re[..., None] * b_re - f_im[..., None] * b_im
    bb_im = f_re[..., None] * b_im + f_im[..., None] * b_re
    gh = SSM_GROUPS // 2
    eye = jnp.eye(gh, dtype=jnp.float32)

    def in_half(bb):
        return jnp.einsum('gph,gk->ghkp', bb, eye).reshape(S5_HALF, S5_HALF_STATE)

    def out_half(cc):
        return jnp.einsum('ghp,gk->gpkh', cc, eye).reshape(S5_HALF_STATE, S5_HALF)

    bm = jnp.stack([jnp.concatenate([in_half(bb_re[s]), in_half(bb_im[s])], axis=1)
                    for s in (slice(0, gh), slice(gh, None))])
    cm = jnp.stack([jnp.concatenate([out_half(c_re[s]), -out_half(c_im[s])], axis=0)
                    for s in (slice(0, gh), slice(gh, None))])
    ar = ab_re.reshape(2, 1, S5_HALF_STATE)
    ai = ab_im.reshape(2, 1, S5_HALF_STATE)
    return bm.astype(jnp.bfloat16), cm.astype(jnp.bfloat16), ar, ai


def _out_ffn2_kernel(x1_ref, a_ref, s_ref, ga_ref, gs_ref, wo_ref, g2_ref, wg_ref, wu_ref, wd_ref,
                     gf_ref, o_ref):
    a_t = a_ref[0]
    r_a = lax.rsqrt(jnp.mean(a_t * a_t, axis=0, keepdims=True) + EPS)
    a = ((a_t * r_a).T * ga_ref[...]).astype(jnp.bfloat16)
    s = _rmsnorm(s_ref[...], gs_ref[...]).astype(jnp.bfloat16)
    x2 = x1_ref[...] + _dot(a, wo_ref[:ATTN_WIDTH, :]) + _dot(s, wo_ref[ATTN_WIDTH:, :])
    h = _rmsnorm(x2, g2_ref[...]).astype(jnp.bfloat16)
    x3 = x2 + 0.5 * _swiglu_acc(h, wg_ref, wu_ref, wd_ref)
    o_ref[...] = _rmsnorm(x3, gf_ref[...])


def _out_ffn2(x1, attn_t, ssm, ga, gs, wo, g2, wg, wu, wd, gf, bsz, t):
    n = bsz * t
    tm = min(TOKEN_TILE, t)
    nt = t // tm
    row = lambda i: (i, 0)
    return pl.pallas_call(
        _out_ffn2_kernel,
        grid=(n // tm,),
        in_specs=[
            pl.BlockSpec((tm, D_MODEL), row),
            pl.BlockSpec((1, ATTN_WIDTH, tm), lambda i: (i // nt, 0, i % nt)),
            pl.BlockSpec((tm, SSM_WIDTH), row),
            _resident((1, ATTN_WIDTH)), _resident((1, SSM_WIDTH)),
            _resident((D_MODEL, D_MODEL)),
            _resident((1, D_MODEL)),
            _resident((D_MODEL, D_FF)), _resident((D_MODEL, D_FF)), _resident((D_FF, D_MODEL)),
            _resident((1, D_MODEL)),
        ],
        out_specs=pl.BlockSpec((tm, D_MODEL), row),
        out_shape=jax.ShapeDtypeStruct((n, D_MODEL), jnp.float32),
        compiler_params=pltpu.CompilerParams(
            dimension_semantics=("parallel",), vmem_limit_bytes=VMEM_LIMIT),
    )(x1, attn_t, ssm, ga, gs, wo, g2, wg, wu, wd, gf)


def _reorder_w_in(w_in, gate_bias):
    per_g = N_GATE_COLS // N_KV
    g_lo = GATE_OFF
    pieces = [w_in[:, :g_lo]]
    bias = []
    gb = gate_bias.reshape(1, N_GATE_COLS)
    for g in range(N_KV):
        cols = w_in[:, g_lo + g * per_g:g_lo + (g + 1) * per_g]
        pieces.append(jnp.pad(cols, ((0, 0), (0, LANES - per_g))))
        bias.append(jnp.pad(gb[:, g * per_g:(g + 1) * per_g], ((0, 0), (0, LANES - per_g))))
    pieces.append(w_in[:, g_lo + N_GATE_COLS:])
    return jnp.concatenate(pieces, axis=1).astype(jnp.bfloat16), jnp.concatenate(bias, axis=1)


def kernel(x, ffn1_norm, ffn1_w_gate, ffn1_w_up, ffn1_w_down, mix_norm, w_in, gate_bias, cmp_pos_k, cmp_pos_v, cmp_k_w1, cmp_k_b1, cmp_k_w2, cmp_v_w1, cmp_v_b1, cmp_v_w2, s5_lambda_re, s5_lambda_im, s5_log_dt, s5_b_re, s5_b_im, s5_c_re, s5_c_im, s5_d, s5_w_glu, s5_b_glu, attn_out_norm, ssm_out_norm, w_out, ffn2_norm, ffn2_w_gate, ffn2_w_up, ffn2_w_down, final_norm):
    bsz, t, _ = x.shape
    n = bsz * t
    bf = jnp.bfloat16
    assert ffn1_norm.shape[0] == 1, "the final rmsnorm is fused into the single layer's last kernel"
    assert t % max(TOKEN_TILE, WINDOW) == 0 and t >= WINDOW + Q_BLOCK and x.shape[2] == D_MODEL
    xc = x.reshape(n, D_MODEL)
    for l in range(1):
        win, gb = _reorder_w_in(w_in[l], gate_bias[l])
        x1, q, kvc, ks, vst, kw, vwt, gates, u = _ffn1_proj(
            xc, ffn1_norm[l][None], ffn1_w_gate[l].astype(bf), ffn1_w_up[l].astype(bf),
            ffn1_w_down[l].astype(bf), mix_norm[l][None], win, gb, bsz, t)

        kvc4 = kvc.reshape(2 * N_KV, bsz, t // CMP_STRIDE, CMP_STRIDE * HEAD_DIM)
        pos = jnp.stack([cmp_pos_k[l], cmp_pos_v[l]]).reshape(2, 1, CMP_LEN * HEAD_DIM)
        w1 = jnp.stack([cmp_k_w1[l], cmp_v_w1[l]]).astype(bf)
        b1 = jnp.stack([cmp_k_b1[l], cmp_v_b1[l]])[:, None, :]
        w2 = jnp.stack([cmp_k_w2[l], cmp_v_w2[l]]).astype(bf)
        kcmp, kcmp_t = _compress(kvc4, pos, w1, b1, w2)

        attn_t = _nsa(q, kcmp, kcmp_t, ks, vst, kw, vwt, gates, bsz, t)

        bm, cm, ar, ai = _s5_params(s5_lambda_re[l], s5_lambda_im[l], s5_log_dt[l], s5_b_re[l],
                                    s5_b_im[l], s5_c_re[l], s5_c_im[l])
        ssm = _s5(u.reshape(bsz, t, SSM_WIDTH), bm, cm, ar, ai,
                  s5_d[l].reshape(1, SSM_WIDTH), s5_w_glu[l].astype(bf), s5_b_glu[l][None])

        xc = _out_ffn2(x1, attn_t, ssm.reshape(n, SSM_WIDTH), attn_out_norm[l][None],
                       ssm_out_norm[l][None], w_out[l].astype(bf), ffn2_norm[l][None],
                       ffn2_w_gate[l].astype(bf), ffn2_w_up[l].astype(bf), ffn2_w_down[l].astype(bf),
                       final_norm[None], bsz, t)
    return xc.reshape(bsz, t, D_MODEL)
```

```python
import functools
import math

import jax
import jax.numpy as jnp
from jax import lax
from jax.experimental import pallas as pl
from jax.experimental.pallas import tpu as pltpu

D_MODEL = 1024
HEAD_DIM = 64
N_HEADS = 8
N_KV = 2
GQA_R = N_HEADS // N_KV
ATTN_WIDTH = N_HEADS * HEAD_DIM
CMP_LEN = 32
CMP_STRIDE = 16
CMP_HIDDEN = 256
SEL_BLOCK = 64
SEL_SHIFT = 6
SEL_TOPK = 16
WINDOW = 512
Q_BLOCK = 256
SSM_WIDTH = 512
SSM_GROUP = 16
SSM_GROUPS = SSM_WIDTH // SSM_GROUP
SSM_STATE = 64
D_FF = 2816
EPS = 1e-6

LANES = 128
SUBLANES = 8
FF_CHUNK = 256
TOKEN_TILE = 512
SEL_KEY_TILE = 256
S5_TIME_CHUNK = 64
S5_LANE_CHUNK = 512
S5_HALF = SSM_WIDTH // 2
S5_HALF_STATE = (SSM_GROUPS // 2) * SSM_STATE
VMEM_LIMIT = 56 * 1024 * 1024
NEG_BIG = -1e30
Q_SCALE = HEAD_DIM ** -0.5 * math.log2(math.e)

Q_OFF = 0
KVC_OFF = 512
KVS_OFF = 768
KVW_OFF = 1024
GATE_OFF = 1280
U_OFF = 1536
IN_COLS_PADDED = 2048
N_GATE_COLS = N_HEADS * 3


def _dot(a, b):
    return jnp.dot(a, b, preferred_element_type=jnp.float32)


def _dot_nt(a, b):
    return lax.dot_general(a, b, (((1,), (1,)), ((), ())), preferred_element_type=jnp.float32)


def _rmsnorm(x, g):
    r = lax.rsqrt(jnp.mean(x * x, axis=-1, keepdims=True) + EPS)
    return (x * r) * g


def _swiglu_acc(h, wg_ref, wu_ref, wd_ref):
    acc = jnp.zeros((h.shape[0], D_MODEL), jnp.float32)
    for f in range(0, D_FF, FF_CHUNK):
        gate = _dot(h, wg_ref[:, f:f + FF_CHUNK])
        up = _dot(h, wu_ref[:, f:f + FF_CHUNK])
        act = (gate * jax.nn.sigmoid(gate) * up).astype(jnp.bfloat16)
        acc = acc + _dot(act, wd_ref[f:f + FF_CHUNK, :])
    return acc


def _resident(shape):
    nd = len(shape)
    return pl.BlockSpec(shape, lambda *_: (0,) * nd, pipeline_mode=pl.Buffered(1))


def _ffn1_proj_kernel(x_ref, g1_ref, wg_ref, wu_ref, wd_ref, gm_ref, win_ref, gb_ref,
                      x1_ref, q_ref, kvc_ref, ks_ref, vst_ref, kw_ref, vwt_ref, gate_ref, u_ref):
    x = x_ref[...]
    h = _rmsnorm(x, g1_ref[...]).astype(jnp.bfloat16)
    x1 = x + 0.5 * _swiglu_acc(h, wg_ref, wu_ref, wd_ref)
    x1_ref[...] = x1
    h2 = _rmsnorm(x1, gm_ref[...]).astype(jnp.bfloat16)
    p = _dot(h2, win_ref[...])
    for hh in range(N_HEADS):
        lo = Q_OFF + hh * HEAD_DIM
        q_ref[hh] = (p[:, lo:lo + HEAD_DIM] * Q_SCALE).astype(q_ref.dtype)
    for i in range(2 * N_KV):
        kvc_ref[i] = p[:, KVC_OFF + i * HEAD_DIM:KVC_OFF + (i + 1) * HEAD_DIM]
    for g in range(N_KV):
        for off, k_ref, vt_ref in ((KVS_OFF, ks_ref, vst_ref), (KVW_OFF, kw_ref, vwt_ref)):
            k_lo = off + g * HEAD_DIM
            v_lo = off + (N_KV + g) * HEAD_DIM
            k_ref[g] = p[:, k_lo:k_lo + HEAD_DIM].astype(k_ref.dtype)
            vt_ref[g, 0] = p[:, v_lo:v_lo + HEAD_DIM].T.astype(vt_ref.dtype)
    for g in range(N_KV):
        lo = GATE_OFF + g * LANES
        gate = jax.nn.sigmoid(p[:, lo:lo + LANES] + gb_ref[:, g * LANES:(g + 1) * LANES])
        gate_ref[g, 0] = gate.T
    u_ref[...] = p[:, U_OFF:U_OFF + SSM_WIDTH]


def _ffn1_proj(x2d, g1, wg, wu, wd, gm, win, gb, bsz, t):
    n = bsz * t
    tm = min(TOKEN_TILE, t)
    nt = t // tm
    row = lambda i: (i, 0)
    lead = lambda i: (0, i, 0)
    k_spec = pl.BlockSpec((N_KV, tm, HEAD_DIM), lead)
    vt_spec = pl.BlockSpec((N_KV, 1, HEAD_DIM, tm), lambda i: (0, i // nt, 0, i % nt))
    k_shape = jax.ShapeDtypeStruct((N_KV, n, HEAD_DIM), jnp.bfloat16)
    vt_shape = jax.ShapeDtypeStruct((N_KV, bsz, HEAD_DIM, t), jnp.bfloat16)
    return pl.pallas_call(
        _ffn1_proj_kernel,
        grid=(n // tm,),
        in_specs=[
            pl.BlockSpec((tm, D_MODEL), row),
            _resident((1, D_MODEL)),
            _resident((D_MODEL, D_FF)), _resident((D_MODEL, D_FF)), _resident((D_FF, D_MODEL)),
            _resident((1, D_MODEL)),
            _resident((D_MODEL, IN_COLS_PADDED)),
            _resident((1, N_KV * LANES)),
        ],
        out_specs=[
            pl.BlockSpec((tm, D_MODEL), row),
            pl.BlockSpec((N_HEADS, tm, HEAD_DIM), lead),
            pl.BlockSpec((2 * N_KV, tm, HEAD_DIM), lead),
            k_spec, vt_spec, k_spec, vt_spec,
            pl.BlockSpec((N_KV, 1, LANES, tm), lambda i: (0, i // nt, 0, i % nt)),
            pl.BlockSpec((tm, SSM_WIDTH), row),
        ],
        out_shape=[
            jax.ShapeDtypeStruct((n, D_MODEL), jnp.float32),
            jax.ShapeDtypeStruct((N_HEADS, n, HEAD_DIM), jnp.bfloat16),
            jax.ShapeDtypeStruct((2 * N_KV, n, HEAD_DIM), jnp.float32),
            k_shape, vt_shape, k_shape, vt_shape,
            jax.ShapeDtypeStruct((N_KV, bsz, LANES, t), jnp.float32),
            jax.ShapeDtypeStruct((n, SSM_WIDTH), jnp.float32),
        ],
        compiler_params=pltpu.CompilerParams(
            dimension_semantics=("parallel",), vmem_limit_bytes=VMEM_LIMIT),
    )(x2d, g1, wg, wu, wd, gm, win, gb)


def _compress_kernel(h_ref, pos_ref, w1_ref, b1_ref, w2_ref, o_ref, ot_ref):
    half = CMP_STRIDE * HEAD_DIM
    hrows = h_ref[0, 0]
    top = (hrows + pos_ref[0, :, :half]).astype(jnp.bfloat16)
    bot = (hrows + pos_ref[0, :, half:]).astype(jnp.bfloat16)
    a = _dot(top, w1_ref[0, :half, :])
    b = _dot(bot, w1_ref[0, half:, :])
    nrow = hrows.shape[0]
    pre = a + pltpu.roll(b, nrow - 1, 0) + b1_ref[0]
    out = _dot(jax.nn.gelu(pre).astype(jnp.bfloat16), w2_ref[0])
    rid = lax.broadcasted_iota(jnp.int32, out.shape, 0)
    out = jnp.where(rid < nrow - 1, out, 0.0)
    o_ref[0, 0] = out
    ot_ref[0, 0] = out.T


def _compress(kvc4, pos, w1, b1, w2):
    _, bsz, nrow, half = kvc4.shape
    kind = lambda i, b: (i // N_KV, 0, 0)
    return pl.pallas_call(
        _compress_kernel,
        grid=(2 * N_KV, bsz),
        in_specs=[
            pl.BlockSpec((1, 1, nrow, half), lambda i, b: (i, b, 0, 0)),
            pl.BlockSpec((1, 1, CMP_LEN * HEAD_DIM), kind),
            pl.BlockSpec((1, CMP_LEN * HEAD_DIM, CMP_HIDDEN), kind),
            pl.BlockSpec((1, 1, CMP_HIDDEN), kind),
            pl.BlockSpec((1, CMP_HIDDEN, HEAD_DIM), kind),
        ],
        out_specs=[pl.BlockSpec((1, 1, nrow, HEAD_DIM), lambda i, b: (i, b, 0, 0)),
                   pl.BlockSpec((1, 1, HEAD_DIM, nrow), lambda i, b: (i, b, 0, 0))],
        out_shape=[jax.ShapeDtypeStruct((2 * N_KV, bsz, nrow, HEAD_DIM), jnp.float32),
                   jax.ShapeDtypeStruct((2 * N_KV, bsz, HEAD_DIM, nrow), jnp.float32)],
        compiler_params=pltpu.CompilerParams(dimension_semantics=("parallel", "parallel")),
    )(kvc4, pos, w1, b1, w2)


def _split3_bf16(x):
    hi = x.astype(jnp.bfloat16)
    r1 = x - hi.astype(jnp.float32)
    mid = r1.astype(jnp.bfloat16)
    lo = (r1 - mid.astype(jnp.float32)).astype(jnp.bfloat16)
    return hi, mid, lo


def _sweep_t(q, k_ref, vt_ref, n_tiles, tile, bias_fn, s_refs, p_refs):
    rows = q.shape[0]
    last_tile = k_ref.shape[1] // tile - 1

    def k_tile(i):
        return k_ref[0, pl.ds(pl.multiple_of(i * tile, tile), tile), :]

    def vt_tile(i):
        return vt_ref[0, 0, :, pl.ds(pl.multiple_of(i * tile, tile), tile)]

    def scores_into(i, s_ref):
        s = _dot_nt(k_tile(i), q)
        bias = bias_fn(i)
        s = jnp.concatenate([s[:, r * Q_BLOCK:(r + 1) * Q_BLOCK] + bias
                             for r in range(rows // Q_BLOCK)], axis=1)
        s_ref[...] = s
        return jnp.max(s, axis=0, keepdims=True)

    def phase(i, carry, s_cur, s_nxt, p_cur, p_prv):
        m, l, acc, s_max = carry
        s_max_next = scores_into(jnp.minimum(i + 1, last_tile), s_nxt)
        acc = acc + _dot(vt_tile(jnp.maximum(i - 1, 0)), p_prv[...])
        m_new = jnp.maximum(m, s_max)
        alpha = jnp.exp2(m - m_new)
        p = jnp.exp2(s_cur[...] - m_new)
        p_cur[...] = p.astype(p_cur.dtype)
        l = alpha * l + jnp.sum(p, axis=0, keepdims=True)
        return m_new, l, alpha * acc, s_max_next

    def body(j, carry):
        carry = phase(2 * j, carry, s_refs[0], s_refs[1], p_refs[0], p_refs[1])
        return phase(2 * j + 1, carry, s_refs[1], s_refs[0], p_refs[1], p_refs[0])

    p_refs[1][...] = jnp.zeros_like(p_refs[1])
    init = (jnp.full((1, rows), NEG_BIG, jnp.float32), jnp.zeros((1, rows), jnp.float32),
            jnp.zeros((HEAD_DIM, rows), jnp.float32), scores_into(0, s_refs[0]))
    n_pairs = (n_tiles + 1) // 2
    _, l, acc, _ = lax.fori_loop(0, n_pairs, body, init)
    return (acc + _dot(vt_tile(2 * n_pairs - 1), p_refs[1][...])) / l


def _nsa_kernel(q_ref, kc_ref, vct_ref, ks_ref, vst_ref, kw_ref, vwt_ref, gate_ref, o_ref,
                bias_ref, s0_ref, s1_ref, p0_ref, p1_ref):
    c = pl.program_id(2)
    rows = GQA_R * Q_BLOCK
    q = q_ref[...].reshape(rows, HEAD_DIM)
    ncp = kc_ref.shape[2]
    t = ks_ref.shape[1]
    nb = t // SEL_BLOCK

    kc = kc_ref[0, 0].astype(jnp.bfloat16)
    vct = vct_ref[0, 0].astype(jnp.bfloat16)
    s_c = _dot_nt(kc, q)
    n_c = lax.broadcasted_iota(jnp.int32, (ncp, Q_BLOCK), 0)
    tq_c = c * Q_BLOCK + lax.broadcasted_iota(jnp.int32, (ncp, Q_BLOCK), 1)
    bias_c = jnp.where(n_c * CMP_STRIDE + (CMP_LEN - 1) <= tq_c, 0.0, NEG_BIG)
    s_c = jnp.concatenate([s_c[:, r * Q_BLOCK:(r + 1) * Q_BLOCK] + bias_c for r in range(GQA_R)], axis=1)
    e_c = jnp.exp2(s_c - jnp.max(s_c, axis=0, keepdims=True))
    tq_row = c * Q_BLOCK + (lax.broadcasted_iota(jnp.int32, (1, rows), 1) & (Q_BLOCK - 1))
    inv_c = jnp.where(tq_row >= CMP_LEN - 1, 1.0 / jnp.sum(e_c, axis=0, keepdims=True), 0.0)
    p_c = e_c * inv_c
    o_cmp = _dot(vct, p_c.astype(jnp.bfloat16))

    band = WINDOW + Q_BLOCK
    w0 = pl.multiple_of(jnp.maximum(c * Q_BLOCK - WINDOW, 0), Q_BLOCK)
    kw = kw_ref[0, pl.ds(w0, band), :]
    vwt = vwt_ref[0, 0, :, pl.ds(w0, band)]
    kpos_w = w0 + lax.broadcasted_iota(jnp.int32, (band, Q_BLOCK), 0)
    tq_w = c * Q_BLOCK + lax.broadcasted_iota(jnp.int32, (band, Q_BLOCK), 1)
    bias_w = jnp.where((kpos_w <= tq_w) & (kpos_w > tq_w - WINDOW), 0.0, NEG_BIG)
    s_w = _dot_nt(kw, q)
    s_w = jnp.concatenate([s_w[:, r * Q_BLOCK:(r + 1) * Q_BLOCK] + bias_w for r in range(GQA_R)], axis=1)
    p_w = jnp.exp2(s_w - jnp.max(s_w, axis=0, keepdims=True))
    o_win = _dot(vwt, p_w.astype(jnp.bfloat16)) / jnp.sum(p_w, axis=0, keepdims=True)

    p_sum = p_c[:, 0:Q_BLOCK]
    for r in range(1, GQA_R):
        p_sum = p_sum + p_c[:, r * Q_BLOCK:(r + 1) * Q_BLOCK]
    jb = lax.broadcasted_iota(jnp.int32, (nb, ncp), 0)
    nn = lax.broadcasted_iota(jnp.int32, (nb, ncp), 1)
    overlap_t = ((nn * CMP_STRIDE < jb * SEL_BLOCK + SEL_BLOCK)
                 & (nn * CMP_STRIDE + CMP_LEN > jb * SEL_BLOCK)
                 & (nn < ncp - 1))
    overlap_t = jnp.where(overlap_t, 1.0, 0.0).astype(jnp.bfloat16)
    hi, mid, lo = _split3_bf16(p_sum)
    imp = _dot(overlap_t, hi) + _dot(overlap_t, mid) + _dot(overlap_t, lo)

    j_t = lax.broadcasted_iota(jnp.int32, (nb, Q_BLOCK), 0)
    tq_t = c * Q_BLOCK + lax.broadcasted_iota(jnp.int32, (nb, Q_BLOCK), 1)
    cur = tq_t >> SEL_SHIFT
    forced = (j_t == 0) | (j_t == cur) | (j_t == cur - 1)
    valid = j_t * SEL_BLOCK <= tq_t
    score = jnp.where(forced, jnp.inf, jnp.where(valid, imp, -jnp.inf))
    groups = [slice(v, v + SUBLANES) for v in range(0, nb, SUBLANES)]
    j_grp = lax.broadcasted_iota(jnp.int32, (SUBLANES, LANES), 0)
    for lt in range(0, Q_BLOCK, LANES):
        sc = score[:, lt:lt + LANES]
        ranks = [jnp.zeros((SUBLANES, LANES), jnp.float32) for _ in groups]
        for jp in range(nb):
            row = jnp.broadcast_to(sc[jp:jp + 1, :], (SUBLANES, LANES))
            for v, grp in enumerate(groups):
                if jp < grp.start:
                    beats = row >= sc[grp]
                elif jp >= grp.stop:
                    beats = row > sc[grp]
                else:
                    beats = (row > sc[grp]) | ((row == sc[grp]) & (j_grp > jp - grp.start))
                ranks[v] = ranks[v] + jnp.where(beats, 1.0, 0.0)
        rank = jnp.concatenate(ranks, axis=0)
        bias_ref[:, lt:lt + LANES] = jnp.where(rank < min(SEL_TOPK, nb), 0.0, NEG_BIG)

    kpos_s = lax.broadcasted_iota(jnp.int32, (SEL_KEY_TILE, Q_BLOCK), 0)
    tq_s = c * Q_BLOCK + lax.broadcasted_iota(jnp.int32, (SEL_KEY_TILE, Q_BLOCK), 1)
    blocks_per_tile = SEL_KEY_TILE // SEL_BLOCK

    def sel_bias(kt):
        blocks = [jnp.broadcast_to(bias_ref[pl.ds(kt * blocks_per_tile + jj, 1), :], (SEL_BLOCK, Q_BLOCK))
                  for jj in range(blocks_per_tile)]
        return jnp.where(kt * SEL_KEY_TILE + kpos_s <= tq_s, jnp.concatenate(blocks, axis=0), NEG_BIG)

    n_sel_tiles = ((c + 1) * Q_BLOCK + SEL_KEY_TILE - 1) // SEL_KEY_TILE
    o_slc = _sweep_t(q, ks_ref, vst_ref, n_sel_tiles, SEL_KEY_TILE, sel_bias,
                     (s0_ref, s1_ref), (p0_ref, p1_ref))

    gt = gate_ref[0, 0]
    for r in range(GQA_R):
        sl = slice(r * Q_BLOCK, (r + 1) * Q_BLOCK)
        o_ref[0, r * HEAD_DIM:(r + 1) * HEAD_DIM, :] = (
            gt[3 * r:3 * r + 1] * o_cmp[:, sl] + gt[3 * r + 1:3 * r + 2] * o_slc[:, sl]
            + gt[3 * r + 2:3 * r + 3] * o_win[:, sl])


def _nsa(q, kcmp, kcmp_t, ks, vst, kw, vwt, gates, bsz, t):
    nq = t // Q_BLOCK
    ncp = kcmp.shape[2]
    rows = GQA_R * Q_BLOCK
    assert t % (2 * SEL_KEY_TILE) == 0, "the selected-block sweep runs pairs of key tiles"
    k_spec = pl.BlockSpec((1, t, HEAD_DIM), lambda b, g, c: (g, b, 0))
    vt_spec = pl.BlockSpec((1, 1, HEAD_DIM, t), lambda b, g, c: (g, b, 0, 0))
    return pl.pallas_call(
        _nsa_kernel,
        grid=(bsz, N_KV, nq),
        in_specs=[
            pl.BlockSpec((GQA_R, Q_BLOCK, HEAD_DIM), lambda b, g, c: (g, b * nq + c, 0)),
            pl.BlockSpec((1, 1, ncp, HEAD_DIM), lambda b, g, c: (g, b, 0, 0)),
            pl.BlockSpec((1, 1, HEAD_DIM, ncp), lambda b, g, c: (N_KV + g, b, 0, 0)),
            k_spec, vt_spec, k_spec, vt_spec,
            pl.BlockSpec((1, 1, LANES, Q_BLOCK), lambda b, g, c: (g, b, 0, c)),
        ],
        out_specs=pl.BlockSpec((1, GQA_R * HEAD_DIM, Q_BLOCK), lambda b, g, c: (b, g, c)),
        out_shape=jax.ShapeDtypeStruct((bsz, ATTN_WIDTH, t), jnp.float32),
        scratch_shapes=[pltpu.VMEM((t // SEL_BLOCK, Q_BLOCK), jnp.float32),
                        pltpu.VMEM((SEL_KEY_TILE, rows), jnp.float32),
                        pltpu.VMEM((SEL_KEY_TILE, rows), jnp.float32),
                        pltpu.VMEM((SEL_KEY_TILE, rows), jnp.bfloat16),
                        pltpu.VMEM((SEL_KEY_TILE, rows), jnp.bfloat16)],
        compiler_params=pltpu.CompilerParams(
            dimension_semantics=("parallel", "parallel", "parallel")),
    )(q, kcmp, kcmp_t, ks, vst, kw, vwt, gates)


def _s5_kernel(u_ref, bm_ref, cm_ref, ar_ref, ai_ref, d_ref, wglu_ref, bglu_ref, o_ref,
               utb_ref, bu_ref, state_ref):
    bsz, steps, _ = u_ref.shape

    @pl.when(pl.program_id(0) == 0)
    def _():
        state_ref[...] = jnp.zeros_like(state_ref)

    n_lane_tiles = utb_ref.shape[0]
    for b in range(bsz):
        for j in range(n_lane_tiles):
            utb_ref[j, pl.ds(b, steps, stride=bsz), :] = u_ref[b, :, j * LANES:(j + 1) * LANES]
    u = jnp.concatenate([utb_ref[j] for j in range(n_lane_tiles)], axis=1)
    ub = u.astype(jnp.bfloat16)
    ys = []
    for half in range(2):
        bu_ref[...] = _dot(ub[:, half * S5_HALF:(half + 1) * S5_HALF], bm_ref[half])
        for lc in range(0, S5_HALF_STATE, S5_LANE_CHUNK):
            re = slice(lc, lc + S5_LANE_CHUNK)
            im = slice(S5_HALF_STATE + lc, S5_HALF_STATE + lc + S5_LANE_CHUNK)
            a_re = jnp.broadcast_to(ar_ref[half, :, re], (bsz, S5_LANE_CHUNK))
            a_im = jnp.broadcast_to(ai_ref[half, :, re], (bsz, S5_LANE_CHUNK))

            def step(i, carry):
                xr, xi = carry
                rows = pl.ds(pl.multiple_of(i * bsz, bsz), bsz)
                nr = a_re * xr - a_im * xi + bu_ref[rows, re]
                ni = a_re * xi + a_im * xr + bu_ref[rows, im]
                bu_ref[rows, re] = nr
                bu_ref[rows, im] = ni
                return nr, ni

            xr, xi = lax.fori_loop(0, steps, step, (state_ref[half, :, re], state_ref[half, :, im]),
                                   unroll=4)
            state_ref[half, :, re] = xr
            state_ref[half, :, im] = xi
        ys.append(_dot(bu_ref[...].astype(jnp.bfloat16), cm_ref[half]))
    y = jnp.concatenate(ys, axis=-1) + d_ref[...] * u
    y = jax.nn.gelu(y)
    out = y * jax.nn.sigmoid(_dot(y.astype(jnp.bfloat16), wglu_ref[...]) + bglu_ref[...])
    for j in range(n_lane_tiles):
        utb_ref[j] = out[:, j * LANES:(j + 1) * LANES]
    for b in range(bsz):
        for j in range(n_lane_tiles):
            o_ref[b, :, j * LANES:(j + 1) * LANES] = utb_ref[j, pl.ds(b, steps, stride=bsz), :]


def _s5(u, bm, cm, ar, ai, d, wglu, bglu):
    bsz, t, _ = u.shape
    rows = S5_TIME_CHUNK * bsz
    const3 = lambda i: (0, 0, 0)
    const2 = lambda i: (0, 0)
    io_spec = pl.BlockSpec((bsz, S5_TIME_CHUNK, SSM_WIDTH), lambda i: (0, i, 0))
    return pl.pallas_call(
        _s5_kernel,
        grid=(t // S5_TIME_CHUNK,),
        in_specs=[
            io_spec,
            pl.BlockSpec((2, S5_HALF, 2 * S5_HALF_STATE), const3),
            pl.BlockSpec((2, 2 * S5_HALF_STATE, S5_HALF), const3),
            pl.BlockSpec((2, 1, S5_HALF_STATE), const3),
            pl.BlockSpec((2, 1, S5_HALF_STATE), const3),
            pl.BlockSpec((1, SSM_WIDTH), const2),
            pl.BlockSpec((SSM_WIDTH, SSM_WIDTH), const2),
            pl.BlockSpec((1, SSM_WIDTH), const2),
        ],
        out_specs=io_spec,
        out_shape=jax.ShapeDtypeStruct((bsz, t, SSM_WIDTH), jnp.float32),
        scratch_shapes=[
            pltpu.VMEM((SSM_WIDTH // LANES, rows, LANES), jnp.float32),
            pltpu.VMEM((rows, 2 * S5_HALF_STATE), jnp.float32),
            pltpu.VMEM((2, bsz, 2 * S5_HALF_STATE), jnp.float32),
        ],
        compiler_params=pltpu.CompilerParams(
            dimension_semantics=("arbitrary",), vmem_limit_bytes=VMEM_LIMIT),
    )(u, bm, cm, ar, ai, d, wglu, bglu)


def _s5_params(lam_re, lam_im, log_dt, b_re, b_im, c_re, c_im):
    dt = jnp.exp(log_dt)[:, None]
    mag = jnp.exp(lam_re * dt)
    ab_re = mag * jnp.cos(lam_im * dt)
    ab_im = mag * jnp.sin(lam_im * dt)
    nr = ab_re - 1.0
    den = lam_re * lam_re + lam_im * lam_im
    f_re = (nr * lam_re + ab_im * lam_im) / den
    f_im = (ab_im * lam_re - nr * lam_im) / den
    bb_re = f_re[..., None] * b_re - f_im[..., None] * b_im
    bb_im = f_re[..., None] * b_im + f_im[..., None] * b_re
    gh = SSM_GROUPS // 2
    eye = jnp.eye(gh, dtype=jnp.float32)

    def in_half(bb):
        return jnp.einsum('gph,gk->ghkp', bb, eye).reshape(S5_HALF, S5_HALF_STATE)

    def out_half(cc):
        return jnp.einsum('ghp,gk->gpkh', cc, eye).reshape(S5_HALF_STATE, S5_HALF)

    bm = jnp.stack([jnp.concatenate([in_half(bb_re[s]), in_half(bb_im[s])], axis=1)
                    for s in (slice(0, gh), slice(gh, None))])
    cm = jnp.stack([jnp.concatenate([out_half(c_re[s]), -out_half(c_im[s])], axis=0)
                    for s in (slice(0, gh), slice(gh, None))])
    ar = ab_re.reshape(2, 1, S5_HALF_STATE)
    ai = ab_im.reshape(2, 1, S5_HALF_STATE)
    return bm.astype(jnp.bfloat16), cm.astype(jnp.bfloat16), ar, ai


def _out_ffn2_kernel(x1_ref, a_ref, s_ref, ga_ref, gs_ref, wo_ref, g2_ref, wg_ref, wu_ref, wd_ref,
                     gf_ref, o_ref):
    a_t = a_ref[0]
    r_a = lax.rsqrt(jnp.mean(a_t * a_t, axis=0, keepdims=True) + EPS)
    a = ((a_t * r_a).T * ga_ref[...]).astype(jnp.bfloat16)
    s = _rmsnorm(s_ref[...], gs_ref[...]).astype(jnp.bfloat16)
    x2 = x1_ref[...] + _dot(a, wo_ref[:ATTN_WIDTH, :]) + _dot(s, wo_ref[ATTN_WIDTH:, :])
    h = _rmsnorm(x2, g2_ref[...]).astype(jnp.bfloat16)
    x3 = x2 + 0.5 * _swiglu_acc(h, wg_ref, wu_ref, wd_ref)
    o_ref[...] = _rmsnorm(x3, gf_ref[...])


def _out_ffn2(x1, attn_t, ssm, ga, gs, wo, g2, wg, wu, wd, gf, bsz, t):
    n = bsz * t
    tm = min(TOKEN_TILE, t)
    nt = t // tm
    row = lambda i: (i, 0)
    return pl.pallas_call(
        _out_ffn2_kernel,
        grid=(n // tm,),
        in_specs=[
            pl.BlockSpec((tm, D_MODEL), row),
            pl.BlockSpec((1, ATTN_WIDTH, tm), lambda i: (i // nt, 0, i % nt)),
            pl.BlockSpec((tm, SSM_WIDTH), row),
            _resident((1, ATTN_WIDTH)), _resident((1, SSM_WIDTH)),
            _resident((D_MODEL, D_MODEL)),
            _resident((1, D_MODEL)),
            _resident((D_MODEL, D_FF)), _resident((D_MODEL, D_FF)), _resident((D_FF, D_MODEL)),
            _resident((1, D_MODEL)),
        ],
        out_specs=pl.BlockSpec((tm, D_MODEL), row),
        out_shape=jax.ShapeDtypeStruct((n, D_MODEL), jnp.float32),
        compiler_params=pltpu.CompilerParams(
            dimension_semantics=("parallel",), vmem_limit_bytes=VMEM_LIMIT),
    )(x1, attn_t, ssm, ga, gs, wo, g2, wg, wu, wd, gf)


def _reorder_w_in(w_in, gate_bias):
    per_g = N_GATE_COLS // N_KV
    g_lo = GATE_OFF
    pieces = [w_in[:, :g_lo]]
    bias = []
    gb = gate_bias.reshape(1, N_GATE_COLS)
    for g in range(N_KV):
        cols = w_in[:, g_lo + g * per_g:g_lo + (g + 1) * per_g]
        pieces.append(jnp.pad(cols, ((0, 0), (0, LANES - per_g))))
        bias.append(jnp.pad(gb[:, g * per_g:(g + 1) * per_g], ((0, 0), (0, LANES - per_g))))
    pieces.append(w_in[:, g_lo + N_GATE_COLS:])
    return jnp.concatenate(pieces, axis=1).astype(jnp.bfloat16), jnp.concatenate(bias, axis=1)


def kernel(x, ffn1_norm, ffn1_w_gate, ffn1_w_up, ffn1_w_down, mix_norm, w_in, gate_bias, cmp_pos_k, cmp_pos_v, cmp_k_w1, cmp_k_b1, cmp_k_w2, cmp_v_w1, cmp_v_b1, cmp_v_w2, s5_lambda_re, s5_lambda_im, s5_log_dt, s5_b_re, s5_b_im, s5_c_re, s5_c_im, s5_d, s5_w_glu, s5_b_glu, attn_out_norm, ssm_out_norm, w_out, ffn2_norm, ffn2_w_gate, ffn2_w_up, ffn2_w_down, final_norm):
    bsz, t, _ = x.shape
    n = bsz * t
    bf = jnp.bfloat16
    assert ffn1_norm.shape[0] == 1, "the final rmsnorm is fused into the single layer's last kernel"
    assert t % max(TOKEN_TILE, WINDOW) == 0 and t >= WINDOW + Q_BLOCK and x.shape[2] == D_MODEL
    xc = x.reshape(n, D_MODEL)
    for l in range(1):
        win, gb = _reorder_w_in(w_in[l], gate_bias[l])
        x1, q, kvc, ks, vst, kw, vwt, gates, u = _ffn1_proj(
            xc, ffn1_norm[l][None], ffn1_w_gate[l].astype(bf), ffn1_w_up[l].astype(bf),
            ffn1_w_down[l].astype(bf), mix_norm[l][None], win, gb, bsz, t)

        kvc4 = kvc.reshape(2 * N_KV, bsz, t // CMP_STRIDE, CMP_STRIDE * HEAD_DIM)
        pos = jnp.stack([cmp_pos_k[l], cmp_pos_v[l]]).reshape(2, 1, CMP_LEN * HEAD_DIM)
        w1 = jnp.stack([cmp_k_w1[l], cmp_v_w1[l]]).astype(bf)
        b1 = jnp.stack([cmp_k_b1[l], cmp_v_b1[l]])[:, None, :]
        w2 = jnp.stack([cmp_k_w2[l], cmp_v_w2[l]]).astype(bf)
        kcmp, kcmp_t = _compress(kvc4, pos, w1, b1, w2)

        attn_t = _nsa(q, kcmp, kcmp_t, ks, vst, kw, vwt, gates, bsz, t)

        bm, cm, ar, ai = _s5_params(s5_lambda_re[l], s5_lambda_im[l], s5_log_dt[l], s5_b_re[l],
                                    s5_b_im[l], s5_c_re[l], s5_c_im[l])
        ssm = _s5(u.reshape(bsz, t, SSM_WIDTH), bm, cm, ar, ai,
                  s5_d[l].reshape(1, SSM_WIDTH), s5_w_glu[l].astype(bf), s5_b_glu[l][None])

        xc = _out_ffn2(x1, attn_t, ssm.reshape(n, SSM_WIDTH), attn_out_norm[l][None],
                       ssm_out_norm[l][None], w_out[l].astype(bf), ffn2_norm[l][None],
                       ffn2_w_gate[l].astype(bf), ffn2_w_up[l].astype(bf), ffn2_w_down[l].astype(bf),
                       final_norm[None], bsz, t)
    return xc.reshape(bsz, t, D_MODEL)
```

```python
import functools
import math

import jax
import jax.numpy as jnp
from jax import lax
from jax.experimental import pallas as pl
from jax.experimental.pallas import tpu as pltpu

D_MODEL = 1024
HEAD_DIM = 64
N_HEADS = 8
N_KV = 2
GQA_R = N_HEADS // N_KV
ATTN_WIDTH = N_HEADS * HEAD_DIM
CMP_LEN = 32
CMP_STRIDE = 16
CMP_HIDDEN = 256
SEL_BLOCK = 64
SEL_SHIFT = 6
SEL_TOPK = 16
WINDOW = 512
Q_BLOCK = 256
SSM_WIDTH = 512
SSM_GROUP = 16
SSM_GROUPS = SSM_WIDTH // SSM_GROUP
SSM_STATE = 64
D_FF = 2816
EPS = 1e-6

LANES = 128
SUBLANES = 8
FF_CHUNK = 256
TOKEN_TILE = 512
SEL_KEY_TILE = 256
S5_TIME_CHUNK = 64
PV_ROWS = HEAD_DIM + 16
TOPK_CHUNK = 16
S5_HALF = SSM_WIDTH // 2
S5_HALF_STATE = (SSM_GROUPS // 2) * SSM_STATE
VMEM_LIMIT = 56 * 1024 * 1024
NEG_BIG = -1e30
Q_SCALE = HEAD_DIM ** -0.5 * math.log2(math.e)

Q_OFF = 0
KVC_OFF = 512
KVS_OFF = 768
KVW_OFF = 1024
GATE_OFF = 1280
U_OFF = 1536
IN_COLS_PADDED = 2048
N_GATE_COLS = N_HEADS * 3


def _dot(a, b):
    return jnp.dot(a, b, preferred_element_type=jnp.float32)


def _dot_nt(a, b):
    return lax.dot_general(a, b, (((1,), (1,)), ((), ())), preferred_element_type=jnp.float32)


def _rmsnorm(x, g):
    r = lax.rsqrt(jnp.mean(x * x, axis=-1, keepdims=True) + EPS)
    return (x * r) * g


def _swiglu_acc(h, wg_ref, wu_ref, wd_ref):
    acc = jnp.zeros((h.shape[0], D_MODEL), jnp.float32)
    for f in range(0, D_FF, FF_CHUNK):
        gate = _dot(h, wg_ref[:, f:f + FF_CHUNK])
        up = _dot(h, wu_ref[:, f:f + FF_CHUNK])
        act = (gate * jax.nn.sigmoid(gate) * up).astype(jnp.bfloat16)
        acc = acc + _dot(act, wd_ref[f:f + FF_CHUNK, :])
    return acc


def _resident(shape):
    nd = len(shape)
    return pl.BlockSpec(shape, lambda *_: (0,) * nd, pipeline_mode=pl.Buffered(1))


def _ffn1_proj_kernel(x_ref, g1_ref, wg_ref, wu_ref, wd_ref, gm_ref, win_ref, gb_ref,
                      x1_ref, q_ref, kvc_ref, ks_ref, vst_ref, kw_ref, vwt_ref, gate_ref, u_ref):
    x = x_ref[...]
    h = _rmsnorm(x, g1_ref[...]).astype(jnp.bfloat16)
    x1 = x + 0.5 * _swiglu_acc(h, wg_ref, wu_ref, wd_ref)
    x1_ref[...] = x1
    h2 = _rmsnorm(x1, gm_ref[...]).astype(jnp.bfloat16)
    p = _dot(h2, win_ref[...])
    for hh in range(N_HEADS):
        lo = Q_OFF + hh * HEAD_DIM
        q_ref[hh] = (p[:, lo:lo + HEAD_DIM] * Q_SCALE).astype(q_ref.dtype)
    for i in range(2 * N_KV):
        kvc_ref[i] = p[:, KVC_OFF + i * HEAD_DIM:KVC_OFF + (i + 1) * HEAD_DIM]
    for g in range(N_KV):
        for off, k_ref, vt_ref in ((KVS_OFF, ks_ref, vst_ref), (KVW_OFF, kw_ref, vwt_ref)):
            k_lo = off + g * HEAD_DIM
            v_lo = off + (N_KV + g) * HEAD_DIM
            k_ref[g] = p[:, k_lo:k_lo + HEAD_DIM].astype(k_ref.dtype)
            vt_ref[g, 0, :HEAD_DIM] = p[:, v_lo:v_lo + HEAD_DIM].T.astype(vt_ref.dtype)
        pad_rows = lax.broadcasted_iota(jnp.int32, (vst_ref.shape[2] - HEAD_DIM, p.shape[0]), 0)
        vst_ref[g, 0, HEAD_DIM:] = jnp.where(pad_rows == 0, 1.0, 0.0).astype(vst_ref.dtype)
    for g in range(N_KV):
        lo = GATE_OFF + g * LANES
        gate = jax.nn.sigmoid(p[:, lo:lo + LANES] + gb_ref[:, g * LANES:(g + 1) * LANES])
        gate_ref[g, 0] = gate.T
    u_ref[...] = p[:, U_OFF:U_OFF + SSM_WIDTH]


def _ffn1_proj(x2d, g1, wg, wu, wd, gm, win, gb, bsz, t):
    n = bsz * t
    tm = min(TOKEN_TILE, t)
    nt = t // tm
    row = lambda i: (i, 0)
    lead = lambda i: (0, i, 0)
    k_spec = pl.BlockSpec((N_KV, tm, HEAD_DIM), lead)
    vt_spec = lambda nrow: pl.BlockSpec((N_KV, 1, nrow, tm), lambda i: (0, i // nt, 0, i % nt))
    k_shape = jax.ShapeDtypeStruct((N_KV, n, HEAD_DIM), jnp.bfloat16)
    vt_shape = lambda nrow: jax.ShapeDtypeStruct((N_KV, bsz, nrow, t), jnp.bfloat16)
    return pl.pallas_call(
        _ffn1_proj_kernel,
        grid=(n // tm,),
        in_specs=[
            pl.BlockSpec((tm, D_MODEL), row),
            _resident((1, D_MODEL)),
            _resident((D_MODEL, D_FF)), _resident((D_MODEL, D_FF)), _resident((D_FF, D_MODEL)),
            _resident((1, D_MODEL)),
            _resident((D_MODEL, IN_COLS_PADDED)),
            _resident((1, N_KV * LANES)),
        ],
        out_specs=[
            pl.BlockSpec((tm, D_MODEL), row),
            pl.BlockSpec((N_HEADS, tm, HEAD_DIM), lead),
            pl.BlockSpec((2 * N_KV, tm, HEAD_DIM), lead),
            k_spec, vt_spec(PV_ROWS), k_spec, vt_spec(HEAD_DIM),
            pl.BlockSpec((N_KV, 1, LANES, tm), lambda i: (0, i // nt, 0, i % nt)),
            pl.BlockSpec((tm, SSM_WIDTH), row),
        ],
        out_shape=[
            jax.ShapeDtypeStruct((n, D_MODEL), jnp.float32),
            jax.ShapeDtypeStruct((N_HEADS, n, HEAD_DIM), jnp.bfloat16),
            jax.ShapeDtypeStruct((2 * N_KV, n, HEAD_DIM), jnp.float32),
            k_shape, vt_shape(PV_ROWS), k_shape, vt_shape(HEAD_DIM),
            jax.ShapeDtypeStruct((N_KV, bsz, LANES, t), jnp.float32),
            jax.ShapeDtypeStruct((n, SSM_WIDTH), jnp.float32),
        ],
        compiler_params=pltpu.CompilerParams(
            dimension_semantics=("parallel",), vmem_limit_bytes=VMEM_LIMIT),
    )(x2d, g1, wg, wu, wd, gm, win, gb)


def _compress_kernel(h_ref, pos_ref, w1_ref, b1_ref, w2_ref, o_ref, ot_ref):
    half = CMP_STRIDE * HEAD_DIM
    hrows = h_ref[0, 0]
    top = (hrows + pos_ref[0, :, :half]).astype(jnp.bfloat16)
    bot = (hrows + pos_ref[0, :, half:]).astype(jnp.bfloat16)
    a = _dot(top, w1_ref[0, :half, :])
    b = _dot(bot, w1_ref[0, half:, :])
    nrow = hrows.shape[0]
    pre = a + pltpu.roll(b, nrow - 1, 0) + b1_ref[0]
    out = _dot(jax.nn.gelu(pre).astype(jnp.bfloat16), w2_ref[0])
    rid = lax.broadcasted_iota(jnp.int32, out.shape, 0)
    out = jnp.where(rid < nrow - 1, out, 0.0)
    o_ref[0, 0] = out
    ot_ref[0, 0] = out.T


def _compress(kvc4, pos, w1, b1, w2):
    _, bsz, nrow, half = kvc4.shape
    kind = lambda i, b: (i // N_KV, 0, 0)
    return pl.pallas_call(
        _compress_kernel,
        grid=(2 * N_KV, bsz),
        in_specs=[
            pl.BlockSpec((1, 1, nrow, half), lambda i, b: (i, b, 0, 0)),
            pl.BlockSpec((1, 1, CMP_LEN * HEAD_DIM), kind),
            pl.BlockSpec((1, CMP_LEN * HEAD_DIM, CMP_HIDDEN), kind),
            pl.BlockSpec((1, 1, CMP_HIDDEN), kind),
            pl.BlockSpec((1, CMP_HIDDEN, HEAD_DIM), kind),
        ],
        out_specs=[pl.BlockSpec((1, 1, nrow, HEAD_DIM), lambda i, b: (i, b, 0, 0)),
                   pl.BlockSpec((1, 1, HEAD_DIM, nrow), lambda i, b: (i, b, 0, 0))],
        out_shape=[jax.ShapeDtypeStruct((2 * N_KV, bsz, nrow, HEAD_DIM), jnp.float32),
                   jax.ShapeDtypeStruct((2 * N_KV, bsz, HEAD_DIM, nrow), jnp.float32)],
        compiler_params=pltpu.CompilerParams(dimension_semantics=("parallel", "parallel")),
    )(kvc4, pos, w1, b1, w2)


def _split3_bf16(x):
    hi = x.astype(jnp.bfloat16)
    r1 = x - hi.astype(jnp.float32)
    mid = r1.astype(jnp.bfloat16)
    lo = (r1 - mid.astype(jnp.float32)).astype(jnp.bfloat16)
    return hi, mid, lo


def _sweep_t(q_ref, k_ref, vt_ref, n_tiles, tile, bias_fn, s_refs, p_refs, acc_ref):
    rows = acc_ref.shape[1]
    last_tile = k_ref.shape[1] // tile - 1

    def k_tile(i):
        return k_ref[0, pl.ds(pl.multiple_of(i * tile, tile), tile), :]

    def vt_tile(i):
        return vt_ref[0, 0, :, pl.ds(pl.multiple_of(i * tile, tile), tile)]

    def scores_into(i, s_ref):
        s = _dot_nt(k_tile(i), q_ref[...].reshape(rows, HEAD_DIM))
        bias = bias_fn(i)
        s = jnp.concatenate([s[:, r * Q_BLOCK:(r + 1) * Q_BLOCK] + bias
                             for r in range(rows // Q_BLOCK)], axis=1)
        s_ref[...] = s
        return jnp.max(s, axis=0, keepdims=True)

    def phase(i, carry, s_cur, s_nxt, p_cur, p_prv):
        m, s_max = carry
        m_new = jnp.maximum(m, s_max)
        p_cur[...] = jnp.exp2(s_cur[...] - m_new).astype(p_cur.dtype)
        pv = _dot(vt_tile(jnp.maximum(i - 1, 0)), p_prv[...])
        s_max_next = scores_into(jnp.minimum(i + 1, last_tile), s_nxt)
        acc_ref[...] = jnp.exp2(m - m_new) * (acc_ref[...] + pv)
        return m_new, s_max_next

    def body(j, carry):
        carry = phase(2 * j, carry, s_refs[0], s_refs[1], p_refs[0], p_refs[1])
        return phase(2 * j + 1, carry, s_refs[1], s_refs[0], p_refs[1], p_refs[0])

    p_refs[1][...] = jnp.zeros_like(p_refs[1])
    acc_ref[...] = jnp.zeros_like(acc_ref)
    init = (jnp.full((1, rows), NEG_BIG, jnp.float32), scores_into(0, s_refs[0]))
    n_pairs = (n_tiles + 1) // 2
    lax.fori_loop(0, n_pairs, body, init)
    acc = acc_ref[...] + _dot(vt_tile(2 * n_pairs - 1), p_refs[1][...])
    return acc[:HEAD_DIM] / acc[HEAD_DIM:HEAD_DIM + 1]


def _nsa_kernel(q_ref, kc_ref, vct_ref, ks_ref, vst_ref, kw_ref, vwt_ref, gate_ref, o_ref,
                bias_ref, rank_ref, causal_ref, s0_ref, s1_ref, p0_ref, p1_ref, acc_ref):
    c = pl.program_id(2)
    rows = GQA_R * Q_BLOCK
    q = q_ref[...].reshape(rows, HEAD_DIM)
    ncp = kc_ref.shape[2]
    t = ks_ref.shape[1]
    nb = t // SEL_BLOCK

    kc = kc_ref[0, 0].astype(jnp.bfloat16)
    vct = vct_ref[0, 0].astype(jnp.bfloat16)
    s_c = _dot_nt(kc, q)
    n_c = lax.broadcasted_iota(jnp.int32, (ncp, Q_BLOCK), 0)
    tq_c = c * Q_BLOCK + lax.broadcasted_iota(jnp.int32, (ncp, Q_BLOCK), 1)
    bias_c = jnp.where(n_c * CMP_STRIDE + (CMP_LEN - 1) <= tq_c, 0.0, NEG_BIG)
    s_c = jnp.concatenate([s_c[:, r * Q_BLOCK:(r + 1) * Q_BLOCK] + bias_c for r in range(GQA_R)], axis=1)
    e_c = jnp.exp2(s_c - jnp.max(s_c, axis=0, keepdims=True))
    tq_row = c * Q_BLOCK + (lax.broadcasted_iota(jnp.int32, (1, rows), 1) & (Q_BLOCK - 1))
    inv_c = jnp.where(tq_row >= CMP_LEN - 1, 1.0 / jnp.sum(e_c, axis=0, keepdims=True), 0.0)
    p_c = e_c * inv_c
    o_cmp = _dot(vct, p_c.astype(jnp.bfloat16))

    band = WINDOW + Q_BLOCK
    w0 = pl.multiple_of(jnp.maximum(c * Q_BLOCK - WINDOW, 0), Q_BLOCK)
    kw = kw_ref[0, pl.ds(w0, band), :]
    vwt = vwt_ref[0, 0, :, pl.ds(w0, band)]
    kpos_w = w0 + lax.broadcasted_iota(jnp.int32, (band, Q_BLOCK), 0)
    tq_w = c * Q_BLOCK + lax.broadcasted_iota(jnp.int32, (band, Q_BLOCK), 1)
    bias_w = jnp.where((kpos_w <= tq_w) & (kpos_w > tq_w - WINDOW), 0.0, NEG_BIG)
    s_w = _dot_nt(kw, q)
    s_w = jnp.concatenate([s_w[:, r * Q_BLOCK:(r + 1) * Q_BLOCK] + bias_w for r in range(GQA_R)], axis=1)
    p_w = jnp.exp2(s_w - jnp.max(s_w, axis=0, keepdims=True))
    o_win = _dot(vwt, p_w.astype(jnp.bfloat16)) / jnp.sum(p_w, axis=0, keepdims=True)

    p_sum = p_c[:, 0:Q_BLOCK]
    for r in range(1, GQA_R):
        p_sum = p_sum + p_c[:, r * Q_BLOCK:(r + 1) * Q_BLOCK]
    jb = lax.broadcasted_iota(jnp.int32, (nb, ncp), 0)
    nn = lax.broadcasted_iota(jnp.int32, (nb, ncp), 1)
    overlap_t = ((nn * CMP_STRIDE < jb * SEL_BLOCK + SEL_BLOCK)
                 & (nn * CMP_STRIDE + CMP_LEN > jb * SEL_BLOCK)
                 & (nn < ncp - 1))
    overlap_t = jnp.where(overlap_t, 1.0, 0.0).astype(jnp.bfloat16)
    hi, mid, lo = _split3_bf16(p_sum)
    imp = _dot(overlap_t, hi) + _dot(overlap_t, mid) + _dot(overlap_t, lo)

    j_t = lax.broadcasted_iota(jnp.int32, (nb, Q_BLOCK), 0)
    tq_t = c * Q_BLOCK + lax.broadcasted_iota(jnp.int32, (nb, Q_BLOCK), 1)
    cur = tq_t >> SEL_SHIFT
    forced = (j_t == 0) | (j_t == cur) | (j_t == cur - 1)
    valid = j_t * SEL_BLOCK <= tq_t
    score = jnp.where(forced, jnp.inf, jnp.where(valid, imp, -jnp.inf))
    rank_ref[...] = jnp.zeros_like(rank_ref)
    j_grp = lax.broadcasted_iota(jnp.int32, (SUBLANES, LANES), 0)

    def count_chunk(cand, ranked):
        groups = [slice(v, v + SUBLANES) for v in range(ranked * TOPK_CHUNK, (ranked + 1) * TOPK_CHUNK, SUBLANES)]
        for lt in range(0, Q_BLOCK, LANES):
            sc = score[:, lt:lt + LANES]
            ranks = [rank_ref[grp, lt:lt + LANES] for grp in groups]
            for jp in range(cand * TOPK_CHUNK, (cand + 1) * TOPK_CHUNK):
                row = jnp.broadcast_to(sc[jp:jp + 1, :], (SUBLANES, LANES))
                for v, grp in enumerate(groups):
                    if jp < grp.start:
                        beats = row >= sc[grp]
                    elif jp >= grp.stop:
                        beats = row > sc[grp]
                    else:
                        beats = (row > sc[grp]) | ((row == sc[grp]) & (j_grp > jp - grp.start))
                    ranks[v] = ranks[v] + jnp.where(beats, 1.0, 0.0)
            for v, grp in enumerate(groups):
                rank_ref[grp, lt:lt + LANES] = ranks[v]

    blocks_seen = (c + 1) * (Q_BLOCK // SEL_BLOCK)
    for shell in range(nb // TOPK_CHUNK):
        @pl.when(blocks_seen > shell * TOPK_CHUNK)
        def _():
            for other in range(shell):
                count_chunk(shell, other)
                count_chunk(other, shell)
            count_chunk(shell, shell)

    bias_ref[...] = jnp.where(rank_ref[...] < min(SEL_TOPK, nb), 0.0, NEG_BIG)

    @pl.when(c == 0)
    def _():
        kpos = lax.broadcasted_iota(jnp.int32, (SEL_KEY_TILE, Q_BLOCK), 0)
        qpos = lax.broadcasted_iota(jnp.int32, (SEL_KEY_TILE, Q_BLOCK), 1)
        causal_ref[0] = jnp.zeros((SEL_KEY_TILE, Q_BLOCK), jnp.float32)
        causal_ref[1] = jnp.where(kpos <= qpos, 0.0, NEG_BIG)
        causal_ref[2] = jnp.full((SEL_KEY_TILE, Q_BLOCK), NEG_BIG, jnp.float32)

    blocks_per_tile = SEL_KEY_TILE // SEL_BLOCK

    def sel_bias(kt):
        blocks = [jnp.broadcast_to(bias_ref[pl.ds(kt * blocks_per_tile + jj, 1), :], (SEL_BLOCK, Q_BLOCK))
                  for jj in range(blocks_per_tile)]
        return jnp.concatenate(blocks, axis=0) + causal_ref[jnp.clip(kt - c + 1, 0, 2)]

    o_slc = _sweep_t(q_ref, ks_ref, vst_ref, c + 1, SEL_KEY_TILE, sel_bias,
                     (s0_ref, s1_ref), (p0_ref, p1_ref), acc_ref)

    gt = gate_ref[0, 0]
    for r in range(GQA_R):
        sl = slice(r * Q_BLOCK, (r + 1) * Q_BLOCK)
        o_ref[0, r * HEAD_DIM:(r + 1) * HEAD_DIM, :] = (
            gt[3 * r:3 * r + 1] * o_cmp[:, sl] + gt[3 * r + 1:3 * r + 2] * o_slc[:, sl]
            + gt[3 * r + 2:3 * r + 3] * o_win[:, sl])


def _nsa(q, kcmp, kcmp_t, ks, vst, kw, vwt, gates, bsz, t):
    nq = t // Q_BLOCK
    ncp = kcmp.shape[2]
    rows = GQA_R * Q_BLOCK
    assert t % (2 * SEL_KEY_TILE) == 0, "the selected-block sweep runs pairs of key tiles"
    assert SEL_KEY_TILE == Q_BLOCK, "the causal bias of a key tile is one of three fixed patterns"
    assert (t // SEL_BLOCK) % TOPK_CHUNK == 0
    k_spec = pl.BlockSpec((1, t, HEAD_DIM), lambda b, g, c: (g, b, 0))
    vt_spec = lambda nrow: pl.BlockSpec((1, 1, nrow, t), lambda b, g, c: (g, b, 0, 0))
    return pl.pallas_call(
        _nsa_kernel,
        grid=(bsz, N_KV, nq),
        in_specs=[
            pl.BlockSpec((GQA_R, Q_BLOCK, HEAD_DIM), lambda b, g, c: (g, b * nq + c, 0)),
            pl.BlockSpec((1, 1, ncp, HEAD_DIM), lambda b, g, c: (g, b, 0, 0)),
            pl.BlockSpec((1, 1, HEAD_DIM, ncp), lambda b, g, c: (N_KV + g, b, 0, 0)),
            k_spec, vt_spec(PV_ROWS), k_spec, vt_spec(HEAD_DIM),
            pl.BlockSpec((1, 1, LANES, Q_BLOCK), lambda b, g, c: (g, b, 0, c)),
        ],
        out_specs=pl.BlockSpec((1, GQA_R * HEAD_DIM, Q_BLOCK), lambda b, g, c: (b, g, c)),
        out_shape=jax.ShapeDtypeStruct((bsz, ATTN_WIDTH, t), jnp.float32),
        scratch_shapes=[pltpu.VMEM((t // SEL_BLOCK, Q_BLOCK), jnp.float32),
                        pltpu.VMEM((t // SEL_BLOCK, Q_BLOCK), jnp.float32),
                        pltpu.VMEM((3, SEL_KEY_TILE, Q_BLOCK), jnp.float32),
                        pltpu.VMEM((SEL_KEY_TILE, rows), jnp.float32),
                        pltpu.VMEM((SEL_KEY_TILE, rows), jnp.float32),
                        pltpu.VMEM((SEL_KEY_TILE, rows), jnp.bfloat16),
                        pltpu.VMEM((SEL_KEY_TILE, rows), jnp.bfloat16),
                        pltpu.VMEM((PV_ROWS, rows), jnp.float32)],
        compiler_params=pltpu.CompilerParams(
            dimension_semantics=("parallel", "parallel", "arbitrary")),
    )(q, kcmp, kcmp_t, ks, vst, kw, vwt, gates)


def _s5_kernel(u_ref, bm_ref, cm_ref, ar_ref, ai_ref, d_ref, wglu_ref, bglu_ref, o_ref,
               utb_ref, bu_ref, state_ref):
    bsz, steps, _ = u_ref.shape

    @pl.when(pl.program_id(0) == 0)
    def _():
        state_ref[...] = jnp.zeros_like(state_ref)

    n_lane_tiles = utb_ref.shape[0]
    for b in range(bsz):
        for j in range(n_lane_tiles):
            utb_ref[j, pl.ds(b, steps, stride=bsz), :] = u_ref[b, :, j * LANES:(j + 1) * LANES]
    u = jnp.concatenate([utb_ref[j] for j in range(n_lane_tiles)], axis=1)
    ub = u.astype(jnp.bfloat16)
    for half in range(2):
        bu_ref[half] = _dot(ub[:, half * S5_HALF:(half + 1) * S5_HALF], bm_ref[half])
    ys = []
    re = slice(0, S5_HALF_STATE)
    im = slice(S5_HALF_STATE, 2 * S5_HALF_STATE)
    for half in range(2):
        a_re = jnp.broadcast_to(ar_ref[half], (bsz, S5_HALF_STATE))
        a_im = jnp.broadcast_to(ai_ref[half], (bsz, S5_HALF_STATE))
        xr = state_ref[half, :, re]
        xi = state_ref[half, :, im]
        for i in range(steps):
            rows = slice(i * bsz, (i + 1) * bsz)
            xr, xi = (a_re * xr - a_im * xi + bu_ref[half, rows, re],
                      a_re * xi + a_im * xr + bu_ref[half, rows, im])
            bu_ref[half, rows, re] = xr
            bu_ref[half, rows, im] = xi
        state_ref[half, :, re] = xr
        state_ref[half, :, im] = xi
        ys.append(_dot(bu_ref[half].astype(jnp.bfloat16), cm_ref[half]))
    y = jnp.concatenate(ys, axis=-1) + d_ref[...] * u
    y = jax.nn.gelu(y)
    out = y * jax.nn.sigmoid(_dot(y.astype(jnp.bfloat16), wglu_ref[...]) + bglu_ref[...])
    for j in range(n_lane_tiles):
        utb_ref[j] = out[:, j * LANES:(j + 1) * LANES]
    for b in range(bsz):
        for j in range(n_lane_tiles):
            o_ref[b, :, j * LANES:(j + 1) * LANES] = utb_ref[j, pl.ds(b, steps, stride=bsz), :]


def _s5(u, bm, cm, ar, ai, d, wglu, bglu):
    bsz, t, _ = u.shape
    rows = S5_TIME_CHUNK * bsz
    const3 = lambda i: (0, 0, 0)
    const2 = lambda i: (0, 0)
    io_spec = pl.BlockSpec((bsz, S5_TIME_CHUNK, SSM_WIDTH), lambda i: (0, i, 0))
    return pl.pallas_call(
        _s5_kernel,
        grid=(t // S5_TIME_CHUNK,),
        in_specs=[
            io_spec,
            pl.BlockSpec((2, S5_HALF, 2 * S5_HALF_STATE), const3),
            pl.BlockSpec((2, 2 * S5_HALF_STATE, S5_HALF), const3),
            pl.BlockSpec((2, 1, S5_HALF_STATE), const3),
            pl.BlockSpec((2, 1, S5_HALF_STATE), const3),
            pl.BlockSpec((1, SSM_WIDTH), const2),
            pl.BlockSpec((SSM_WIDTH, SSM_WIDTH), const2),
            pl.BlockSpec((1, SSM_WIDTH), const2),
        ],
        out_specs=io_spec,
        out_shape=jax.ShapeDtypeStruct((bsz, t, SSM_WIDTH), jnp.float32),
        scratch_shapes=[
            pltpu.VMEM((SSM_WIDTH // LANES, rows, LANES), jnp.float32),
            pltpu.VMEM((2, rows, 2 * S5_HALF_STATE), jnp.float32),
            pltpu.VMEM((2, bsz, 2 * S5_HALF_STATE), jnp.float32),
        ],
        compiler_params=pltpu.CompilerParams(
            dimension_semantics=("arbitrary",), vmem_limit_bytes=VMEM_LIMIT),
    )(u, bm, cm, ar, ai, d, wglu, bglu)


def _s5_params(lam_re, lam_im, log_dt, b_re, b_im, c_re, c_im):
    dt = jnp.exp(log_dt)[:, None]
    mag = jnp.exp(lam_re * dt)
    ab_re = mag * jnp.cos(lam_im * dt)
    ab_im = mag * jnp.sin(lam_im * dt)
    nr = ab_re - 1.0
    den = lam_re * lam_re + lam_im * lam_im
    f_re = (nr * lam_re + ab_im * lam_im) / den
    f_im = (ab_im * lam_re - nr * lam_im) / den
    bb_re = f_re[..., None] * b_re - f_im[..., None] * b_im
    bb_im = f_re[..., None] * b_im + f_im[..., None] * b_re
    gh = SSM_GROUPS // 2
    eye = jnp.eye(gh, dtype=jnp.float32)

    def in_half(bb):
        return jnp.einsum('gph,gk->ghkp', bb, eye).reshape(S5_HALF, S5_HALF_STATE)

    def out_half(cc):
        return jnp.einsum('ghp,gk->gpkh', cc, eye).reshape(S5_HALF_STATE, S5_HALF)

    bm = jnp.stack([jnp.concatenate([in_half(bb_re[s]), in_half(bb_im[s])], axis=1)
                    for s in (slice(0, gh), slice(gh, None))])
    cm = jnp.stack([jnp.concatenate([out_half(c_re[s]), -out_half(c_im[s])], axis=0)
                    for s in (slice(0, gh), slice(gh, None))])
    ar = ab_re.reshape(2, 1, S5_HALF_STATE)
    ai = ab_im.reshape(2, 1, S5_HALF_STATE)
    return bm.astype(jnp.bfloat16), cm.astype(jnp.bfloat16), ar, ai


def _out_ffn2_kernel(x1_ref, a_ref, s_ref, ga_ref, gs_ref, wo_ref, g2_ref, wg_ref, wu_ref, wd_ref,
                     gf_ref, o_ref):
    a_t = a_ref[0]
    r_a = lax.rsqrt(jnp.mean(a_t * a_t, axis=0, keepdims=True) + EPS)
    a = ((a_t * r_a).T * ga_ref[...]).astype(jnp.bfloat16)
    s = _rmsnorm(s_ref[...], gs_ref[...]).astype(jnp.bfloat16)
    x2 = x1_ref[...] + _dot(a, wo_ref[:ATTN_WIDTH, :]) + _dot(s, wo_ref[ATTN_WIDTH:, :])
    h = _rmsnorm(x2, g2_ref[...]).astype(jnp.bfloat16)
    x3 = x2 + 0.5 * _swiglu_acc(h, wg_ref, wu_ref, wd_ref)
    o_ref[...] = _rmsnorm(x3, gf_ref[...])


def _out_ffn2(x1, attn_t, ssm, ga, gs, wo, g2, wg, wu, wd, gf, bsz, t):
    n = bsz * t
    tm = min(TOKEN_TILE, t)
    nt = t // tm
    row = lambda i: (i, 0)
    return pl.pallas_call(
        _out_ffn2_kernel,
        grid=(n // tm,),
        in_specs=[
            pl.BlockSpec((tm, D_MODEL), row),
            pl.BlockSpec((1, ATTN_WIDTH, tm), lambda i: (i // nt, 0, i % nt)),
            pl.BlockSpec((tm, SSM_WIDTH), row),
            _resident((1, ATTN_WIDTH)), _resident((1, SSM_WIDTH)),
            _resident((D_MODEL, D_MODEL)),
            _resident((1, D_MODEL)),
            _resident((D_MODEL, D_FF)), _resident((D_MODEL, D_FF)), _resident((D_FF, D_MODEL)),
            _resident((1, D_MODEL)),
        ],
        out_specs=pl.BlockSpec((tm, D_MODEL), row),
        out_shape=jax.ShapeDtypeStruct((n, D_MODEL), jnp.float32),
        compiler_params=pltpu.CompilerParams(
            dimension_semantics=("parallel",), vmem_limit_bytes=VMEM_LIMIT),
    )(x1, attn_t, ssm, ga, gs, wo, g2, wg, wu, wd, gf)


def _reorder_w_in(w_in, gate_bias):
    per_g = N_GATE_COLS // N_KV
    g_lo = GATE_OFF
    pieces = [w_in[:, :g_lo]]
    bias = []
    gb = gate_bias.reshape(1, N_GATE_COLS)
    for g in range(N_KV):
        cols = w_in[:, g_lo + g * per_g:g_lo + (g + 1) * per_g]
        pieces.append(jnp.pad(cols, ((0, 0), (0, LANES - per_g))))
        bias.append(jnp.pad(gb[:, g * per_g:(g + 1) * per_g], ((0, 0), (0, LANES - per_g))))
    pieces.append(w_in[:, g_lo + N_GATE_COLS:])
    return jnp.concatenate(pieces, axis=1).astype(jnp.bfloat16), jnp.concatenate(bias, axis=1)


def kernel(x, ffn1_norm, ffn1_w_gate, ffn1_w_up, ffn1_w_down, mix_norm, w_in, gate_bias, cmp_pos_k, cmp_pos_v, cmp_k_w1, cmp_k_b1, cmp_k_w2, cmp_v_w1, cmp_v_b1, cmp_v_w2, s5_lambda_re, s5_lambda_im, s5_log_dt, s5_b_re, s5_b_im, s5_c_re, s5_c_im, s5_d, s5_w_glu, s5_b_glu, attn_out_norm, ssm_out_norm, w_out, ffn2_norm, ffn2_w_gate, ffn2_w_up, ffn2_w_down, final_norm):
    bsz, t, _ = x.shape
    n = bsz * t
    bf = jnp.bfloat16
    assert ffn1_norm.shape[0] == 1, "the final rmsnorm is fused into the single layer's last kernel"
    assert t % max(TOKEN_TILE, WINDOW) == 0 and t >= WINDOW + Q_BLOCK and x.shape[2] == D_MODEL
    xc = x.reshape(n, D_MODEL)
    for l in range(1):
        win, gb = _reorder_w_in(w_in[l], gate_bias[l])
        x1, q, kvc, ks, vst, kw, vwt, gates, u = _ffn1_proj(
            xc, ffn1_norm[l][None], ffn1_w_gate[l].astype(bf), ffn1_w_up[l].astype(bf),
            ffn1_w_down[l].astype(bf), mix_norm[l][None], win, gb, bsz, t)

        kvc4 = kvc.reshape(2 * N_KV, bsz, t // CMP_STRIDE, CMP_STRIDE * HEAD_DIM)
        pos = jnp.stack([cmp_pos_k[l], cmp_pos_v[l]]).reshape(2, 1, CMP_LEN * HEAD_DIM)
        w1 = jnp.stack([cmp_k_w1[l], cmp_v_w1[l]]).astype(bf)
        b1 = jnp.stack([cmp_k_b1[l], cmp_v_b1[l]])[:, None, :]
        w2 = jnp.stack([cmp_k_w2[l], cmp_v_w2[l]]).astype(bf)
        kcmp, kcmp_t = _compress(kvc4, pos, w1, b1, w2)

        attn_t = _nsa(q, kcmp, kcmp_t, ks, vst, kw, vwt, gates, bsz, t)

        bm, cm, ar, ai = _s5_params(s5_lambda_re[l], s5_lambda_im[l], s5_log_dt[l], s5_b_re[l],
                                    s5_b_im[l], s5_c_re[l], s5_c_im[l])
        ssm = _s5(u.reshape(bsz, t, SSM_WIDTH), bm, cm, ar, ai,
                  s5_d[l].reshape(1, SSM_WIDTH), s5_w_glu[l].astype(bf), s5_b_glu[l][None])

        xc = _out_ffn2(x1, attn_t, ssm.reshape(n, SSM_WIDTH), attn_out_norm[l][None],
                       ssm_out_norm[l][None], w_out[l].astype(bf), ffn2_norm[l][None],
                       ffn2_w_gate[l].astype(bf), ffn2_w_up[l].astype(bf), ffn2_w_down[l].astype(bf),
                       final_norm[None], bsz, t)
    return xc.reshape(bsz, t, D_MODEL)
```

```python
import functools
import math

import jax
import jax.numpy as jnp
from jax import lax
from jax.experimental import pallas as pl
from jax.experimental.pallas import tpu as pltpu

D_MODEL = 1024
HEAD_DIM = 64
N_HEADS = 8
N_KV = 2
GQA_R = N_HEADS // N_KV
ATTN_WIDTH = N_HEADS * HEAD_DIM
CMP_LEN = 32
CMP_STRIDE = 16
CMP_HIDDEN = 256
SEL_BLOCK = 64
SEL_SHIFT = 6
SEL_TOPK = 16
WINDOW = 512
Q_BLOCK = 256
SSM_WIDTH = 512
SSM_GROUP = 16
SSM_GROUPS = SSM_WIDTH // SSM_GROUP
SSM_STATE = 64
D_FF = 2816
EPS = 1e-6

LANES = 128
SUBLANES = 8
FF_CHUNK = 256
TOKEN_TILE = 512
SEL_KEY_TILE = 256
S5_TIME_CHUNK = 64
PV_ROWS = HEAD_DIM + 16
TOPK_CHUNK = 16
S5_HALF = SSM_WIDTH // 2
S5_HALF_STATE = (SSM_GROUPS // 2) * SSM_STATE
VMEM_LIMIT = 56 * 1024 * 1024
NEG_BIG = -1e30
Q_SCALE = HEAD_DIM ** -0.5 * math.log2(math.e)

Q_OFF = 0
KVC_OFF = 512
KVS_OFF = 768
KVW_OFF = 1024
GATE_OFF = 1280
U_OFF = 1536
IN_COLS_PADDED = 2048
N_GATE_COLS = N_HEADS * 3


def _dot(a, b):
    return jnp.dot(a, b, preferred_element_type=jnp.float32)


def _dot_nt(a, b):
    return lax.dot_general(a, b, (((1,), (1,)), ((), ())), preferred_element_type=jnp.float32)


def _rmsnorm(x, g):
    r = lax.rsqrt(jnp.mean(x * x, axis=-1, keepdims=True) + EPS)
    return (x * r) * g


def _swiglu_acc(h, wg_ref, wu_ref, wd_ref):
    acc = jnp.zeros((h.shape[0], D_MODEL), jnp.float32)
    for f in range(0, D_FF, FF_CHUNK):
        gate = _dot(h, wg_ref[:, f:f + FF_CHUNK])
        up = _dot(h, wu_ref[:, f:f + FF_CHUNK])
        act = (gate * jax.nn.sigmoid(gate) * up).astype(jnp.bfloat16)
        acc = acc + _dot(act, wd_ref[f:f + FF_CHUNK, :])
    return acc


def _resident(shape):
    nd = len(shape)
    return pl.BlockSpec(shape, lambda *_: (0,) * nd, pipeline_mode=pl.Buffered(1))


def _ffn1_proj_kernel(x_ref, g1_ref, wg_ref, wu_ref, wd_ref, gm_ref, win_ref, gb_ref,
                      x1_ref, q_ref, kvc_ref, ks_ref, vst_ref, kw_ref, vwt_ref, gate_ref, u_ref):
    x = x_ref[...]
    h = _rmsnorm(x, g1_ref[...]).astype(jnp.bfloat16)
    x1 = x + 0.5 * _swiglu_acc(h, wg_ref, wu_ref, wd_ref)
    x1_ref[...] = x1
    h2 = _rmsnorm(x1, gm_ref[...]).astype(jnp.bfloat16)
    p = _dot(h2, win_ref[...])
    for hh in range(N_HEADS):
        lo = Q_OFF + hh * HEAD_DIM
        q_ref[hh] = (p[:, lo:lo + HEAD_DIM] * Q_SCALE).astype(q_ref.dtype)
    for i in range(2 * N_KV):
        kvc_ref[i] = p[:, KVC_OFF + i * HEAD_DIM:KVC_OFF + (i + 1) * HEAD_DIM]
    for g in range(N_KV):
        for off, k_ref, vt_ref in ((KVS_OFF, ks_ref, vst_ref), (KVW_OFF, kw_ref, vwt_ref)):
            k_lo = off + g * HEAD_DIM
            v_lo = off + (N_KV + g) * HEAD_DIM
            k_ref[g] = p[:, k_lo:k_lo + HEAD_DIM].astype(k_ref.dtype)
            vt_ref[g, 0, :HEAD_DIM] = p[:, v_lo:v_lo + HEAD_DIM].T.astype(vt_ref.dtype)
        pad_rows = lax.broadcasted_iota(jnp.int32, (vst_ref.shape[2] - HEAD_DIM, p.shape[0]), 0)
        vst_ref[g, 0, HEAD_DIM:] = jnp.where(pad_rows == 0, 1.0, 0.0).astype(vst_ref.dtype)
    for g in range(N_KV):
        lo = GATE_OFF + g * LANES
        gate = jax.nn.sigmoid(p[:, lo:lo + LANES] + gb_ref[:, g * LANES:(g + 1) * LANES])
        gate_ref[g, 0] = gate.T
    u_ref[...] = p[:, U_OFF:U_OFF + SSM_WIDTH]


def _ffn1_proj(x2d, g1, wg, wu, wd, gm, win, gb, bsz, t):
    n = bsz * t
    tm = min(TOKEN_TILE, t)
    nt = t // tm
    row = lambda i: (i, 0)
    lead = lambda i: (0, i, 0)
    k_spec = pl.BlockSpec((N_KV, tm, HEAD_DIM), lead)
    vt_spec = lambda nrow: pl.BlockSpec((N_KV, 1, nrow, tm), lambda i: (0, i // nt, 0, i % nt))
    k_shape = jax.ShapeDtypeStruct((N_KV, n, HEAD_DIM), jnp.bfloat16)
    vt_shape = lambda nrow: jax.ShapeDtypeStruct((N_KV, bsz, nrow, t), jnp.bfloat16)
    return pl.pallas_call(
        _ffn1_proj_kernel,
        grid=(n // tm,),
        in_specs=[
            pl.BlockSpec((tm, D_MODEL), row),
            _resident((1, D_MODEL)),
            _resident((D_MODEL, D_FF)), _resident((D_MODEL, D_FF)), _resident((D_FF, D_MODEL)),
            _resident((1, D_MODEL)),
            _resident((D_MODEL, IN_COLS_PADDED)),
            _resident((1, N_KV * LANES)),
        ],
        out_specs=[
            pl.BlockSpec((tm, D_MODEL), row),
            pl.BlockSpec((N_HEADS, tm, HEAD_DIM), lead),
            pl.BlockSpec((2 * N_KV, tm, HEAD_DIM), lead),
            k_spec, vt_spec(PV_ROWS), k_spec, vt_spec(HEAD_DIM),
            pl.BlockSpec((N_KV, 1, LANES, tm), lambda i: (0, i // nt, 0, i % nt)),
            pl.BlockSpec((tm, SSM_WIDTH), row),
        ],
        out_shape=[
            jax.ShapeDtypeStruct((n, D_MODEL), jnp.float32),
            jax.ShapeDtypeStruct((N_HEADS, n, HEAD_DIM), jnp.bfloat16),
            jax.ShapeDtypeStruct((2 * N_KV, n, HEAD_DIM), jnp.float32),
            k_shape, vt_shape(PV_ROWS), k_shape, vt_shape(HEAD_DIM),
            jax.ShapeDtypeStruct((N_KV, bsz, LANES, t), jnp.float32),
            jax.ShapeDtypeStruct((n, SSM_WIDTH), jnp.float32),
        ],
        compiler_params=pltpu.CompilerParams(
            dimension_semantics=("parallel",), vmem_limit_bytes=VMEM_LIMIT),
    )(x2d, g1, wg, wu, wd, gm, win, gb)


def _compress_kernel(h_ref, pos_ref, w1_ref, b1_ref, w2_ref, o_ref, ot_ref):
    half = CMP_STRIDE * HEAD_DIM
    hrows = h_ref[0, 0]
    top = (hrows + pos_ref[0, :, :half]).astype(jnp.bfloat16)
    bot = (hrows + pos_ref[0, :, half:]).astype(jnp.bfloat16)
    a = _dot(top, w1_ref[0, :half, :])
    b = _dot(bot, w1_ref[0, half:, :])
    nrow = hrows.shape[0]
    pre = a + pltpu.roll(b, nrow - 1, 0) + b1_ref[0]
    out = _dot(jax.nn.gelu(pre).astype(jnp.bfloat16), w2_ref[0])
    rid = lax.broadcasted_iota(jnp.int32, out.shape, 0)
    out = jnp.where(rid < nrow - 1, out, 0.0)
    o_ref[0, 0] = out
    ot_ref[0, 0] = out.T


def _compress(kvc4, pos, w1, b1, w2):
    _, bsz, nrow, half = kvc4.shape
    kind = lambda i, b: (i // N_KV, 0, 0)
    return pl.pallas_call(
        _compress_kernel,
        grid=(2 * N_KV, bsz),
        in_specs=[
            pl.BlockSpec((1, 1, nrow, half), lambda i, b: (i, b, 0, 0)),
            pl.BlockSpec((1, 1, CMP_LEN * HEAD_DIM), kind),
            pl.BlockSpec((1, CMP_LEN * HEAD_DIM, CMP_HIDDEN), kind),
            pl.BlockSpec((1, 1, CMP_HIDDEN), kind),
            pl.BlockSpec((1, CMP_HIDDEN, HEAD_DIM), kind),
        ],
        out_specs=[pl.BlockSpec((1, 1, nrow, HEAD_DIM), lambda i, b: (i, b, 0, 0)),
                   pl.BlockSpec((1, 1, HEAD_DIM, nrow), lambda i, b: (i, b, 0, 0))],
        out_shape=[jax.ShapeDtypeStruct((2 * N_KV, bsz, nrow, HEAD_DIM), jnp.float32),
                   jax.ShapeDtypeStruct((2 * N_KV, bsz, HEAD_DIM, nrow), jnp.float32)],
        compiler_params=pltpu.CompilerParams(dimension_semantics=("parallel", "parallel")),
    )(kvc4, pos, w1, b1, w2)


def _split3_bf16(x):
    hi = x.astype(jnp.bfloat16)
    r1 = x - hi.astype(jnp.float32)
    mid = r1.astype(jnp.bfloat16)
    lo = (r1 - mid.astype(jnp.float32)).astype(jnp.bfloat16)
    return hi, mid, lo


def _sweep_start(q_ref, k_ref, tile, s_refs, p_refs, acc_ref):
    rows = acc_ref.shape[1]
    p_refs[1][...] = jnp.zeros_like(p_refs[1])
    acc_ref[...] = jnp.zeros_like(acc_ref)
    s_refs[0][...] = _dot_nt(k_ref[0, 0:tile, :], q_ref[...].reshape(rows, HEAD_DIM))


def _sweep_t(q_ref, k_ref, vt_ref, n_tiles, tile, bias_fn, s_refs, p_refs, acc_ref):
    rows = acc_ref.shape[1]
    last_tile = k_ref.shape[1] // tile - 1

    def k_tile(i):
        return k_ref[0, pl.ds(pl.multiple_of(i * tile, tile), tile), :]

    def vt_tile(i):
        return vt_ref[0, 0, :, pl.ds(pl.multiple_of(i * tile, tile), tile)]

    def add_bias(s, i, s_ref):
        bias = bias_fn(i)
        s = jnp.concatenate([s[:, r * Q_BLOCK:(r + 1) * Q_BLOCK] + bias
                             for r in range(rows // Q_BLOCK)], axis=1)
        s_ref[...] = s
        return jnp.max(s, axis=0, keepdims=True)

    def scores_into(i, s_ref):
        return add_bias(_dot_nt(k_tile(i), q_ref[...].reshape(rows, HEAD_DIM)), i, s_ref)

    def phase(i, carry, s_cur, s_nxt, p_cur, p_prv):
        m, s_max = carry
        m_new = jnp.maximum(m, s_max)
        p_cur[...] = jnp.exp2(s_cur[...] - m_new).astype(p_cur.dtype)
        pv = _dot(vt_tile(jnp.maximum(i - 1, 0)), p_prv[...])
        s_max_next = scores_into(jnp.minimum(i + 1, last_tile), s_nxt)
        acc_ref[...] = jnp.exp2(m - m_new) * (acc_ref[...] + pv)
        return m_new, s_max_next

    def body(j, carry):
        carry = phase(2 * j, carry, s_refs[0], s_refs[1], p_refs[0], p_refs[1])
        return phase(2 * j + 1, carry, s_refs[1], s_refs[0], p_refs[1], p_refs[0])

    init = (jnp.full((1, rows), NEG_BIG, jnp.float32), add_bias(s_refs[0][...], 0, s_refs[0]))
    n_pairs = (n_tiles + 1) // 2
    lax.fori_loop(0, n_pairs, body, init)
    acc = acc_ref[...] + _dot(vt_tile(2 * n_pairs - 1), p_refs[1][...])
    return acc[:HEAD_DIM] / acc[HEAD_DIM:HEAD_DIM + 1]


def _nsa_kernel(q_ref, kc_ref, vct_ref, ks_ref, vst_ref, kw_ref, vwt_ref, gate_ref, o_ref,
                bias_ref, rank_ref, causal_ref, s0_ref, s1_ref, p0_ref, p1_ref, acc_ref):
    c = pl.program_id(2)
    rows = GQA_R * Q_BLOCK
    q = q_ref[...].reshape(rows, HEAD_DIM)
    ncp = kc_ref.shape[2]
    t = ks_ref.shape[1]
    nb = t // SEL_BLOCK

    kc = kc_ref[0, 0].astype(jnp.bfloat16)
    vct = vct_ref[0, 0].astype(jnp.bfloat16)
    s_c = _dot_nt(kc, q)
    band = WINDOW + Q_BLOCK
    w0 = pl.multiple_of(jnp.maximum(c * Q_BLOCK - WINDOW, 0), Q_BLOCK)
    s_w = _dot_nt(kw_ref[0, pl.ds(w0, band), :], q)
    _sweep_start(q_ref, ks_ref, SEL_KEY_TILE, (s0_ref, s1_ref), (p0_ref, p1_ref), acc_ref)

    n_c = lax.broadcasted_iota(jnp.int32, (ncp, Q_BLOCK), 0)
    tq_c = c * Q_BLOCK + lax.broadcasted_iota(jnp.int32, (ncp, Q_BLOCK), 1)
    bias_c = jnp.where(n_c * CMP_STRIDE + (CMP_LEN - 1) <= tq_c, 0.0, NEG_BIG)
    s_c = jnp.concatenate([s_c[:, r * Q_BLOCK:(r + 1) * Q_BLOCK] + bias_c for r in range(GQA_R)], axis=1)
    e_c = jnp.exp2(s_c - jnp.max(s_c, axis=0, keepdims=True))
    tq_row = c * Q_BLOCK + (lax.broadcasted_iota(jnp.int32, (1, rows), 1) & (Q_BLOCK - 1))
    inv_c = jnp.where(tq_row >= CMP_LEN - 1, 1.0 / jnp.sum(e_c, axis=0, keepdims=True), 0.0)
    p_c = e_c * inv_c
    o_cmp = _dot(vct, p_c.astype(jnp.bfloat16))

    vwt = vwt_ref[0, 0, :, pl.ds(w0, band)]
    kpos_w = w0 + lax.broadcasted_iota(jnp.int32, (band, Q_BLOCK), 0)
    tq_w = c * Q_BLOCK + lax.broadcasted_iota(jnp.int32, (band, Q_BLOCK), 1)
    bias_w = jnp.where((kpos_w <= tq_w) & (kpos_w > tq_w - WINDOW), 0.0, NEG_BIG)
    s_w = jnp.concatenate([s_w[:, r * Q_BLOCK:(r + 1) * Q_BLOCK] + bias_w for r in range(GQA_R)], axis=1)
    p_w = jnp.exp2(s_w - jnp.max(s_w, axis=0, keepdims=True))
    o_win = _dot(vwt, p_w.astype(jnp.bfloat16)) / jnp.sum(p_w, axis=0, keepdims=True)

    p_sum = p_c[:, 0:Q_BLOCK]
    for r in range(1, GQA_R):
        p_sum = p_sum + p_c[:, r * Q_BLOCK:(r + 1) * Q_BLOCK]
    jb = lax.broadcasted_iota(jnp.int32, (nb, ncp), 0)
    nn = lax.broadcasted_iota(jnp.int32, (nb, ncp), 1)
    overlap_t = ((nn * CMP_STRIDE < jb * SEL_BLOCK + SEL_BLOCK)
                 & (nn * CMP_STRIDE + CMP_LEN > jb * SEL_BLOCK)
                 & (nn < ncp - 1))
    overlap_t = jnp.where(overlap_t, 1.0, 0.0).astype(jnp.bfloat16)
    hi, mid, lo = _split3_bf16(p_sum)
    imp = _dot(overlap_t, hi) + _dot(overlap_t, mid) + _dot(overlap_t, lo)

    j_t = lax.broadcasted_iota(jnp.int32, (nb, Q_BLOCK), 0)
    tq_t = c * Q_BLOCK + lax.broadcasted_iota(jnp.int32, (nb, Q_BLOCK), 1)
    cur = tq_t >> SEL_SHIFT
    forced = (j_t == 0) | (j_t == cur) | (j_t == cur - 1)
    valid = j_t * SEL_BLOCK <= tq_t
    score = jnp.where(forced, jnp.inf, jnp.where(valid, imp, -jnp.inf))
    rank_ref[...] = jnp.zeros_like(rank_ref)
    j_grp = lax.broadcasted_iota(jnp.int32, (SUBLANES, LANES), 0)

    def count_chunk(cand, ranked):
        groups = [slice(v, v + SUBLANES) for v in range(ranked * TOPK_CHUNK, (ranked + 1) * TOPK_CHUNK, SUBLANES)]
        for lt in range(0, Q_BLOCK, LANES):
            sc = score[:, lt:lt + LANES]
            ranks = [rank_ref[grp, lt:lt + LANES] for grp in groups]
            for jp in range(cand * TOPK_CHUNK, (cand + 1) * TOPK_CHUNK):
                row = jnp.broadcast_to(sc[jp:jp + 1, :], (SUBLANES, LANES))
                for v, grp in enumerate(groups):
                    if jp < grp.start:
                        beats = row >= sc[grp]
                    elif jp >= grp.stop:
                        beats = row > sc[grp]
                    else:
                        beats = (row > sc[grp]) | ((row == sc[grp]) & (j_grp > jp - grp.start))
                    ranks[v] = ranks[v] + jnp.where(beats, 1.0, 0.0)
            for v, grp in enumerate(groups):
                rank_ref[grp, lt:lt + LANES] = ranks[v]

    blocks_seen = (c + 1) * (Q_BLOCK // SEL_BLOCK)
    for shell in range(nb // TOPK_CHUNK):
        @pl.when(blocks_seen > shell * TOPK_CHUNK)
        def _():
            for other in range(shell):
                count_chunk(shell, other)
                count_chunk(other, shell)
            count_chunk(shell, shell)

    bias_ref[...] = jnp.where(rank_ref[...] < min(SEL_TOPK, nb), 0.0, NEG_BIG)

    @pl.when(c == 0)
    def _():
        kpos = lax.broadcasted_iota(jnp.int32, (SEL_KEY_TILE, Q_BLOCK), 0)
        qpos = lax.broadcasted_iota(jnp.int32, (SEL_KEY_TILE, Q_BLOCK), 1)
        causal_ref[0] = jnp.zeros((SEL_KEY_TILE, Q_BLOCK), jnp.float32)
        causal_ref[1] = jnp.where(kpos <= qpos, 0.0, NEG_BIG)
        causal_ref[2] = jnp.full((SEL_KEY_TILE, Q_BLOCK), NEG_BIG, jnp.float32)

    blocks_per_tile = SEL_KEY_TILE // SEL_BLOCK

    def sel_bias(kt):
        blocks = [jnp.broadcast_to(bias_ref[pl.ds(kt * blocks_per_tile + jj, 1), :], (SEL_BLOCK, Q_BLOCK))
                  for jj in range(blocks_per_tile)]
        return jnp.concatenate(blocks, axis=0) + causal_ref[jnp.clip(kt - c + 1, 0, 2)]

    gt = gate_ref[0, 0]
    for r in range(GQA_R):
        sl = slice(r * Q_BLOCK, (r + 1) * Q_BLOCK)
        o_ref[0, r * HEAD_DIM:(r + 1) * HEAD_DIM, :] = (
            gt[3 * r:3 * r + 1] * o_cmp[:, sl] + gt[3 * r + 2:3 * r + 3] * o_win[:, sl])

    o_slc = _sweep_t(q_ref, ks_ref, vst_ref, c + 1, SEL_KEY_TILE, sel_bias,
                     (s0_ref, s1_ref), (p0_ref, p1_ref), acc_ref)
    for r in range(GQA_R):
        sl = slice(r * Q_BLOCK, (r + 1) * Q_BLOCK)
        o_ref[0, r * HEAD_DIM:(r + 1) * HEAD_DIM, :] += gt[3 * r + 1:3 * r + 2] * o_slc[:, sl]


def _nsa(q, kcmp, kcmp_t, ks, vst, kw, vwt, gates, bsz, t):
    nq = t // Q_BLOCK
    ncp = kcmp.shape[2]
    rows = GQA_R * Q_BLOCK
    assert t % (2 * SEL_KEY_TILE) == 0, "the selected-block sweep runs pairs of key tiles"
    assert SEL_KEY_TILE == Q_BLOCK, "the causal bias of a key tile is one of three fixed patterns"
    assert (t // SEL_BLOCK) % TOPK_CHUNK == 0
    k_spec = pl.BlockSpec((1, t, HEAD_DIM), lambda b, g, c: (g, b, 0))
    vt_spec = lambda nrow: pl.BlockSpec((1, 1, nrow, t), lambda b, g, c: (g, b, 0, 0))
    return pl.pallas_call(
        _nsa_kernel,
        grid=(bsz, N_KV, nq),
        in_specs=[
            pl.BlockSpec((GQA_R, Q_BLOCK, HEAD_DIM), lambda b, g, c: (g, b * nq + c, 0)),
            pl.BlockSpec((1, 1, ncp, HEAD_DIM), lambda b, g, c: (g, b, 0, 0)),
            pl.BlockSpec((1, 1, HEAD_DIM, ncp), lambda b, g, c: (N_KV + g, b, 0, 0)),
            k_spec, vt_spec(PV_ROWS), k_spec, vt_spec(HEAD_DIM),
            pl.BlockSpec((1, 1, LANES, Q_BLOCK), lambda b, g, c: (g, b, 0, c)),
        ],
        out_specs=pl.BlockSpec((1, GQA_R * HEAD_DIM, Q_BLOCK), lambda b, g, c: (b, g, c)),
        out_shape=jax.ShapeDtypeStruct((bsz, ATTN_WIDTH, t), jnp.float32),
        scratch_shapes=[pltpu.VMEM((t // SEL_BLOCK, Q_BLOCK), jnp.float32),
                        pltpu.VMEM((t // SEL_BLOCK, Q_BLOCK), jnp.float32),
                        pltpu.VMEM((3, SEL_KEY_TILE, Q_BLOCK), jnp.float32),
                        pltpu.VMEM((SEL_KEY_TILE, rows), jnp.float32),
                        pltpu.VMEM((SEL_KEY_TILE, rows), jnp.float32),
                        pltpu.VMEM((SEL_KEY_TILE, rows), jnp.bfloat16),
                        pltpu.VMEM((SEL_KEY_TILE, rows), jnp.bfloat16),
                        pltpu.VMEM((PV_ROWS, rows), jnp.float32)],
        compiler_params=pltpu.CompilerParams(
            dimension_semantics=("parallel", "parallel", "arbitrary")),
    )(q, kcmp, kcmp_t, ks, vst, kw, vwt, gates)


def _s5_kernel(u_ref, bm_ref, cm_ref, ar_ref, ai_ref, d_ref, wglu_ref, bglu_ref, o_ref,
               utb_ref, bu_ref, state_ref):
    bsz, steps, _ = u_ref.shape

    @pl.when(pl.program_id(0) == 0)
    def _():
        state_ref[...] = jnp.zeros_like(state_ref)

    n_lane_tiles = utb_ref.shape[0]
    for b in range(bsz):
        for j in range(n_lane_tiles):
            utb_ref[j, pl.ds(b, steps, stride=bsz), :] = u_ref[b, :, j * LANES:(j + 1) * LANES]
    u = jnp.concatenate([utb_ref[j] for j in range(n_lane_tiles)], axis=1)
    ub = u.astype(jnp.bfloat16)
    for half in range(2):
        bu_ref[half] = _dot(ub[:, half * S5_HALF:(half + 1) * S5_HALF], bm_ref[half])
    ys = []
    re = slice(0, S5_HALF_STATE)
    im = slice(S5_HALF_STATE, 2 * S5_HALF_STATE)
    for half in range(2):
        a_re = jnp.broadcast_to(ar_ref[half], (bsz, S5_HALF_STATE))
        a_im = jnp.broadcast_to(ai_ref[half], (bsz, S5_HALF_STATE))
        xr = state_ref[half, :, re]
        xi = state_ref[half, :, im]
        for i in range(steps):
            rows = slice(i * bsz, (i + 1) * bsz)
            xr, xi = (a_re * xr - a_im * xi + bu_ref[half, rows, re],
                      a_re * xi + a_im * xr + bu_ref[half, rows, im])
            bu_ref[half, rows, re] = xr
            bu_ref[half, rows, im] = xi
        state_ref[half, :, re] = xr
        state_ref[half, :, im] = xi
        ys.append(_dot(bu_ref[half].astype(jnp.bfloat16), cm_ref[half]))
    y = jnp.concatenate(ys, axis=-1) + d_ref[...] * u
    y = jax.nn.gelu(y)
    out = y * jax.nn.sigmoid(_dot(y.astype(jnp.bfloat16), wglu_ref[...]) + bglu_ref[...])
    for j in range(n_lane_tiles):
        utb_ref[j] = out[:, j * LANES:(j + 1) * LANES]
    for b in range(bsz):
        for j in range(n_lane_tiles):
            o_ref[b, :, j * LANES:(j + 1) * LANES] = utb_ref[j, pl.ds(b, steps, stride=bsz), :]


def _s5(u, bm, cm, ar, ai, d, wglu, bglu):
    bsz, t, _ = u.shape
    rows = S5_TIME_CHUNK * bsz
    const3 = lambda i: (0, 0, 0)
    const2 = lambda i: (0, 0)
    io_spec = pl.BlockSpec((bsz, S5_TIME_CHUNK, SSM_WIDTH), lambda i: (0, i, 0))
    return pl.pallas_call(
        _s5_kernel,
        grid=(t // S5_TIME_CHUNK,),
        in_specs=[
            io_spec,
            pl.BlockSpec((2, S5_HALF, 2 * S5_HALF_STATE), const3),
            pl.BlockSpec((2, 2 * S5_HALF_STATE, S5_HALF), const3),
            pl.BlockSpec((2, 1, S5_HALF_STATE), const3),
            pl.BlockSpec((2, 1, S5_HALF_STATE), const3),
            pl.BlockSpec((1, SSM_WIDTH), const2),
            pl.BlockSpec((SSM_WIDTH, SSM_WIDTH), const2),
            pl.BlockSpec((1, SSM_WIDTH), const2),
        ],
        out_specs=io_spec,
        out_shape=jax.ShapeDtypeStruct((bsz, t, SSM_WIDTH), jnp.float32),
        scratch_shapes=[
            pltpu.VMEM((SSM_WIDTH // LANES, rows, LANES), jnp.float32),
            pltpu.VMEM((2, rows, 2 * S5_HALF_STATE), jnp.float32),
            pltpu.VMEM((2, bsz, 2 * S5_HALF_STATE), jnp.float32),
        ],
        compiler_params=pltpu.CompilerParams(
            dimension_semantics=("arbitrary",), vmem_limit_bytes=VMEM_LIMIT),
    )(u, bm, cm, ar, ai, d, wglu, bglu)


def _s5_params(lam_re, lam_im, log_dt, b_re, b_im, c_re, c_im):
    dt = jnp.exp(log_dt)[:, None]
    mag = jnp.exp(lam_re * dt)
    ab_re = mag * jnp.cos(lam_im * dt)
    ab_im = mag * jnp.sin(lam_im * dt)
    nr = ab_re - 1.0
    den = lam_re * lam_re + lam_im * lam_im
    f_re = (nr * lam_re + ab_im * lam_im) / den
    f_im = (ab_im * lam_re - nr * lam_im) / den
    bb_re = f_re[..., None] * b_re - f_im[..., None] * b_im
    bb_im = f_re[..., None] * b_im + f_im[..., None] * b_re
    gh = SSM_GROUPS // 2
    eye = jnp.eye(gh, dtype=jnp.float32)

    def in_half(bb):
        return jnp.einsum('gph,gk->ghkp', bb, eye).reshape(S5_HALF, S5_HALF_STATE)

    def out_half(cc):
        return jnp.einsum('ghp,gk->gpkh', cc, eye).reshape(S5_HALF_STATE, S5_HALF)

    bm = jnp.stack([jnp.concatenate([in_half(bb_re[s]), in_half(bb_im[s])], axis=1)
                    for s in (slice(0, gh), slice(gh, None))])
    cm = jnp.stack([jnp.concatenate([out_half(c_re[s]), -out_half(c_im[s])], axis=0)
                    for s in (slice(0, gh), slice(gh, None))])
    ar = ab_re.reshape(2, 1, S5_HALF_STATE)
    ai = ab_im.reshape(2, 1, S5_HALF_STATE)
    return bm.astype(jnp.bfloat16), cm.astype(jnp.bfloat16), ar, ai


def _out_ffn2_kernel(x1_ref, a_ref, s_ref, ga_ref, gs_ref, wo_ref, g2_ref, wg_ref, wu_ref, wd_ref,
                     gf_ref, o_ref):
    a_t = a_ref[0]
    r_a = lax.rsqrt(jnp.mean(a_t * a_t, axis=0, keepdims=True) + EPS)
    a = ((a_t * r_a).T * ga_ref[...]).astype(jnp.bfloat16)
    s = _rmsnorm(s_ref[...], gs_ref[...]).astype(jnp.bfloat16)
    x2 = x1_ref[...] + _dot(a, wo_ref[:ATTN_WIDTH, :]) + _dot(s, wo_ref[ATTN_WIDTH:, :])
    h = _rmsnorm(x2, g2_ref[...]).astype(jnp.bfloat16)
    x3 = x2 + 0.5 * _swiglu_acc(h, wg_ref, wu_ref, wd_ref)
    o_ref[...] = _rmsnorm(x3, gf_ref[...])


def _out_ffn2(x1, attn_t, ssm, ga, gs, wo, g2, wg, wu, wd, gf, bsz, t):
    n = bsz * t
    tm = min(TOKEN_TILE, t)
    nt = t // tm
    row = lambda i: (i, 0)
    return pl.pallas_call(
        _out_ffn2_kernel,
        grid=(n // tm,),
        in_specs=[
            pl.BlockSpec((tm, D_MODEL), row),
            pl.BlockSpec((1, ATTN_WIDTH, tm), lambda i: (i // nt, 0, i % nt)),
            pl.BlockSpec((tm, SSM_WIDTH), row),
            _resident((1, ATTN_WIDTH)), _resident((1, SSM_WIDTH)),
            _resident((D_MODEL, D_MODEL)),
            _resident((1, D_MODEL)),
            _resident((D_MODEL, D_FF)), _resident((D_MODEL, D_FF)), _resident((D_FF, D_MODEL)),
            _resident((1, D_MODEL)),
        ],
        out_specs=pl.BlockSpec((tm, D_MODEL), row),
        out_shape=jax.ShapeDtypeStruct((n, D_MODEL), jnp.float32),
        compiler_params=pltpu.CompilerParams(
            dimension_semantics=("parallel",), vmem_limit_bytes=VMEM_LIMIT),
    )(x1, attn_t, ssm, ga, gs, wo, g2, wg, wu, wd, gf)


def _reorder_w_in(w_in, gate_bias):
    per_g = N_GATE_COLS // N_KV
    g_lo = GATE_OFF
    pieces = [w_in[:, :g_lo]]
    bias = []
    gb = gate_bias.reshape(1, N_GATE_COLS)
    for g in range(N_KV):
        cols = w_in[:, g_lo + g * per_g:g_lo + (g + 1) * per_g]
        pieces.append(jnp.pad(cols, ((0, 0), (0, LANES - per_g))))
        bias.append(jnp.pad(gb[:, g * per_g:(g + 1) * per_g], ((0, 0), (0, LANES - per_g))))
    pieces.append(w_in[:, g_lo + N_GATE_COLS:])
    return jnp.concatenate(pieces, axis=1).astype(jnp.bfloat16), jnp.concatenate(bias, axis=1)


def kernel(x, ffn1_norm, ffn1_w_gate, ffn1_w_up, ffn1_w_down, mix_norm, w_in, gate_bias, cmp_pos_k, cmp_pos_v, cmp_k_w1, cmp_k_b1, cmp_k_w2, cmp_v_w1, cmp_v_b1, cmp_v_w2, s5_lambda_re, s5_lambda_im, s5_log_dt, s5_b_re, s5_b_im, s5_c_re, s5_c_im, s5_d, s5_w_glu, s5_b_glu, attn_out_norm, ssm_out_norm, w_out, ffn2_norm, ffn2_w_gate, ffn2_w_up, ffn2_w_down, final_norm):
    bsz, t, _ = x.shape
    n = bsz * t
    bf = jnp.bfloat16
    assert ffn1_norm.shape[0] == 1, "the final rmsnorm is fused into the single layer's last kernel"
    assert t % max(TOKEN_TILE, WINDOW) == 0 and t >= WINDOW + Q_BLOCK and x.shape[2] == D_MODEL
    xc = x.reshape(n, D_MODEL)
    for l in range(1):
        win, gb = _reorder_w_in(w_in[l], gate_bias[l])
        x1, q, kvc, ks, vst, kw, vwt, gates, u = _ffn1_proj(
            xc, ffn1_norm[l][None], ffn1_w_gate[l].astype(bf), ffn1_w_up[l].astype(bf),
            ffn1_w_down[l].astype(bf), mix_norm[l][None], win, gb, bsz, t)

        kvc4 = kvc.reshape(2 * N_KV, bsz, t // CMP_STRIDE, CMP_STRIDE * HEAD_DIM)
        pos = jnp.stack([cmp_pos_k[l], cmp_pos_v[l]]).reshape(2, 1, CMP_LEN * HEAD_DIM)
        w1 = jnp.stack([cmp_k_w1[l], cmp_v_w1[l]]).astype(bf)
        b1 = jnp.stack([cmp_k_b1[l], cmp_v_b1[l]])[:, None, :]
        w2 = jnp.stack([cmp_k_w2[l], cmp_v_w2[l]]).astype(bf)
        kcmp, kcmp_t = _compress(kvc4, pos, w1, b1, w2)

        attn_t = _nsa(q, kcmp, kcmp_t, ks, vst, kw, vwt, gates, bsz, t)

        bm, cm, ar, ai = _s5_params(s5_lambda_re[l], s5_lambda_im[l], s5_log_dt[l], s5_b_re[l],
                                    s5_b_im[l], s5_c_re[l], s5_c_im[l])
        ssm = _s5(u.reshape(bsz, t, SSM_WIDTH), bm, cm, ar, ai,
                  s5_d[l].reshape(1, SSM_WIDTH), s5_w_glu[l].astype(bf), s5_b_glu[l][None])

        xc = _out_ffn2(x1, attn_t, ssm.reshape(n, SSM_WIDTH), attn_out_norm[l][None],
                       ssm_out_norm[l][None], w_out[l].astype(bf), ffn2_norm[l][None],
                       ffn2_w_gate[l].astype(bf), ffn2_w_up[l].astype(bf), ffn2_w_down[l].astype(bf),
                       final_norm[None], bsz, t)
    return xc.reshape(bsz, t, D_MODEL)
```

```python
import functools
import math

import jax
import jax.numpy as jnp
from jax import lax
from jax.experimental import pallas as pl
from jax.experimental.pallas import tpu as pltpu

D_MODEL = 1024
HEAD_DIM = 64
N_HEADS = 8
N_KV = 2
GQA_R = N_HEADS // N_KV
ATTN_WIDTH = N_HEADS * HEAD_DIM
CMP_LEN = 32
CMP_STRIDE = 16
CMP_HIDDEN = 256
SEL_BLOCK = 64
SEL_SHIFT = 6
SEL_TOPK = 16
WINDOW = 512
Q_BLOCK = 256
SSM_WIDTH = 512
SSM_GROUP = 16
SSM_GROUPS = SSM_WIDTH // SSM_GROUP
SSM_STATE = 64
D_FF = 2816
EPS = 1e-6

LANES = 128
SUBLANES = 8
FF_CHUNK = 256
TOKEN_TILE = 512
SEL_KEY_TILE = 256
S5_TIME_CHUNK = 64
S5_LANE_CHUNK = 512
PV_ROWS = HEAD_DIM + 16
TOPK_CHUNK = 16
S5_HALF = SSM_WIDTH // 2
S5_HALF_STATE = (SSM_GROUPS // 2) * SSM_STATE
VMEM_LIMIT = 56 * 1024 * 1024
NEG_BIG = -1e30
Q_SCALE = HEAD_DIM ** -0.5 * math.log2(math.e)

Q_OFF = 0
KVC_OFF = 512
KVS_OFF = 768
KVW_OFF = 1024
GATE_OFF = 1280
U_OFF = 1536
IN_COLS_PADDED = 2048
N_GATE_COLS = N_HEADS * 3


def _dot(a, b):
    return jnp.dot(a, b, preferred_element_type=jnp.float32)


def _dot_nt(a, b):
    return lax.dot_general(a, b, (((1,), (1,)), ((), ())), preferred_element_type=jnp.float32)


def _rmsnorm(x, g):
    r = lax.rsqrt(jnp.mean(x * x, axis=-1, keepdims=True) + EPS)
    return (x * r) * g


def _swiglu_acc(h, wg_ref, wu_ref, wd_ref):
    acc = jnp.zeros((h.shape[0], D_MODEL), jnp.float32)
    for f in range(0, D_FF, FF_CHUNK):
        gate = _dot(h, wg_ref[:, f:f + FF_CHUNK])
        up = _dot(h, wu_ref[:, f:f + FF_CHUNK])
        act = (gate * jax.nn.sigmoid(gate) * up).astype(jnp.bfloat16)
        acc = acc + _dot(act, wd_ref[f:f + FF_CHUNK, :])
    return acc


def _ones_row_pad(nrow, ncol, dtype):
    return jnp.where(lax.broadcasted_iota(jnp.int32, (nrow, ncol), 0) == 0, 1.0, 0.0).astype(dtype)


def _resident(shape):
    nd = len(shape)
    return pl.BlockSpec(shape, lambda *_: (0,) * nd, pipeline_mode=pl.Buffered(1))


def _ffn1_proj_kernel(x_ref, g1_ref, wg_ref, wu_ref, wd_ref, gm_ref, win_ref, gb_ref,
                      x1_ref, q_ref, kvc_ref, ks_ref, vst_ref, kw_ref, vwt_ref, gate_ref, u_ref):
    x = x_ref[...]
    h = _rmsnorm(x, g1_ref[...]).astype(jnp.bfloat16)
    x1 = x + 0.5 * _swiglu_acc(h, wg_ref, wu_ref, wd_ref)
    x1_ref[...] = x1
    h2 = _rmsnorm(x1, gm_ref[...]).astype(jnp.bfloat16)
    p = _dot(h2, win_ref[...])
    for hh in range(N_HEADS):
        lo = Q_OFF + hh * HEAD_DIM
        q_ref[hh] = (p[:, lo:lo + HEAD_DIM] * Q_SCALE).astype(q_ref.dtype)
    kv_cols = N_KV * HEAD_DIM
    for kind in range(2):
        kvc_ref[kind] = p[:, KVC_OFF + kind * kv_cols:KVC_OFF + (kind + 1) * kv_cols]
    for g in range(N_KV):
        for off, k_ref, vt_ref in ((KVS_OFF, ks_ref, vst_ref), (KVW_OFF, kw_ref, vwt_ref)):
            k_lo = off + g * HEAD_DIM
            v_lo = off + (N_KV + g) * HEAD_DIM
            k_ref[g] = p[:, k_lo:k_lo + HEAD_DIM].astype(k_ref.dtype)
            vt_ref[g, 0, :HEAD_DIM] = p[:, v_lo:v_lo + HEAD_DIM].T.astype(vt_ref.dtype)
            vt_ref[g, 0, HEAD_DIM:] = _ones_row_pad(PV_ROWS - HEAD_DIM, p.shape[0], vt_ref.dtype)
    for g in range(N_KV):
        lo = GATE_OFF + g * LANES
        gate = jax.nn.sigmoid(p[:, lo:lo + LANES] + gb_ref[:, g * LANES:(g + 1) * LANES])
        gate_ref[g, 0] = gate.T
    u_ref[...] = p[:, U_OFF:U_OFF + SSM_WIDTH]


def _ffn1_proj(x2d, g1, wg, wu, wd, gm, win, gb, bsz, t):
    n = bsz * t
    tm = min(TOKEN_TILE, t)
    nt = t // tm
    row = lambda i: (i, 0)
    lead = lambda i: (0, i, 0)
    k_spec = pl.BlockSpec((N_KV, tm, HEAD_DIM), lead)
    vt_spec = lambda nrow: pl.BlockSpec((N_KV, 1, nrow, tm), lambda i: (0, i // nt, 0, i % nt))
    k_shape = jax.ShapeDtypeStruct((N_KV, n, HEAD_DIM), jnp.bfloat16)
    vt_shape = lambda nrow: jax.ShapeDtypeStruct((N_KV, bsz, nrow, t), jnp.bfloat16)
    return pl.pallas_call(
        _ffn1_proj_kernel,
        grid=(n // tm,),
        in_specs=[
            pl.BlockSpec((tm, D_MODEL), row),
            _resident((1, D_MODEL)),
            _resident((D_MODEL, D_FF)), _resident((D_MODEL, D_FF)), _resident((D_FF, D_MODEL)),
            _resident((1, D_MODEL)),
            _resident((D_MODEL, IN_COLS_PADDED)),
            _resident((1, N_KV * LANES)),
        ],
        out_specs=[
            pl.BlockSpec((tm, D_MODEL), row),
            pl.BlockSpec((N_HEADS, tm, HEAD_DIM), lead),
            pl.BlockSpec((2, tm, N_KV * HEAD_DIM), lead),
            k_spec, vt_spec(PV_ROWS), k_spec, vt_spec(PV_ROWS),
            pl.BlockSpec((N_KV, 1, LANES, tm), lambda i: (0, i // nt, 0, i % nt)),
            pl.BlockSpec((tm, SSM_WIDTH), row),
        ],
        out_shape=[
            jax.ShapeDtypeStruct((n, D_MODEL), jnp.float32),
            jax.ShapeDtypeStruct((N_HEADS, n, HEAD_DIM), jnp.bfloat16),
            jax.ShapeDtypeStruct((2, n, N_KV * HEAD_DIM), jnp.float32),
            k_shape, vt_shape(PV_ROWS), k_shape, vt_shape(PV_ROWS),
            jax.ShapeDtypeStruct((N_KV, bsz, LANES, t), jnp.float32),
            jax.ShapeDtypeStruct((n, SSM_WIDTH), jnp.float32),
        ],
        compiler_params=pltpu.CompilerParams(
            dimension_semantics=("parallel",), vmem_limit_bytes=VMEM_LIMIT),
    )(x2d, g1, wg, wu, wd, gm, win, gb)


def _compress_kernel(x_ref, pos_ref, w1_ref, b1_ref, w2_ref, o_ref, ot_ref):
    nrow = x_ref.shape[1] // CMP_STRIDE
    tops = [jnp.zeros((nrow, CMP_HIDDEN), jnp.float32) for _ in range(N_KV)]
    bots = [jnp.zeros((nrow, CMP_HIDDEN), jnp.float32) for _ in range(N_KV)]
    for l in range(CMP_STRIDE):
        xl = x_ref[0, pl.ds(l, nrow, stride=CMP_STRIDE), :]
        w_top = w1_ref[0, l * HEAD_DIM:(l + 1) * HEAD_DIM, :]
        w_bot = w1_ref[0, (CMP_STRIDE + l) * HEAD_DIM:(CMP_STRIDE + l + 1) * HEAD_DIM, :]
        for g in range(N_KV):
            xg = xl[:, g * HEAD_DIM:(g + 1) * HEAD_DIM]
            tops[g] = tops[g] + _dot((xg + pos_ref[0, l:l + 1, :]).astype(jnp.bfloat16), w_top)
            bots[g] = bots[g] + _dot((xg + pos_ref[0, CMP_STRIDE + l:CMP_STRIDE + l + 1, :])
                                     .astype(jnp.bfloat16), w_bot)
    for g in range(N_KV):
        pre = tops[g] + pltpu.roll(bots[g], nrow - 1, 0) + b1_ref[0]
        out = _dot(jax.nn.gelu(pre).astype(jnp.bfloat16), w2_ref[0])
        rid = lax.broadcasted_iota(jnp.int32, out.shape, 0)
        out = jnp.where(rid < nrow - 1, out, 0.0)
        o_ref[g, 0] = out
        ot_ref[g, 0, :HEAD_DIM] = out.T
        ot_ref[g, 0, HEAD_DIM:] = _ones_row_pad(PV_ROWS - HEAD_DIM, nrow, ot_ref.dtype)


def _compress(kvc, pos, w1, b1, w2, bsz):
    t = kvc.shape[1] // bsz
    nrow = t // CMP_STRIDE
    kind = lambda k, b: (k, 0, 0)
    return pl.pallas_call(
        _compress_kernel,
        grid=(2, bsz),
        in_specs=[
            pl.BlockSpec((1, t, N_KV * HEAD_DIM), lambda k, b: (k, b, 0)),
            pl.BlockSpec((1, CMP_LEN, HEAD_DIM), kind),
            pl.BlockSpec((1, CMP_LEN * HEAD_DIM, CMP_HIDDEN), kind),
            pl.BlockSpec((1, 1, CMP_HIDDEN), kind),
            pl.BlockSpec((1, CMP_HIDDEN, HEAD_DIM), kind),
        ],
        out_specs=[pl.BlockSpec((N_KV, 1, nrow, HEAD_DIM), lambda k, b: (k, b, 0, 0)),
                   pl.BlockSpec((N_KV, 1, PV_ROWS, nrow), lambda k, b: (k, b, 0, 0))],
        out_shape=[jax.ShapeDtypeStruct((2 * N_KV, bsz, nrow, HEAD_DIM), jnp.float32),
                   jax.ShapeDtypeStruct((2 * N_KV, bsz, PV_ROWS, nrow), jnp.float32)],
        compiler_params=pltpu.CompilerParams(dimension_semantics=("parallel", "parallel")),
    )(kvc, pos, w1, b1, w2)


def _split3_bf16(x):
    hi = x.astype(jnp.bfloat16)
    r1 = x - hi.astype(jnp.float32)
    mid = r1.astype(jnp.bfloat16)
    lo = (r1 - mid.astype(jnp.float32)).astype(jnp.bfloat16)
    return hi, mid, lo


def _sweep_start(q_ref, k_ref, tile, s_refs, p_refs, acc_ref):
    rows = acc_ref.shape[1]
    p_refs[1][...] = jnp.zeros_like(p_refs[1])
    acc_ref[...] = jnp.zeros_like(acc_ref)
    s_refs[0][...] = _dot_nt(k_ref[0, 0:tile, :], q_ref[...].reshape(rows, HEAD_DIM))


def _sweep_t(q_ref, k_ref, vt_ref, n_tiles, tile, bias_fn, s_refs, p_refs, acc_ref):
    rows = acc_ref.shape[1]
    last_tile = k_ref.shape[1] // tile - 1

    def k_tile(i):
        return k_ref[0, pl.ds(pl.multiple_of(i * tile, tile), tile), :]

    def vt_tile(i):
        return vt_ref[0, 0, :, pl.ds(pl.multiple_of(i * tile, tile), tile)]

    def add_bias(s, i, s_ref):
        bias = bias_fn(i)
        s = jnp.concatenate([s[:, r * Q_BLOCK:(r + 1) * Q_BLOCK] + bias
                             for r in range(rows // Q_BLOCK)], axis=1)
        s_ref[...] = s
        return jnp.max(s, axis=0, keepdims=True)

    def scores_into(i, s_ref):
        return add_bias(_dot_nt(k_tile(i), q_ref[...].reshape(rows, HEAD_DIM)), i, s_ref)

    def phase(i, carry, s_cur, s_nxt, p_cur, p_prv):
        m, s_max = carry
        m_new = jnp.maximum(m, s_max)
        p_cur[...] = jnp.exp2(s_cur[...] - m_new).astype(p_cur.dtype)
        pv = _dot(vt_tile(jnp.maximum(i - 1, 0)), p_prv[...])
        s_max_next = scores_into(jnp.minimum(i + 1, last_tile), s_nxt)
        acc_ref[...] = jnp.exp2(m - m_new) * (acc_ref[...] + pv)
        return m_new, s_max_next

    def body(j, carry):
        carry = phase(2 * j, carry, s_refs[0], s_refs[1], p_refs[0], p_refs[1])
        return phase(2 * j + 1, carry, s_refs[1], s_refs[0], p_refs[1], p_refs[0])

    init = (jnp.full((1, rows), NEG_BIG, jnp.float32), add_bias(s_refs[0][...], 0, s_refs[0]))
    n_pairs = (n_tiles + 1) // 2
    lax.fori_loop(0, n_pairs, body, init)
    acc = acc_ref[...] + _dot(vt_tile(2 * n_pairs - 1), p_refs[1][...])
    return acc[:HEAD_DIM] / acc[HEAD_DIM:HEAD_DIM + 1]


def _nsa_kernel(q_ref, kc_ref, vct_ref, ks_ref, vst_ref, kw_ref, vwt_ref, gate_ref, o_ref,
                bias_ref, rank_ref, causal_ref, s0_ref, s1_ref, p0_ref, p1_ref, acc_ref):
    c = pl.program_id(2)
    rows = GQA_R * Q_BLOCK
    q = q_ref[...].reshape(rows, HEAD_DIM)
    ncp = kc_ref.shape[2]
    t = ks_ref.shape[1]
    nb = t // SEL_BLOCK

    kc = kc_ref[0, 0].astype(jnp.bfloat16)
    vct = vct_ref[0, 0].astype(jnp.bfloat16)
    s_c = _dot_nt(kc, q)
    band = WINDOW + Q_BLOCK
    w0 = pl.multiple_of(jnp.maximum(c * Q_BLOCK - WINDOW, 0), Q_BLOCK)
    s_w = _dot_nt(kw_ref[0, pl.ds(w0, band), :], q)
    _sweep_start(q_ref, ks_ref, SEL_KEY_TILE, (s0_ref, s1_ref), (p0_ref, p1_ref), acc_ref)

    n_c = lax.broadcasted_iota(jnp.int32, (ncp, Q_BLOCK), 0)
    tq_c = c * Q_BLOCK + lax.broadcasted_iota(jnp.int32, (ncp, Q_BLOCK), 1)
    bias_c = jnp.where(n_c * CMP_STRIDE + (CMP_LEN - 1) <= tq_c, 0.0, NEG_BIG)
    s_c = jnp.concatenate([s_c[:, r * Q_BLOCK:(r + 1) * Q_BLOCK] + bias_c for r in range(GQA_R)], axis=1)
    e_c = jnp.exp2(s_c - jnp.max(s_c, axis=0, keepdims=True))
    tq_row = c * Q_BLOCK + (lax.broadcasted_iota(jnp.int32, (1, rows), 1) & (Q_BLOCK - 1))
    ev_c = _dot(vct, e_c.astype(jnp.bfloat16))
    inv_c = jnp.where(tq_row >= CMP_LEN - 1, 1.0 / ev_c[HEAD_DIM:HEAD_DIM + 1], 0.0)
    p_c = e_c * inv_c
    o_cmp = ev_c[:HEAD_DIM] * inv_c

    vwt = vwt_ref[0, 0, :, pl.ds(w0, band)]
    kpos_w = w0 + lax.broadcasted_iota(jnp.int32, (band, Q_BLOCK), 0)
    tq_w = c * Q_BLOCK + lax.broadcasted_iota(jnp.int32, (band, Q_BLOCK), 1)
    bias_w = jnp.where((kpos_w <= tq_w) & (kpos_w > tq_w - WINDOW), 0.0, NEG_BIG)
    s_w = jnp.concatenate([s_w[:, r * Q_BLOCK:(r + 1) * Q_BLOCK] + bias_w for r in range(GQA_R)], axis=1)
    p_w = jnp.exp2(s_w - jnp.max(s_w, axis=0, keepdims=True))
    pv_w = _dot(vwt, p_w.astype(jnp.bfloat16))
    o_win = pv_w[:HEAD_DIM] / pv_w[HEAD_DIM:HEAD_DIM + 1]

    p_sum = p_c[:, 0:Q_BLOCK]
    for r in range(1, GQA_R):
        p_sum = p_sum + p_c[:, r * Q_BLOCK:(r + 1) * Q_BLOCK]
    jb = lax.broadcasted_iota(jnp.int32, (nb, ncp), 0)
    nn = lax.broadcasted_iota(jnp.int32, (nb, ncp), 1)
    overlap_t = ((nn * CMP_STRIDE < jb * SEL_BLOCK + SEL_BLOCK)
                 & (nn * CMP_STRIDE + CMP_LEN > jb * SEL_BLOCK)
                 & (nn < ncp - 1))
    overlap_t = jnp.where(overlap_t, 1.0, 0.0).astype(jnp.bfloat16)
    hi, mid, lo = _split3_bf16(p_sum)
    imp = _dot(overlap_t, hi) + _dot(overlap_t, mid) + _dot(overlap_t, lo)

    j_t = lax.broadcasted_iota(jnp.int32, (nb, Q_BLOCK), 0)
    tq_t = c * Q_BLOCK + lax.broadcasted_iota(jnp.int32, (nb, Q_BLOCK), 1)
    cur = tq_t >> SEL_SHIFT
    forced = (j_t == 0) | (j_t == cur) | (j_t == cur - 1)
    valid = j_t * SEL_BLOCK <= tq_t
    score = jnp.where(forced, jnp.inf, jnp.where(valid, imp, -jnp.inf))
    rank_ref[...] = jnp.zeros_like(rank_ref)
    j_grp = lax.broadcasted_iota(jnp.int32, (SUBLANES, LANES), 0)

    def count_chunk(cand, ranked):
        groups = [slice(v, v + SUBLANES) for v in range(ranked * TOPK_CHUNK, (ranked + 1) * TOPK_CHUNK, SUBLANES)]
        for lt in range(0, Q_BLOCK, LANES):
            sc = score[:, lt:lt + LANES]
            ranks = [rank_ref[grp, lt:lt + LANES] for grp in groups]
            for jp in range(cand * TOPK_CHUNK, (cand + 1) * TOPK_CHUNK):
                row = jnp.broadcast_to(sc[jp:jp + 1, :], (SUBLANES, LANES))
                for v, grp in enumerate(groups):
                    if jp < grp.start:
                        beats = row >= sc[grp]
                    elif jp >= grp.stop:
                        beats = row > sc[grp]
                    else:
                        beats = (row > sc[grp]) | ((row == sc[grp]) & (j_grp > jp - grp.start))
                    ranks[v] = ranks[v] + jnp.where(beats, 1.0, 0.0)
            for v, grp in enumerate(groups):
                rank_ref[grp, lt:lt + LANES] = ranks[v]

    blocks_seen = (c + 1) * (Q_BLOCK // SEL_BLOCK)
    for shell in range(nb // TOPK_CHUNK):
        @pl.when(blocks_seen > shell * TOPK_CHUNK)
        def _():
            for other in range(shell):
                count_chunk(shell, other)
                count_chunk(other, shell)
            count_chunk(shell, shell)

    bias_ref[...] = jnp.where(rank_ref[...] < min(SEL_TOPK, nb), 0.0, NEG_BIG)

    tiles_per_q = Q_BLOCK // SEL_KEY_TILE

    @pl.when(c == 0)
    def _():
        kpos = lax.broadcasted_iota(jnp.int32, (SEL_KEY_TILE, Q_BLOCK), 0)
        qpos = lax.broadcasted_iota(jnp.int32, (SEL_KEY_TILE, Q_BLOCK), 1)
        causal_ref[0] = jnp.zeros((SEL_KEY_TILE, Q_BLOCK), jnp.float32)
        for j in range(tiles_per_q):
            causal_ref[1 + j] = jnp.where(kpos + j * SEL_KEY_TILE <= qpos, 0.0, NEG_BIG)
        causal_ref[1 + tiles_per_q] = jnp.full((SEL_KEY_TILE, Q_BLOCK), NEG_BIG, jnp.float32)

    blocks_per_tile = SEL_KEY_TILE // SEL_BLOCK

    def sel_bias(kt):
        blocks = [jnp.broadcast_to(bias_ref[pl.ds(kt * blocks_per_tile + jj, 1), :], (SEL_BLOCK, Q_BLOCK))
                  for jj in range(blocks_per_tile)]
        pattern = jnp.clip(kt - c * tiles_per_q + 1, 0, tiles_per_q + 1)
        return jnp.concatenate(blocks, axis=0) + causal_ref[pattern]

    gt = gate_ref[0, 0]
    for r in range(GQA_R):
        sl = slice(r * Q_BLOCK, (r + 1) * Q_BLOCK)
        o_ref[0, r * HEAD_DIM:(r + 1) * HEAD_DIM, :] = (
            gt[3 * r:3 * r + 1] * o_cmp[:, sl] + gt[3 * r + 2:3 * r + 3] * o_win[:, sl])

    o_slc = _sweep_t(q_ref, ks_ref, vst_ref, (c + 1) * tiles_per_q, SEL_KEY_TILE, sel_bias,
                     (s0_ref, s1_ref), (p0_ref, p1_ref), acc_ref)
    for r in range(GQA_R):
        sl = slice(r * Q_BLOCK, (r + 1) * Q_BLOCK)
        o_ref[0, r * HEAD_DIM:(r + 1) * HEAD_DIM, :] += gt[3 * r + 1:3 * r + 2] * o_slc[:, sl]


def _nsa(q, kcmp, kcmp_t, ks, vst, kw, vwt, gates, bsz, t):
    nq = t // Q_BLOCK
    ncp = kcmp.shape[2]
    rows = GQA_R * Q_BLOCK
    assert t % (2 * SEL_KEY_TILE) == 0, "the selected-block sweep runs pairs of key tiles"
    assert Q_BLOCK % SEL_KEY_TILE == 0, "the causal bias of a key tile is one of a few fixed patterns"
    assert (t // SEL_BLOCK) % TOPK_CHUNK == 0
    k_spec = pl.BlockSpec((1, t, HEAD_DIM), lambda b, g, c: (g, b, 0))
    vt_spec = lambda nrow: pl.BlockSpec((1, 1, nrow, t), lambda b, g, c: (g, b, 0, 0))
    return pl.pallas_call(
        _nsa_kernel,
        grid=(bsz, N_KV, nq),
        in_specs=[
            pl.BlockSpec((GQA_R, Q_BLOCK, HEAD_DIM), lambda b, g, c: (g, b * nq + c, 0)),
            pl.BlockSpec((1, 1, ncp, HEAD_DIM), lambda b, g, c: (g, b, 0, 0)),
            pl.BlockSpec((1, 1, PV_ROWS, ncp), lambda b, g, c: (N_KV + g, b, 0, 0)),
            k_spec, vt_spec(PV_ROWS), k_spec, vt_spec(PV_ROWS),
            pl.BlockSpec((1, 1, LANES, Q_BLOCK), lambda b, g, c: (g, b, 0, c)),
        ],
        out_specs=pl.BlockSpec((1, GQA_R * HEAD_DIM, Q_BLOCK), lambda b, g, c: (b, g, c)),
        out_shape=jax.ShapeDtypeStruct((bsz, ATTN_WIDTH, t), jnp.float32),
        scratch_shapes=[pltpu.VMEM((t // SEL_BLOCK, Q_BLOCK), jnp.float32),
                        pltpu.VMEM((t // SEL_BLOCK, Q_BLOCK), jnp.float32),
                        pltpu.VMEM((Q_BLOCK // SEL_KEY_TILE + 2, SEL_KEY_TILE, Q_BLOCK), jnp.float32),
                        pltpu.VMEM((SEL_KEY_TILE, rows), jnp.float32),
                        pltpu.VMEM((SEL_KEY_TILE, rows), jnp.float32),
                        pltpu.VMEM((SEL_KEY_TILE, rows), jnp.bfloat16),
                        pltpu.VMEM((SEL_KEY_TILE, rows), jnp.bfloat16),
                        pltpu.VMEM((PV_ROWS, rows), jnp.float32)],
        compiler_params=pltpu.CompilerParams(
            dimension_semantics=("parallel", "parallel", "arbitrary"),
            vmem_limit_bytes=VMEM_LIMIT),
    )(q, kcmp, kcmp_t, ks, vst, kw, vwt, gates)


def _s5_kernel(u_ref, bm_ref, cm_ref, ar_ref, ai_ref, d_ref, wglu_ref, bglu_ref, o_ref,
               utb_ref, bu_ref, state_ref):
    bsz, steps, _ = u_ref.shape

    @pl.when(pl.program_id(0) == 0)
    def _():
        state_ref[...] = jnp.zeros_like(state_ref)

    n_lane_tiles = utb_ref.shape[0]
    for b in range(bsz):
        for j in range(n_lane_tiles):
            utb_ref[j, pl.ds(b, steps, stride=bsz), :] = u_ref[b, :, j * LANES:(j + 1) * LANES]
    u = jnp.concatenate([utb_ref[j] for j in range(n_lane_tiles)], axis=1)
    ub = u.astype(jnp.bfloat16)
    for half in range(2):
        bu_ref[half] = _dot(ub[:, half * S5_HALF:(half + 1) * S5_HALF], bm_ref[half])
    ys = []
    for half in range(2):
        for lc in range(0, S5_HALF_STATE, S5_LANE_CHUNK):
            re = slice(lc, lc + S5_LANE_CHUNK)
            im = slice(S5_HALF_STATE + lc, S5_HALF_STATE + lc + S5_LANE_CHUNK)
            a_re = jnp.broadcast_to(ar_ref[half, :, re], (bsz, S5_LANE_CHUNK))
            a_im = jnp.broadcast_to(ai_ref[half, :, re], (bsz, S5_LANE_CHUNK))
            xr = state_ref[half, :, re]
            xi = state_ref[half, :, im]
            for i in range(steps):
                rows = slice(i * bsz, (i + 1) * bsz)
                xr, xi = (a_re * xr - a_im * xi + bu_ref[half, rows, re],
                          a_re * xi + a_im * xr + bu_ref[half, rows, im])
                bu_ref[half, rows, re] = xr
                bu_ref[half, rows, im] = xi
            state_ref[half, :, re] = xr
            state_ref[half, :, im] = xi
        ys.append(_dot(bu_ref[half].astype(jnp.bfloat16), cm_ref[half]))
    y = jnp.concatenate(ys, axis=-1) + d_ref[...] * u
    y = jax.nn.gelu(y)
    out = y * jax.nn.sigmoid(_dot(y.astype(jnp.bfloat16), wglu_ref[...]) + bglu_ref[...])
    for j in range(n_lane_tiles):
        utb_ref[j] = out[:, j * LANES:(j + 1) * LANES]
    for b in range(bsz):
        for j in range(n_lane_tiles):
            o_ref[b, :, j * LANES:(j + 1) * LANES] = utb_ref[j, pl.ds(b, steps, stride=bsz), :]


def _s5(u, bm, cm, ar, ai, d, wglu, bglu):
    bsz, t, _ = u.shape
    rows = S5_TIME_CHUNK * bsz
    const3 = lambda i: (0, 0, 0)
    const2 = lambda i: (0, 0)
    io_spec = pl.BlockSpec((bsz, S5_TIME_CHUNK, SSM_WIDTH), lambda i: (0, i, 0))
    return pl.pallas_call(
        _s5_kernel,
        grid=(t // S5_TIME_CHUNK,),
        in_specs=[
            io_spec,
            pl.BlockSpec((2, S5_HALF, 2 * S5_HALF_STATE), const3),
            pl.BlockSpec((2, 2 * S5_HALF_STATE, S5_HALF), const3),
            pl.BlockSpec((2, 1, S5_HALF_STATE), const3),
            pl.BlockSpec((2, 1, S5_HALF_STATE), const3),
            pl.BlockSpec((1, SSM_WIDTH), const2),
            pl.BlockSpec((SSM_WIDTH, SSM_WIDTH), const2),
            pl.BlockSpec((1, SSM_WIDTH), const2),
        ],
        out_specs=io_spec,
        out_shape=jax.ShapeDtypeStruct((bsz, t, SSM_WIDTH), jnp.float32),
        scratch_shapes=[
            pltpu.VMEM((SSM_WIDTH // LANES, rows, LANES), jnp.float32),
            pltpu.VMEM((2, rows, 2 * S5_HALF_STATE), jnp.float32),
            pltpu.VMEM((2, bsz, 2 * S5_HALF_STATE), jnp.float32),
        ],
        compiler_params=pltpu.CompilerParams(
            dimension_semantics=("arbitrary",), vmem_limit_bytes=VMEM_LIMIT),
    )(u, bm, cm, ar, ai, d, wglu, bglu)


def _s5_params(lam_re, lam_im, log_dt, b_re, b_im, c_re, c_im):
    dt = jnp.exp(log_dt)[:, None]
    mag = jnp.exp(lam_re * dt)
    ab_re = mag * jnp.cos(lam_im * dt)
    ab_im = mag * jnp.sin(lam_im * dt)
    nr = ab_re - 1.0
    den = lam_re * lam_re + lam_im * lam_im
    f_re = (nr * lam_re + ab_im * lam_im) / den
    f_im = (ab_im * lam_re - nr * lam_im) / den
    bb_re = f_re[..., None] * b_re - f_im[..., None] * b_im
    bb_im = f_re[..., None] * b_im + f_im[..., None] * b_re
    gh = SSM_GROUPS // 2
    eye = jnp.eye(gh, dtype=jnp.float32)

    def in_half(bb):
        return jnp.einsum('gph,gk->ghkp', bb, eye).reshape(S5_HALF, S5_HALF_STATE)

    def out_half(cc):
        return jnp.einsum('ghp,gk->gpkh', cc, eye).reshape(S5_HALF_STATE, S5_HALF)

    bm = jnp.stack([jnp.concatenate([in_half(bb_re[s]), in_half(bb_im[s])], axis=1)
                    for s in (slice(0, gh), slice(gh, None))])
    cm = jnp.stack([jnp.concatenate([out_half(c_re[s]), -out_half(c_im[s])], axis=0)
                    for s in (slice(0, gh), slice(gh, None))])
    ar = ab_re.reshape(2, 1, S5_HALF_STATE)
    ai = ab_im.reshape(2, 1, S5_HALF_STATE)
    return bm.astype(jnp.bfloat16), cm.astype(jnp.bfloat16), ar, ai


def _out_ffn2_kernel(x1_ref, a_ref, s_ref, ga_ref, gs_ref, wo_ref, g2_ref, wg_ref, wu_ref, wd_ref,
                     gf_ref, o_ref):
    a_t = a_ref[0]
    r_a = lax.rsqrt(jnp.mean(a_t * a_t, axis=0, keepdims=True) + EPS)
    a = ((a_t * r_a).T * ga_ref[...]).astype(jnp.bfloat16)
    s = _rmsnorm(s_ref[...], gs_ref[...]).astype(jnp.bfloat16)
    x2 = x1_ref[...] + _dot(a, wo_ref[:ATTN_WIDTH, :]) + _dot(s, wo_ref[ATTN_WIDTH:, :])
    h = _rmsnorm(x2, g2_ref[...]).astype(jnp.bfloat16)
    x3 = x2 + 0.5 * _swiglu_acc(h, wg_ref, wu_ref, wd_ref)
    o_ref[...] = _rmsnorm(x3, gf_ref[...])


def _out_ffn2(x1, attn_t, ssm, ga, gs, wo, g2, wg, wu, wd, gf, bsz, t):
    n = bsz * t
    tm = min(TOKEN_TILE, t)
    nt = t // tm
    row = lambda i: (i, 0)
    return pl.pallas_call(
        _out_ffn2_kernel,
        grid=(n // tm,),
        in_specs=[
            pl.BlockSpec((tm, D_MODEL), row),
            pl.BlockSpec((1, ATTN_WIDTH, tm), lambda i: (i // nt, 0, i % nt)),
            pl.BlockSpec((tm, SSM_WIDTH), row),
            _resident((1, ATTN_WIDTH)), _resident((1, SSM_WIDTH)),
            _resident((D_MODEL, D_MODEL)),
            _resident((1, D_MODEL)),
            _resident((D_MODEL, D_FF)), _resident((D_MODEL, D_FF)), _resident((D_FF, D_MODEL)),
            _resident((1, D_MODEL)),
        ],
        out_specs=pl.BlockSpec((tm, D_MODEL), row),
        out_shape=jax.ShapeDtypeStruct((n, D_MODEL), jnp.float32),
        compiler_params=pltpu.CompilerParams(
            dimension_semantics=("parallel",), vmem_limit_bytes=VMEM_LIMIT),
    )(x1, attn_t, ssm, ga, gs, wo, g2, wg, wu, wd, gf)


def _reorder_w_in(w_in, gate_bias):
    per_g = N_GATE_COLS // N_KV
    g_lo = GATE_OFF
    pieces = [w_in[:, :g_lo]]
    bias = []
    gb = gate_bias.reshape(1, N_GATE_COLS)
    for g in range(N_KV):
        cols = w_in[:, g_lo + g * per_g:g_lo + (g + 1) * per_g]
        pieces.append(jnp.pad(cols, ((0, 0), (0, LANES - per_g))))
        bias.append(jnp.pad(gb[:, g * per_g:(g + 1) * per_g], ((0, 0), (0, LANES - per_g))))
    pieces.append(w_in[:, g_lo + N_GATE_COLS:])
    return jnp.concatenate(pieces, axis=1).astype(jnp.bfloat16), jnp.concatenate(bias, axis=1)


def kernel(x, ffn1_norm, ffn1_w_gate, ffn1_w_up, ffn1_w_down, mix_norm, w_in, gate_bias, cmp_pos_k, cmp_pos_v, cmp_k_w1, cmp_k_b1, cmp_k_w2, cmp_v_w1, cmp_v_b1, cmp_v_w2, s5_lambda_re, s5_lambda_im, s5_log_dt, s5_b_re, s5_b_im, s5_c_re, s5_c_im, s5_d, s5_w_glu, s5_b_glu, attn_out_norm, ssm_out_norm, w_out, ffn2_norm, ffn2_w_gate, ffn2_w_up, ffn2_w_down, final_norm):
    bsz, t, _ = x.shape
    n = bsz * t
    bf = jnp.bfloat16
    assert ffn1_norm.shape[0] == 1, "the final rmsnorm is fused into the single layer's last kernel"
    assert t % max(TOKEN_TILE, WINDOW) == 0 and t >= WINDOW + Q_BLOCK and x.shape[2] == D_MODEL
    xc = x.reshape(n, D_MODEL)
    for l in range(1):
        win, gb = _reorder_w_in(w_in[l], gate_bias[l])
        x1, q, kvc, ks, vst, kw, vwt, gates, u = _ffn1_proj(
            xc, ffn1_norm[l][None], ffn1_w_gate[l].astype(bf), ffn1_w_up[l].astype(bf),
            ffn1_w_down[l].astype(bf), mix_norm[l][None], win, gb, bsz, t)

        pos = jnp.stack([cmp_pos_k[l], cmp_pos_v[l]])
        w1 = jnp.stack([cmp_k_w1[l], cmp_v_w1[l]]).astype(bf)
        b1 = jnp.stack([cmp_k_b1[l], cmp_v_b1[l]])[:, None, :]
        w2 = jnp.stack([cmp_k_w2[l], cmp_v_w2[l]]).astype(bf)
        kcmp, kcmp_t = _compress(kvc, pos, w1, b1, w2, bsz)

        attn_t = _nsa(q, kcmp, kcmp_t, ks, vst, kw, vwt, gates, bsz, t)

        bm, cm, ar, ai = _s5_params(s5_lambda_re[l], s5_lambda_im[l], s5_log_dt[l], s5_b_re[l],
                                    s5_b_im[l], s5_c_re[l], s5_c_im[l])
        ssm = _s5(u.reshape(bsz, t, SSM_WIDTH), bm, cm, ar, ai,
                  s5_d[l].reshape(1, SSM_WIDTH), s5_w_glu[l].astype(bf), s5_b_glu[l][None])

        xc = _out_ffn2(x1, attn_t, ssm.reshape(n, SSM_WIDTH), attn_out_norm[l][None],
                       ssm_out_norm[l][None], w_out[l].astype(bf), ffn2_norm[l][None],
                       ffn2_w_gate[l].astype(bf), ffn2_w_up[l].astype(bf), ffn2_w_down[l].astype(bf),
                       final_norm[None], bsz, t)
    return xc.reshape(bsz, t, D_MODEL)
```

```python
import functools
import math

import jax
import jax.numpy as jnp
from jax import lax
from jax.experimental import pallas as pl
from jax.experimental.pallas import tpu as pltpu

D_MODEL = 1024
HEAD_DIM = 64
N_HEADS = 8
N_KV = 2
GQA_R = N_HEADS // N_KV
ATTN_WIDTH = N_HEADS * HEAD_DIM
CMP_LEN = 32
CMP_STRIDE = 16
CMP_HIDDEN = 256
SEL_BLOCK = 64
SEL_SHIFT = 6
SEL_TOPK = 16
WINDOW = 512
Q_BLOCK = 256
SSM_WIDTH = 512
SSM_GROUP = 16
SSM_GROUPS = SSM_WIDTH // SSM_GROUP
SSM_STATE = 64
D_FF = 2816
EPS = 1e-6

LANES = 128
SUBLANES = 8
FF_CHUNK = 256
TOKEN_TILE = 512
SEL_KEY_TILE = 256
S5_TIME_CHUNK = 64
S5_SCAN_PARTS = 2
PV_ROWS = HEAD_DIM + 16
TOPK_CHUNK = 16
S5_HALF = SSM_WIDTH // 2
S5_HALF_STATE = (SSM_GROUPS // 2) * SSM_STATE
VMEM_LIMIT = 56 * 1024 * 1024
NEG_BIG = -1e30
SCORE_DTYPE = jnp.bfloat16
Q_SCALE = HEAD_DIM ** -0.5 * math.log2(math.e)

Q_OFF = 0
KVC_OFF = 512
KVS_OFF = 768
KVW_OFF = 1024
GATE_OFF = 1280
U_OFF = 1536
IN_COLS_PADDED = 2048
N_GATE_COLS = N_HEADS * 3


def _dot(a, b):
    return jnp.dot(a, b, preferred_element_type=jnp.float32)


def _rmsnorm(x, g):
    r = lax.rsqrt(jnp.mean(x * x, axis=-1, keepdims=True) + EPS)
    return (x * r) * g


def _swiglu_acc(h, wg_ref, wu_ref, wd_ref):
    acc = jnp.zeros((h.shape[0], D_MODEL), jnp.float32)
    for f in range(0, D_FF, FF_CHUNK):
        gate = _dot(h, wg_ref[:, f:f + FF_CHUNK])
        up = _dot(h, wu_ref[:, f:f + FF_CHUNK])
        act = (gate * jax.nn.sigmoid(gate) * up).astype(jnp.bfloat16)
        acc = acc + _dot(act, wd_ref[f:f + FF_CHUNK, :])
    return acc


def _ones_row_pad(nrow, ncol, dtype):
    return jnp.where(lax.broadcasted_iota(jnp.int32, (nrow, ncol), 0) == 0, 1.0, 0.0).astype(dtype)


def _resident(shape):
    nd = len(shape)
    return pl.BlockSpec(shape, lambda *_: (0,) * nd, pipeline_mode=pl.Buffered(1))


def _ffn1_proj_kernel(x_ref, g1_ref, wg_ref, wu_ref, wd_ref, gm_ref, win_ref, gb_ref,
                      x1_ref, q_ref, kvc_ref, ks_ref, vst_ref, kw_ref, vwt_ref, gate_ref, u_ref):
    x = x_ref[...]
    h = _rmsnorm(x, g1_ref[...]).astype(jnp.bfloat16)
    x1 = x + 0.5 * _swiglu_acc(h, wg_ref, wu_ref, wd_ref)
    x1_ref[...] = x1
    h2 = _rmsnorm(x1, gm_ref[...]).astype(jnp.bfloat16)
    p = _dot(h2, win_ref[...])
    for hh in range(N_HEADS):
        lo = Q_OFF + hh * HEAD_DIM
        q_ref[hh, 0] = (p[:, lo:lo + HEAD_DIM] * Q_SCALE).T.astype(q_ref.dtype)
    kv_cols = N_KV * HEAD_DIM
    for kind in range(2):
        kvc_ref[kind] = p[:, KVC_OFF + kind * kv_cols:KVC_OFF + (kind + 1) * kv_cols]
    for g in range(N_KV):
        for off, k_ref, vt_ref in ((KVS_OFF, ks_ref, vst_ref), (KVW_OFF, kw_ref, vwt_ref)):
            k_lo = off + g * HEAD_DIM
            v_lo = off + (N_KV + g) * HEAD_DIM
            k_ref[g] = p[:, k_lo:k_lo + HEAD_DIM].astype(k_ref.dtype)
            vt_ref[g, 0, :HEAD_DIM] = p[:, v_lo:v_lo + HEAD_DIM].T.astype(vt_ref.dtype)
            vt_ref[g, 0, HEAD_DIM:] = _ones_row_pad(PV_ROWS - HEAD_DIM, p.shape[0], vt_ref.dtype)
    for g in range(N_KV):
        lo = GATE_OFF + g * LANES
        gate = jax.nn.sigmoid(p[:, lo:lo + LANES] + gb_ref[:, g * LANES:(g + 1) * LANES])
        gate_ref[g, 0] = gate.T
    u_ref[...] = p[:, U_OFF:U_OFF + SSM_WIDTH]


def _ffn1_proj(x2d, g1, wg, wu, wd, gm, win, gb, bsz, t):
    n = bsz * t
    tm = min(TOKEN_TILE, t)
    nt = t // tm
    row = lambda i: (i, 0)
    lead = lambda i: (0, i, 0)
    k_spec = pl.BlockSpec((N_KV, tm, HEAD_DIM), lead)
    vt_spec = lambda nrow: pl.BlockSpec((N_KV, 1, nrow, tm), lambda i: (0, i // nt, 0, i % nt))
    k_shape = jax.ShapeDtypeStruct((N_KV, n, HEAD_DIM), jnp.bfloat16)
    vt_shape = lambda nrow: jax.ShapeDtypeStruct((N_KV, bsz, nrow, t), jnp.bfloat16)
    return pl.pallas_call(
        _ffn1_proj_kernel,
        grid=(n // tm,),
        in_specs=[
            pl.BlockSpec((tm, D_MODEL), row),
            _resident((1, D_MODEL)),
            _resident((D_MODEL, D_FF)), _resident((D_MODEL, D_FF)), _resident((D_FF, D_MODEL)),
            _resident((1, D_MODEL)),
            _resident((D_MODEL, IN_COLS_PADDED)),
            _resident((1, N_KV * LANES)),
        ],
        out_specs=[
            pl.BlockSpec((tm, D_MODEL), row),
            pl.BlockSpec((N_HEADS, 1, HEAD_DIM, tm), lambda i: (0, i // nt, 0, i % nt)),
            pl.BlockSpec((2, tm, N_KV * HEAD_DIM), lead),
            k_spec, vt_spec(PV_ROWS), k_spec, vt_spec(PV_ROWS),
            pl.BlockSpec((N_KV, 1, LANES, tm), lambda i: (0, i // nt, 0, i % nt)),
            pl.BlockSpec((tm, SSM_WIDTH), row),
        ],
        out_shape=[
            jax.ShapeDtypeStruct((n, D_MODEL), jnp.float32),
            jax.ShapeDtypeStruct((N_HEADS, bsz, HEAD_DIM, t), jnp.bfloat16),
            jax.ShapeDtypeStruct((2, n, N_KV * HEAD_DIM), jnp.float32),
            k_shape, vt_shape(PV_ROWS), k_shape, vt_shape(PV_ROWS),
            jax.ShapeDtypeStruct((N_KV, bsz, LANES, t), jnp.float32),
            jax.ShapeDtypeStruct((n, SSM_WIDTH), jnp.float32),
        ],
        compiler_params=pltpu.CompilerParams(
            dimension_semantics=("parallel",), vmem_limit_bytes=VMEM_LIMIT),
    )(x2d, g1, wg, wu, wd, gm, win, gb)


def _compress_kernel(x_ref, pos_ref, w1_ref, b1_ref, w2_ref, o_ref, ot_ref):
    nrow = x_ref.shape[1] // CMP_STRIDE
    tops = [jnp.zeros((nrow, CMP_HIDDEN), jnp.float32) for _ in range(N_KV)]
    bots = [jnp.zeros((nrow, CMP_HIDDEN), jnp.float32) for _ in range(N_KV)]
    for l in range(CMP_STRIDE):
        xl = x_ref[0, pl.ds(l, nrow, stride=CMP_STRIDE), :]
        w_top = w1_ref[0, l * HEAD_DIM:(l + 1) * HEAD_DIM, :]
        w_bot = w1_ref[0, (CMP_STRIDE + l) * HEAD_DIM:(CMP_STRIDE + l + 1) * HEAD_DIM, :]
        for g in range(N_KV):
            xg = xl[:, g * HEAD_DIM:(g + 1) * HEAD_DIM]
            tops[g] = tops[g] + _dot((xg + pos_ref[0, l:l + 1, :]).astype(jnp.bfloat16), w_top)
            bots[g] = bots[g] + _dot((xg + pos_ref[0, CMP_STRIDE + l:CMP_STRIDE + l + 1, :])
                                     .astype(jnp.bfloat16), w_bot)
    for g in range(N_KV):
        pre = tops[g] + pltpu.roll(bots[g], nrow - 1, 0) + b1_ref[0]
        out = _dot(jax.nn.gelu(pre).astype(jnp.bfloat16), w2_ref[0])
        rid = lax.broadcasted_iota(jnp.int32, out.shape, 0)
        out = jnp.where(rid < nrow - 1, out, 0.0)
        o_ref[g, 0] = out
        ot_ref[g, 0, :HEAD_DIM] = out.T
        ot_ref[g, 0, HEAD_DIM:] = _ones_row_pad(PV_ROWS - HEAD_DIM, nrow, ot_ref.dtype)


def _compress(kvc, pos, w1, b1, w2, bsz):
    t = kvc.shape[1] // bsz
    nrow = t // CMP_STRIDE
    kind = lambda k, b: (k, 0, 0)
    return pl.pallas_call(
        _compress_kernel,
        grid=(2, bsz),
        in_specs=[
            pl.BlockSpec((1, t, N_KV * HEAD_DIM), lambda k, b: (k, b, 0)),
            pl.BlockSpec((1, CMP_LEN, HEAD_DIM), kind),
            pl.BlockSpec((1, CMP_LEN * HEAD_DIM, CMP_HIDDEN), kind),
            pl.BlockSpec((1, 1, CMP_HIDDEN), kind),
            pl.BlockSpec((1, CMP_HIDDEN, HEAD_DIM), kind),
        ],
        out_specs=[pl.BlockSpec((N_KV, 1, nrow, HEAD_DIM), lambda k, b: (k, b, 0, 0)),
                   pl.BlockSpec((N_KV, 1, PV_ROWS, nrow), lambda k, b: (k, b, 0, 0))],
        out_shape=[jax.ShapeDtypeStruct((2 * N_KV, bsz, nrow, HEAD_DIM), jnp.float32),
                   jax.ShapeDtypeStruct((2 * N_KV, bsz, PV_ROWS, nrow), jnp.float32)],
        compiler_params=pltpu.CompilerParams(dimension_semantics=("parallel", "parallel")),
    )(kvc, pos, w1, b1, w2)


def _split3_bf16(x):
    hi = x.astype(jnp.bfloat16)
    r1 = x - hi.astype(jnp.float32)
    mid = r1.astype(jnp.bfloat16)
    lo = (r1 - mid.astype(jnp.float32)).astype(jnp.bfloat16)
    return hi, mid, lo


def _q_t(q_ref, g):
    return jnp.concatenate([q_ref[g * GQA_R + r, 0] for r in range(GQA_R)], axis=1)


def _sweep_start(q_ref, k_ref, tile, s_ref, p_ref, acc_ref):
    p_ref[:, 1] = jnp.zeros_like(p_ref[:, 1])
    acc_ref[...] = jnp.zeros_like(acc_ref)
    for g in range(N_KV):
        s_ref[g, 0] = _dot(k_ref[g, 0:tile, :], _q_t(q_ref, g)).astype(SCORE_DTYPE)


def _sweep_t(q_ref, k_ref, vt_ref, n_tiles, tile, bias_fn, s_ref, p_ref, acc_ref):
    rows = acc_ref.shape[2]
    last_tile = k_ref.shape[1] // tile - 1

    def k_tile(g, i):
        return k_ref[g, pl.ds(pl.multiple_of(i * tile, tile), tile), :]

    def vt_tile(g, i):
        return vt_ref[g, 0, :, pl.ds(pl.multiple_of(i * tile, tile), tile)]

    def add_bias(s, g, i, slot):
        bias = bias_fn(g, i).astype(s.dtype)
        s = jnp.concatenate([s[:, r * Q_BLOCK:(r + 1) * Q_BLOCK] + bias
                             for r in range(rows // Q_BLOCK)], axis=1)
        s_ref[g, slot] = s
        return jnp.max(s, axis=0, keepdims=True).astype(jnp.float32)

    def phase(i, carry, g, slot):
        m, s_max = carry
        m_new = jnp.maximum(m, s_max)
        p_ref[g, slot] = jnp.exp2(s_ref[g, slot] - m_new.astype(SCORE_DTYPE)).astype(p_ref.dtype)
        pv = _dot(vt_tile(g, jnp.maximum(i - 1, 0)), p_ref[g, 1 - slot])
        nxt = jnp.minimum(i + 1, last_tile)
        s_max_next = add_bias(_dot(k_tile(g, nxt), _q_t(q_ref, g)).astype(SCORE_DTYPE), g, nxt, 1 - slot)
        acc_ref[g] = jnp.exp2(m - m_new) * (acc_ref[g] + pv)
        return m_new, s_max_next

    def body(j, carry):
        for slot in range(2):
            carry = tuple(phase(2 * j + slot, carry[g], g, slot) for g in range(N_KV))
        return carry

    init = tuple((jnp.full((1, rows), NEG_BIG, jnp.float32), add_bias(s_ref[g, 0], g, 0, 0))
                 for g in range(N_KV))
    n_pairs = (n_tiles + 1) // 2
    lax.fori_loop(0, n_pairs, body, init)
    outs = []
    for g in range(N_KV):
        acc = acc_ref[g] + _dot(vt_tile(g, 2 * n_pairs - 1), p_ref[g, 1])
        outs.append(acc[:HEAD_DIM] / acc[HEAD_DIM:HEAD_DIM + 1])
    return outs


def _nsa_kernel(q_ref, kc_ref, vct_ref, ks_ref, vst_ref, kw_ref, vwt_ref, gate_ref, o_ref,
                bias_ref, rank_ref, causal_ref, s_ref, p_ref, acc_ref):
    c = pl.program_id(1)
    rows = GQA_R * Q_BLOCK
    heads = range(N_KV)
    ncp = kc_ref.shape[2]
    t = ks_ref.shape[1]
    nb = t // SEL_BLOCK

    q_t = [_q_t(q_ref, g) for g in heads]
    s_c = [_dot(kc_ref[g, 0].astype(jnp.bfloat16), q_t[g]) for g in heads]
    band = WINDOW + Q_BLOCK
    w0 = pl.multiple_of(jnp.maximum(c * Q_BLOCK - WINDOW, 0), Q_BLOCK)
    s_w = [_dot(kw_ref[g, pl.ds(w0, band), :], q_t[g]) for g in heads]
    _sweep_start(q_ref, ks_ref, SEL_KEY_TILE, s_ref, p_ref, acc_ref)

    n_c = lax.broadcasted_iota(jnp.int32, (ncp, Q_BLOCK), 0)
    tq_c = c * Q_BLOCK + lax.broadcasted_iota(jnp.int32, (ncp, Q_BLOCK), 1)
    bias_c = jnp.where(n_c * CMP_STRIDE + (CMP_LEN - 1) <= tq_c, 0.0, NEG_BIG)
    tq_row = c * Q_BLOCK + (lax.broadcasted_iota(jnp.int32, (1, rows), 1) & (Q_BLOCK - 1))
    p_c, o_cmp = [], []
    for g in heads:
        s = jnp.concatenate([s_c[g][:, r * Q_BLOCK:(r + 1) * Q_BLOCK] + bias_c for r in range(GQA_R)], axis=1)
        e_c = jnp.exp2(s - jnp.max(s, axis=0, keepdims=True))
        ev_c = _dot(vct_ref[g, 0].astype(jnp.bfloat16), e_c.astype(jnp.bfloat16))
        inv_c = jnp.where(tq_row >= CMP_LEN - 1, 1.0 / ev_c[HEAD_DIM:HEAD_DIM + 1], 0.0)
        p_c.append(e_c * inv_c)
        o_cmp.append(ev_c[:HEAD_DIM] * inv_c)

    kpos_w = w0 + lax.broadcasted_iota(jnp.int32, (band, Q_BLOCK), 0)
    tq_w = c * Q_BLOCK + lax.broadcasted_iota(jnp.int32, (band, Q_BLOCK), 1)
    bias_w = jnp.where((kpos_w <= tq_w) & (kpos_w > tq_w - WINDOW), 0.0, NEG_BIG).astype(SCORE_DTYPE)
    o_win = []
    for g in heads:
        s = s_w[g].astype(SCORE_DTYPE)
        s = jnp.concatenate([s[:, r * Q_BLOCK:(r + 1) * Q_BLOCK] + bias_w for r in range(GQA_R)], axis=1)
        p_w = jnp.exp2(s - jnp.max(s, axis=0, keepdims=True))
        pv_w = _dot(vwt_ref[g, 0, :, pl.ds(w0, band)], p_w.astype(jnp.bfloat16))
        o_win.append(pv_w[:HEAD_DIM] / pv_w[HEAD_DIM:HEAD_DIM + 1])

    jb = lax.broadcasted_iota(jnp.int32, (nb, ncp), 0)
    nn = lax.broadcasted_iota(jnp.int32, (nb, ncp), 1)
    overlap_t = ((nn * CMP_STRIDE < jb * SEL_BLOCK + SEL_BLOCK)
                 & (nn * CMP_STRIDE + CMP_LEN > jb * SEL_BLOCK)
                 & (nn < ncp - 1))
    overlap_t = jnp.where(overlap_t, 1.0, 0.0).astype(jnp.bfloat16)
    j_t = lax.broadcasted_iota(jnp.int32, (nb, Q_BLOCK), 0)
    tq_t = c * Q_BLOCK + lax.broadcasted_iota(jnp.int32, (nb, Q_BLOCK), 1)
    cur = tq_t >> SEL_SHIFT
    forced = (j_t == 0) | (j_t == cur) | (j_t == cur - 1)
    valid = j_t * SEL_BLOCK <= tq_t
    score = []
    for g in heads:
        p_sum = p_c[g][:, 0:Q_BLOCK]
        for r in range(1, GQA_R):
            p_sum = p_sum + p_c[g][:, r * Q_BLOCK:(r + 1) * Q_BLOCK]
        hi, mid, lo = _split3_bf16(p_sum)
        imp = _dot(overlap_t, hi) + _dot(overlap_t, mid) + _dot(overlap_t, lo)
        score.append(jnp.where(forced, jnp.inf, jnp.where(valid, imp, -jnp.inf)))

    rank_ref[...] = jnp.zeros_like(rank_ref)
    j_grp = lax.broadcasted_iota(jnp.int32, (SUBLANES, LANES), 0)

    def count_chunk(g, cand, ranked):
        groups = [slice(v, v + SUBLANES) for v in range(ranked * TOPK_CHUNK, (ranked + 1) * TOPK_CHUNK, SUBLANES)]
        for lt in range(0, Q_BLOCK, LANES):
            sc = score[g][:, lt:lt + LANES]
            ranks = [rank_ref[g, grp, lt:lt + LANES] for grp in groups]
            for jp in range(cand * TOPK_CHUNK, (cand + 1) * TOPK_CHUNK):
                row = jnp.broadcast_to(sc[jp:jp + 1, :], (SUBLANES, LANES))
                for v, grp in enumerate(groups):
                    if jp < grp.start:
                        beats = row >= sc[grp]
                    elif jp >= grp.stop:
                        beats = row > sc[grp]
                    else:
                        beats = (row > sc[grp]) | ((row == sc[grp]) & (j_grp > jp - grp.start))
                    ranks[v] = ranks[v] + jnp.where(beats, 1.0, 0.0)
            for v, grp in enumerate(groups):
                rank_ref[g, grp, lt:lt + LANES] = ranks[v]

    blocks_seen = (c + 1) * (Q_BLOCK // SEL_BLOCK)
    for shell in range(nb // TOPK_CHUNK):
        @pl.when(blocks_seen > shell * TOPK_CHUNK)
        def _():
            for g in heads:
                for other in range(shell):
                    count_chunk(g, shell, other)
                    count_chunk(g, other, shell)
                count_chunk(g, shell, shell)

    bias_ref[...] = jnp.where(rank_ref[...] < min(SEL_TOPK, nb), 0.0, NEG_BIG)

    tiles_per_q = Q_BLOCK // SEL_KEY_TILE

    @pl.when(c == 0)
    def _():
        kpos = lax.broadcasted_iota(jnp.int32, (SEL_KEY_TILE, Q_BLOCK), 0)
        qpos = lax.broadcasted_iota(jnp.int32, (SEL_KEY_TILE, Q_BLOCK), 1)
        causal_ref[0] = jnp.zeros((SEL_KEY_TILE, Q_BLOCK), jnp.float32)
        for j in range(tiles_per_q):
            causal_ref[1 + j] = jnp.where(kpos + j * SEL_KEY_TILE <= qpos, 0.0, NEG_BIG)
        causal_ref[1 + tiles_per_q] = jnp.full((SEL_KEY_TILE, Q_BLOCK), NEG_BIG, jnp.float32)

    blocks_per_tile = SEL_KEY_TILE // SEL_BLOCK

    def sel_bias(g, kt):
        blocks = [jnp.broadcast_to(bias_ref[g, pl.ds(kt * blocks_per_tile + jj, 1), :], (SEL_BLOCK, Q_BLOCK))
                  for jj in range(blocks_per_tile)]
        pattern = jnp.clip(kt - c * tiles_per_q + 1, 0, tiles_per_q + 1)
        return jnp.concatenate(blocks, axis=0) + causal_ref[pattern]

    def out_rows(g, r):
        return slice((g * GQA_R + r) * HEAD_DIM, (g * GQA_R + r + 1) * HEAD_DIM)

    for g in heads:
        for r in range(GQA_R):
            sl = slice(r * Q_BLOCK, (r + 1) * Q_BLOCK)
            o_ref[0, out_rows(g, r), :] = (gate_ref[g, 0, 3 * r:3 * r + 1, :] * o_cmp[g][:, sl]
                                           + gate_ref[g, 0, 3 * r + 2:3 * r + 3, :] * o_win[g][:, sl])

    o_slc = _sweep_t(q_ref, ks_ref, vst_ref, (c + 1) * tiles_per_q, SEL_KEY_TILE, sel_bias,
                     s_ref, p_ref, acc_ref)
    for g in heads:
        for r in range(GQA_R):
            sl = slice(r * Q_BLOCK, (r + 1) * Q_BLOCK)
            o_ref[0, out_rows(g, r), :] += gate_ref[g, 0, 3 * r + 1:3 * r + 2, :] * o_slc[g][:, sl]


def _nsa(q, kcmp, kcmp_t, ks, vst, kw, vwt, gates, bsz, t):
    nq = t // Q_BLOCK
    ncp = kcmp.shape[2]
    rows = GQA_R * Q_BLOCK
    nb = t // SEL_BLOCK
    assert t % (2 * SEL_KEY_TILE) == 0, "the selected-block sweep runs pairs of key tiles"
    assert Q_BLOCK % SEL_KEY_TILE == 0, "the causal bias of a key tile is one of a few fixed patterns"
    assert nb % TOPK_CHUNK == 0
    k_spec = pl.BlockSpec((N_KV, t, HEAD_DIM), lambda b, c: (0, b, 0))
    vt_spec = pl.BlockSpec((N_KV, 1, PV_ROWS, t), lambda b, c: (0, b, 0, 0))
    return pl.pallas_call(
        _nsa_kernel,
        grid=(bsz, nq),
        in_specs=[
            pl.BlockSpec((N_HEADS, 1, HEAD_DIM, Q_BLOCK), lambda b, c: (0, b, 0, c)),
            pl.BlockSpec((N_KV, 1, ncp, HEAD_DIM), lambda b, c: (0, b, 0, 0)),
            pl.BlockSpec((N_KV, 1, PV_ROWS, ncp), lambda b, c: (1, b, 0, 0)),
            k_spec, vt_spec, k_spec, vt_spec,
            pl.BlockSpec((N_KV, 1, LANES, Q_BLOCK), lambda b, c: (0, b, 0, c)),
        ],
        out_specs=pl.BlockSpec((1, ATTN_WIDTH, Q_BLOCK), lambda b, c: (b, 0, c)),
        out_shape=jax.ShapeDtypeStruct((bsz, ATTN_WIDTH, t), jnp.float32),
        scratch_shapes=[pltpu.VMEM((N_KV, nb, Q_BLOCK), jnp.float32),
                        pltpu.VMEM((N_KV, nb, Q_BLOCK), jnp.float32),
                        pltpu.VMEM((Q_BLOCK // SEL_KEY_TILE + 2, SEL_KEY_TILE, Q_BLOCK), jnp.float32),
                        pltpu.VMEM((N_KV, 2, SEL_KEY_TILE, rows), SCORE_DTYPE),
                        pltpu.VMEM((N_KV, 2, SEL_KEY_TILE, rows), jnp.bfloat16),
                        pltpu.VMEM((N_KV, PV_ROWS, rows), jnp.float32)],
        compiler_params=pltpu.CompilerParams(
            dimension_semantics=("parallel", "arbitrary"),
            vmem_limit_bytes=VMEM_LIMIT),
    )(q, kcmp, kcmp_t, ks, vst, kw, vwt, gates)


def _s5_kernel(u_ref, bm_ref, cm_ref, ar_ref, ai_ref, d_ref, wglu_ref, bglu_ref, o_ref,
               utb_ref, bu_ref, state_ref):
    bsz, steps, _ = u_ref.shape

    @pl.when(pl.program_id(0) == 0)
    def _():
        state_ref[...] = jnp.zeros_like(state_ref)

    n_lane_tiles = utb_ref.shape[0]
    for b in range(bsz):
        for j in range(n_lane_tiles):
            utb_ref[j, pl.ds(b, steps, stride=bsz), :] = u_ref[b, :, j * LANES:(j + 1) * LANES]
    u = jnp.concatenate([utb_ref[j] for j in range(n_lane_tiles)], axis=1)
    ub = u.astype(jnp.bfloat16)
    re = slice(0, S5_HALF_STATE)
    im = slice(S5_HALF_STATE, 2 * S5_HALF_STATE)
    part_steps = steps // S5_SCAN_PARTS
    part_rows = [slice(j * part_steps * bsz, (j + 1) * part_steps * bsz) for j in range(S5_SCAN_PARTS)]

    def scan_part(half, j, carry):
        a_re = jnp.broadcast_to(ar_ref[half], (bsz, S5_HALF_STATE))
        a_im = jnp.broadcast_to(ai_ref[half], (bsz, S5_HALF_STATE))
        xr, xi = carry
        for i in range(j * part_steps, (j + 1) * part_steps):
            rows = slice(i * bsz, (i + 1) * bsz)
            xr, xi = (a_re * xr - a_im * xi + bu_ref[half, rows, re],
                      a_re * xi + a_im * xr + bu_ref[half, rows, im])
            bu_ref[half, rows, re] = xr
            bu_ref[half, rows, im] = xi
        return xr, xi

    bu_ref[0] = _dot(ub[:, :S5_HALF], bm_ref[0])
    carry = state_ref[0, :, re], state_ref[0, :, im]
    for j in range(S5_SCAN_PARTS):
        carry = scan_part(0, j, carry)
        bu_ref[1, part_rows[j]] = _dot(ub[part_rows[j], S5_HALF:], bm_ref[1])
    state_ref[0, :, re], state_ref[0, :, im] = carry
    carry = state_ref[1, :, re], state_ref[1, :, im]
    y0 = []
    for j in range(S5_SCAN_PARTS):
        carry = scan_part(1, j, carry)
        y0.append(_dot(bu_ref[0, part_rows[j]].astype(jnp.bfloat16), cm_ref[0]))
    state_ref[1, :, re], state_ref[1, :, im] = carry
    y1 = _dot(bu_ref[1].astype(jnp.bfloat16), cm_ref[1])
    y = jnp.concatenate([jnp.concatenate(y0, axis=0), y1], axis=-1) + d_ref[...] * u
    y = jax.nn.gelu(y)
    out = y * jax.nn.sigmoid(_dot(y.astype(jnp.bfloat16), wglu_ref[...]) + bglu_ref[...])
    for j in range(n_lane_tiles):
        utb_ref[j] = out[:, j * LANES:(j + 1) * LANES]
    for b in range(bsz):
        for j in range(n_lane_tiles):
            o_ref[b, :, j * LANES:(j + 1) * LANES] = utb_ref[j, pl.ds(b, steps, stride=bsz), :]


def _s5(u, bm, cm, ar, ai, d, wglu, bglu):
    bsz, t, _ = u.shape
    rows = S5_TIME_CHUNK * bsz
    const3 = lambda i: (0, 0, 0)
    const2 = lambda i: (0, 0)
    io_spec = pl.BlockSpec((bsz, S5_TIME_CHUNK, SSM_WIDTH), lambda i: (0, i, 0))
    return pl.pallas_call(
        _s5_kernel,
        grid=(t // S5_TIME_CHUNK,),
        in_specs=[
            io_spec,
            pl.BlockSpec((2, S5_HALF, 2 * S5_HALF_STATE), const3),
            pl.BlockSpec((2, 2 * S5_HALF_STATE, S5_HALF), const3),
            pl.BlockSpec((2, 1, S5_HALF_STATE), const3),
            pl.BlockSpec((2, 1, S5_HALF_STATE), const3),
            pl.BlockSpec((1, SSM_WIDTH), const2),
            pl.BlockSpec((SSM_WIDTH, SSM_WIDTH), const2),
            pl.BlockSpec((1, SSM_WIDTH), const2),
        ],
        out_specs=io_spec,
        out_shape=jax.ShapeDtypeStruct((bsz, t, SSM_WIDTH), jnp.float32),
        scratch_shapes=[
            pltpu.VMEM((SSM_WIDTH // LANES, rows, LANES), jnp.float32),
            pltpu.VMEM((2, rows, 2 * S5_HALF_STATE), jnp.float32),
            pltpu.VMEM((2, bsz, 2 * S5_HALF_STATE), jnp.float32),
        ],
        compiler_params=pltpu.CompilerParams(
            dimension_semantics=("arbitrary",), vmem_limit_bytes=VMEM_LIMIT),
    )(u, bm, cm, ar, ai, d, wglu, bglu)


def _s5_params(lam_re, lam_im, log_dt, b_re, b_im, c_re, c_im):
    dt = jnp.exp(log_dt)[:, None]
    mag = jnp.exp(lam_re * dt)
    ab_re = mag * jnp.cos(lam_im * dt)
    ab_im = mag * jnp.sin(lam_im * dt)
    nr = ab_re - 1.0
    den = lam_re * lam_re + lam_im * lam_im
    f_re = (nr * lam_re + ab_im * lam_im) / den
    f_im = (ab_im * lam_re - nr * lam_im) / den
    bb_re = f_re[..., None] * b_re - f_im[..., None] * b_im
    bb_im = f_re[..., None] * b_im + f_im[..., None] * b_re
    gh = SSM_GROUPS // 2
    eye = jnp.eye(gh, dtype=jnp.float32)

    def in_half(bb):
        return jnp.einsum('gph,gk->ghkp', bb, eye).reshape(S5_HALF, S5_HALF_STATE)

    def out_half(cc):
        return jnp.einsum('ghp,gk->gpkh', cc, eye).reshape(S5_HALF_STATE, S5_HALF)

    bm = jnp.stack([jnp.concatenate([in_half(bb_re[s]), in_half(bb_im[s])], axis=1)
                    for s in (slice(0, gh), slice(gh, None))])
    cm = jnp.stack([jnp.concatenate([out_half(c_re[s]), -out_half(c_im[s])], axis=0)
                    for s in (slice(0, gh), slice(gh, None))])
    ar = ab_re.reshape(2, 1, S5_HALF_STATE)
    ai = ab_im.reshape(2, 1, S5_HALF_STATE)
    return bm.astype(jnp.bfloat16), cm.astype(jnp.bfloat16), ar, ai


def _out_ffn2_kernel(x1_ref, a_ref, s_ref, ga_ref, gs_ref, wo_ref, g2_ref, wg_ref, wu_ref, wd_ref,
                     gf_ref, o_ref):
    a_t = a_ref[0]
    r_a = lax.rsqrt(jnp.mean(a_t * a_t, axis=0, keepdims=True) + EPS)
    a = ((a_t * r_a).T * ga_ref[...]).astype(jnp.bfloat16)
    s = _rmsnorm(s_ref[...], gs_ref[...]).astype(jnp.bfloat16)
    x2 = x1_ref[...] + _dot(a, wo_ref[:ATTN_WIDTH, :]) + _dot(s, wo_ref[ATTN_WIDTH:, :])
    h = _rmsnorm(x2, g2_ref[...]).astype(jnp.bfloat16)
    x3 = x2 + 0.5 * _swiglu_acc(h, wg_ref, wu_ref, wd_ref)
    o_ref[...] = _rmsnorm(x3, gf_ref[...])


def _out_ffn2(x1, attn_t, ssm, ga, gs, wo, g2, wg, wu, wd, gf, bsz, t):
    n = bsz * t
    tm = min(TOKEN_TILE, t)
    nt = t // tm
    row = lambda i: (i, 0)
    return pl.pallas_call(
        _out_ffn2_kernel,
        grid=(n // tm,),
        in_specs=[
            pl.BlockSpec((tm, D_MODEL), row),
            pl.BlockSpec((1, ATTN_WIDTH, tm), lambda i: (i // nt, 0, i % nt)),
            pl.BlockSpec((tm, SSM_WIDTH), row),
            _resident((1, ATTN_WIDTH)), _resident((1, SSM_WIDTH)),
            _resident((D_MODEL, D_MODEL)),
            _resident((1, D_MODEL)),
            _resident((D_MODEL, D_FF)), _resident((D_MODEL, D_FF)), _resident((D_FF, D_MODEL)),
            _resident((1, D_MODEL)),
        ],
        out_specs=pl.BlockSpec((tm, D_MODEL), row),
        out_shape=jax.ShapeDtypeStruct((n, D_MODEL), jnp.float32),
        compiler_params=pltpu.CompilerParams(
            dimension_semantics=("parallel",), vmem_limit_bytes=VMEM_LIMIT),
    )(x1, attn_t, ssm, ga, gs, wo, g2, wg, wu, wd, gf)


def _reorder_w_in(w_in, gate_bias):
    per_g = N_GATE_COLS // N_KV
    g_lo = GATE_OFF
    pieces = [w_in[:, :g_lo]]
    bias = []
    gb = gate_bias.reshape(1, N_GATE_COLS)
    for g in range(N_KV):
        cols = w_in[:, g_lo + g * per_g:g_lo + (g + 1) * per_g]
        pieces.append(jnp.pad(cols, ((0, 0), (0, LANES - per_g))))
        bias.append(jnp.pad(gb[:, g * per_g:(g + 1) * per_g], ((0, 0), (0, LANES - per_g))))
    pieces.append(w_in[:, g_lo + N_GATE_COLS:])
    return jnp.concatenate(pieces, axis=1).astype(jnp.bfloat16), jnp.concatenate(bias, axis=1)


def kernel(x, ffn1_norm, ffn1_w_gate, ffn1_w_up, ffn1_w_down, mix_norm, w_in, gate_bias, cmp_pos_k, cmp_pos_v, cmp_k_w1, cmp_k_b1, cmp_k_w2, cmp_v_w1, cmp_v_b1, cmp_v_w2, s5_lambda_re, s5_lambda_im, s5_log_dt, s5_b_re, s5_b_im, s5_c_re, s5_c_im, s5_d, s5_w_glu, s5_b_glu, attn_out_norm, ssm_out_norm, w_out, ffn2_norm, ffn2_w_gate, ffn2_w_up, ffn2_w_down, final_norm):
    bsz, t, _ = x.shape
    n = bsz * t
    bf = jnp.bfloat16
    assert ffn1_norm.shape[0] == 1, "the final rmsnorm is fused into the single layer's last kernel"
    assert t % max(TOKEN_TILE, WINDOW) == 0 and t >= WINDOW + Q_BLOCK and x.shape[2] == D_MODEL
    xc = x.reshape(n, D_MODEL)
    for l in range(1):
        win, gb = _reorder_w_in(w_in[l], gate_bias[l])
        x1, q, kvc, ks, vst, kw, vwt, gates, u = _ffn1_proj(
            xc, ffn1_norm[l][None], ffn1_w_gate[l].astype(bf), ffn1_w_up[l].astype(bf),
            ffn1_w_down[l].astype(bf), mix_norm[l][None], win, gb, bsz, t)

        pos = jnp.stack([cmp_pos_k[l], cmp_pos_v[l]])
        w1 = jnp.stack([cmp_k_w1[l], cmp_v_w1[l]]).astype(bf)
        b1 = jnp.stack([cmp_k_b1[l], cmp_v_b1[l]])[:, None, :]
        w2 = jnp.stack([cmp_k_w2[l], cmp_v_w2[l]]).astype(bf)
        kcmp, kcmp_t = _compress(kvc, pos, w1, b1, w2, bsz)

        attn_t = _nsa(q, kcmp, kcmp_t, ks, vst, kw, vwt, gates, bsz, t)

        bm, cm, ar, ai = _s5_params(s5_lambda_re[l], s5_lambda_im[l], s5_log_dt[l], s5_b_re[l],
                                    s5_b_im[l], s5_c_re[l], s5_c_im[l])
        ssm = _s5(u.reshape(bsz, t, SSM_WIDTH), bm, cm, ar, ai,
                  s5_d[l].reshape(1, SSM_WIDTH), s5_w_glu[l].astype(bf), s5_b_glu[l][None])

        xc = _out_ffn2(x1, attn_t, ssm.reshape(n, SSM_WIDTH), attn_out_norm[l][None],
                       ssm_out_norm[l][None], w_out[l].astype(bf), ffn2_norm[l][None],
                       ffn2_w_gate[l].astype(bf), ffn2_w_up[l].astype(bf), ffn2_w_down[l].astype(bf),
                       final_norm[None], bsz, t)
    return xc.reshape(bsz, t, D_MODEL)
```

```python
import math

import jax
import jax.numpy as jnp
from jax import lax
from jax.experimental import pallas as pl
from jax.experimental.pallas import tpu as pltpu

D_MODEL = 1024
HEAD_DIM = 64
N_HEADS = 8
N_KV = 2
GQA_R = N_HEADS // N_KV
ATTN_WIDTH = N_HEADS * HEAD_DIM
CMP_LEN = 32
CMP_STRIDE = 16
CMP_HIDDEN = 256
SEL_BLOCK = 64
SEL_SHIFT = 6
SEL_TOPK = 16
WINDOW = 512
Q_BLOCK = 256
SSM_WIDTH = 512
SSM_GROUP = 16
SSM_GROUPS = SSM_WIDTH // SSM_GROUP
SSM_STATE = 64
D_FF = 2816
EPS = 1e-6

LANES = 128
SUBLANES = 8
FF_CHUNK = 256
TOKEN_TILE = 512
SEL_KEY_TILE = 256
S5_TIME_CHUNK = 128
S5_SCAN_PARTS = 2
PV_ROWS = HEAD_DIM + 16
TOPK_CHUNK = 16
S5_HALF = SSM_WIDTH // 2
S5_HALF_STATE = (SSM_GROUPS // 2) * SSM_STATE
VMEM_LIMIT = 56 * 1024 * 1024
NEG_BIG = -1e30
SCORE_DTYPE = jnp.bfloat16
Q_SCALE = HEAD_DIM ** -0.5 * math.log2(math.e)

Q_OFF = 0
KVC_OFF = 512
KVS_OFF = 768
KVW_OFF = 1024
GATE_OFF = 1280
U_OFF = 1536
IN_COLS_PADDED = 2048
N_GATE_COLS = N_HEADS * 3


def _dot(a, b):
    return jnp.dot(a, b, preferred_element_type=jnp.float32)


def _rmsnorm(x, g):
    r = lax.rsqrt(jnp.mean(x * x, axis=-1, keepdims=True) + EPS)
    return (x * r) * g


def _swiglu_acc(h, wg_ref, wu_ref, wd_ref):
    acc = jnp.zeros((h.shape[0], D_MODEL), jnp.float32)
    for f in range(0, D_FF, FF_CHUNK):
        gate = _dot(h, wg_ref[:, f:f + FF_CHUNK])
        up = _dot(h, wu_ref[:, f:f + FF_CHUNK])
        act = (gate * jax.nn.sigmoid(gate) * up).astype(jnp.bfloat16)
        acc = acc + _dot(act, wd_ref[f:f + FF_CHUNK, :])
    return acc


def _ones_row_pad(nrow, ncol, dtype):
    return jnp.where(lax.broadcasted_iota(jnp.int32, (nrow, ncol), 0) == 0, 1.0, 0.0).astype(dtype)


def _resident(shape):
    nd = len(shape)
    return pl.BlockSpec(shape, lambda *_: (0,) * nd, pipeline_mode=pl.Buffered(1))


def _ffn1_proj_kernel(x_ref, g1_ref, wg_ref, wu_ref, wd_ref, gm_ref, win_ref, gb_ref,
                      x1_ref, q_ref, kvc_ref, ks_ref, vst_ref, kw_ref, vwt_ref, gate_ref, u_ref):
    x = x_ref[...]
    h = _rmsnorm(x, g1_ref[...]).astype(jnp.bfloat16)
    x1 = x + 0.5 * _swiglu_acc(h, wg_ref, wu_ref, wd_ref)
    x1_ref[...] = x1
    h2 = _rmsnorm(x1, gm_ref[...]).astype(jnp.bfloat16)
    p = _dot(h2, win_ref[...])
    for hh in range(N_HEADS):
        lo = Q_OFF + hh * HEAD_DIM
        q_ref[hh, 0] = (p[:, lo:lo + HEAD_DIM] * Q_SCALE).T.astype(q_ref.dtype)
    kv_cols = N_KV * HEAD_DIM
    for kind in range(2):
        kvc_ref[kind] = p[:, KVC_OFF + kind * kv_cols:KVC_OFF + (kind + 1) * kv_cols]
    for g in range(N_KV):
        for off, k_ref, vt_ref in ((KVS_OFF, ks_ref, vst_ref), (KVW_OFF, kw_ref, vwt_ref)):
            k_lo = off + g * HEAD_DIM
            v_lo = off + (N_KV + g) * HEAD_DIM
            k_ref[g] = p[:, k_lo:k_lo + HEAD_DIM].astype(k_ref.dtype)
            vt_ref[g, 0, :HEAD_DIM] = p[:, v_lo:v_lo + HEAD_DIM].T.astype(vt_ref.dtype)
            vt_ref[g, 0, HEAD_DIM:] = _ones_row_pad(PV_ROWS - HEAD_DIM, p.shape[0], vt_ref.dtype)
    for g in range(N_KV):
        lo = GATE_OFF + g * LANES
        gate = jax.nn.sigmoid(p[:, lo:lo + LANES] + gb_ref[:, g * LANES:(g + 1) * LANES])
        gate_ref[g, 0] = gate.T
    u_ref[...] = p[:, U_OFF:U_OFF + SSM_WIDTH]


def _ffn1_proj(x2d, g1, wg, wu, wd, gm, win, gb, bsz, t):
    n = bsz * t
    tm = min(TOKEN_TILE, t)
    nt = t // tm
    row = lambda i: (i, 0)
    lead = lambda i: (0, i, 0)
    k_spec = pl.BlockSpec((N_KV, tm, HEAD_DIM), lead)
    vt_spec = lambda nrow: pl.BlockSpec((N_KV, 1, nrow, tm), lambda i: (0, i // nt, 0, i % nt))
    k_shape = jax.ShapeDtypeStruct((N_KV, n, HEAD_DIM), jnp.bfloat16)
    vt_shape = lambda nrow: jax.ShapeDtypeStruct((N_KV, bsz, nrow, t), jnp.bfloat16)
    return pl.pallas_call(
        _ffn1_proj_kernel,
        grid=(n // tm,),
        in_specs=[
            pl.BlockSpec((tm, D_MODEL), row),
            _resident((1, D_MODEL)),
            _resident((D_MODEL, D_FF)), _resident((D_MODEL, D_FF)), _resident((D_FF, D_MODEL)),
            _resident((1, D_MODEL)),
            _resident((D_MODEL, IN_COLS_PADDED)),
            _resident((1, N_KV * LANES)),
        ],
        out_specs=[
            pl.BlockSpec((tm, D_MODEL), row),
            pl.BlockSpec((N_HEADS, 1, HEAD_DIM, tm), lambda i: (0, i // nt, 0, i % nt)),
            pl.BlockSpec((2, tm, N_KV * HEAD_DIM), lead),
            k_spec, vt_spec(PV_ROWS), k_spec, vt_spec(PV_ROWS),
            pl.BlockSpec((N_KV, 1, LANES, tm), lambda i: (0, i // nt, 0, i % nt)),
            pl.BlockSpec((tm, SSM_WIDTH), row),
        ],
        out_shape=[
            jax.ShapeDtypeStruct((n, D_MODEL), jnp.float32),
            jax.ShapeDtypeStruct((N_HEADS, bsz, HEAD_DIM, t), jnp.bfloat16),
            jax.ShapeDtypeStruct((2, n, N_KV * HEAD_DIM), jnp.float32),
            k_shape, vt_shape(PV_ROWS), k_shape, vt_shape(PV_ROWS),
            jax.ShapeDtypeStruct((N_KV, bsz, LANES, t), jnp.float32),
            jax.ShapeDtypeStruct((n, SSM_WIDTH), jnp.float32),
        ],
        compiler_params=pltpu.CompilerParams(
            dimension_semantics=("parallel",), vmem_limit_bytes=VMEM_LIMIT),
    )(x2d, g1, wg, wu, wd, gm, win, gb)


def _compress_kernel(x_ref, pos_ref, w1_ref, b1_ref, w2_ref, o_ref, ot_ref):
    nrow = x_ref.shape[1] // CMP_STRIDE
    tops = [jnp.zeros((nrow, CMP_HIDDEN), jnp.float32) for _ in range(N_KV)]
    bots = [jnp.zeros((nrow, CMP_HIDDEN), jnp.float32) for _ in range(N_KV)]
    for l in range(CMP_STRIDE):
        xl = x_ref[0, pl.ds(l, nrow, stride=CMP_STRIDE), :]
        w_top = w1_ref[0, l * HEAD_DIM:(l + 1) * HEAD_DIM, :]
        w_bot = w1_ref[0, (CMP_STRIDE + l) * HEAD_DIM:(CMP_STRIDE + l + 1) * HEAD_DIM, :]
        for g in range(N_KV):
            xg = xl[:, g * HEAD_DIM:(g + 1) * HEAD_DIM]
            tops[g] = tops[g] + _dot((xg + pos_ref[0, l:l + 1, :]).astype(jnp.bfloat16), w_top)
            bots[g] = bots[g] + _dot((xg + pos_ref[0, CMP_STRIDE + l:CMP_STRIDE + l + 1, :])
                                     .astype(jnp.bfloat16), w_bot)
    for g in range(N_KV):
        pre = tops[g] + pltpu.roll(bots[g], nrow - 1, 0) + b1_ref[0]
        out = _dot(jax.nn.gelu(pre).astype(jnp.bfloat16), w2_ref[0])
        rid = lax.broadcasted_iota(jnp.int32, out.shape, 0)
        out = jnp.where(rid < nrow - 1, out, 0.0)
        o_ref[g, 0] = out
        ot_ref[g, 0, :HEAD_DIM] = out.T
        ot_ref[g, 0, HEAD_DIM:] = _ones_row_pad(PV_ROWS - HEAD_DIM, nrow, ot_ref.dtype)


def _compress(kvc, pos, w1, b1, w2, bsz):
    t = kvc.shape[1] // bsz
    nrow = t // CMP_STRIDE
    kind = lambda k, b: (k, 0, 0)
    return pl.pallas_call(
        _compress_kernel,
        grid=(2, bsz),
        in_specs=[
            pl.BlockSpec((1, t, N_KV * HEAD_DIM), lambda k, b: (k, b, 0)),
            pl.BlockSpec((1, CMP_LEN, HEAD_DIM), kind),
            pl.BlockSpec((1, CMP_LEN * HEAD_DIM, CMP_HIDDEN), kind),
            pl.BlockSpec((1, 1, CMP_HIDDEN), kind),
            pl.BlockSpec((1, CMP_HIDDEN, HEAD_DIM), kind),
        ],
        out_specs=[pl.BlockSpec((N_KV, 1, nrow, HEAD_DIM), lambda k, b: (k, b, 0, 0)),
                   pl.BlockSpec((N_KV, 1, PV_ROWS, nrow), lambda k, b: (k, b, 0, 0))],
        out_shape=[jax.ShapeDtypeStruct((2 * N_KV, bsz, nrow, HEAD_DIM), jnp.float32),
                   jax.ShapeDtypeStruct((2 * N_KV, bsz, PV_ROWS, nrow), jnp.float32)],
        compiler_params=pltpu.CompilerParams(dimension_semantics=("parallel", "parallel")),
    )(kvc, pos, w1, b1, w2)


def _split3_bf16(x):
    hi = x.astype(jnp.bfloat16)
    r1 = x - hi.astype(jnp.float32)
    mid = r1.astype(jnp.bfloat16)
    lo = (r1 - mid.astype(jnp.float32)).astype(jnp.bfloat16)
    return hi, mid, lo


def _q_t(q_ref, g):
    return jnp.concatenate([q_ref[g * GQA_R + r, 0] for r in range(GQA_R)], axis=1)


def _sweep_start(q_ref, k_ref, tile, s_ref, p_ref, acc_ref):
    p_ref[:, 1] = jnp.zeros_like(p_ref[:, 1])
    acc_ref[...] = jnp.zeros_like(acc_ref)
    for g in range(N_KV):
        s_ref[g, 0] = _dot(k_ref[g, 0:tile, :], _q_t(q_ref, g)).astype(SCORE_DTYPE)


def _sweep_t(q_ref, k_ref, vt_ref, n_tiles, tile, bias_fn, s_ref, p_ref, acc_ref):
    rows = acc_ref.shape[2]
    last_tile = k_ref.shape[1] // tile - 1

    def k_tile(g, i):
        return k_ref[g, pl.ds(pl.multiple_of(i * tile, tile), tile), :]

    def vt_tile(g, i):
        return vt_ref[g, 0, :, pl.ds(pl.multiple_of(i * tile, tile), tile)]

    def add_bias(s, g, i, slot):
        bias = bias_fn(g, i).astype(s.dtype)
        s = jnp.concatenate([s[:, r * Q_BLOCK:(r + 1) * Q_BLOCK] + bias
                             for r in range(rows // Q_BLOCK)], axis=1)
        s_ref[g, slot] = s
        return jnp.max(s, axis=0, keepdims=True).astype(jnp.float32)

    def phase(i, carry, g, slot):
        m, s_max = carry
        m_new = jnp.maximum(m, s_max)
        p_ref[g, slot] = jnp.exp2(s_ref[g, slot] - m_new.astype(SCORE_DTYPE)).astype(p_ref.dtype)
        pv = _dot(vt_tile(g, jnp.maximum(i - 1, 0)), p_ref[g, 1 - slot])
        nxt = jnp.minimum(i + 1, last_tile)
        s_max_next = add_bias(_dot(k_tile(g, nxt), _q_t(q_ref, g)).astype(SCORE_DTYPE), g, nxt, 1 - slot)
        acc_ref[g] = jnp.exp2(m - m_new) * (acc_ref[g] + pv)
        return m_new, s_max_next

    def body(j, carry):
        for slot in range(2):
            carry = tuple(phase(2 * j + slot, carry[g], g, slot) for g in range(N_KV))
        return carry

    init = tuple((jnp.full((1, rows), NEG_BIG, jnp.float32), add_bias(s_ref[g, 0], g, 0, 0))
                 for g in range(N_KV))
    n_pairs = (n_tiles + 1) // 2
    lax.fori_loop(0, n_pairs, body, init)
    outs = []
    for g in range(N_KV):
        acc = acc_ref[g] + _dot(vt_tile(g, 2 * n_pairs - 1), p_ref[g, 1])
        outs.append(acc[:HEAD_DIM] / acc[HEAD_DIM:HEAD_DIM + 1])
    return outs


def _nsa_kernel(q_ref, kc_ref, vct_ref, ks_ref, vst_ref, kw_ref, vwt_ref, gate_ref, o_ref,
                bias_ref, rank_ref, causal_ref, s_ref, p_ref, acc_ref):
    c = pl.program_id(1)
    rows = GQA_R * Q_BLOCK
    heads = range(N_KV)
    ncp = kc_ref.shape[2]
    t = ks_ref.shape[1]
    nb = t // SEL_BLOCK

    q_t = [_q_t(q_ref, g) for g in heads]
    s_c = [_dot(kc_ref[g, 0].astype(jnp.bfloat16), q_t[g]) for g in heads]
    band = WINDOW + Q_BLOCK
    w0 = pl.multiple_of(jnp.maximum(c * Q_BLOCK - WINDOW, 0), Q_BLOCK)
    s_w = [_dot(kw_ref[g, pl.ds(w0, band), :], q_t[g]) for g in heads]
    _sweep_start(q_ref, ks_ref, SEL_KEY_TILE, s_ref, p_ref, acc_ref)

    n_c = lax.broadcasted_iota(jnp.int32, (ncp, Q_BLOCK), 0)
    tq_c = c * Q_BLOCK + lax.broadcasted_iota(jnp.int32, (ncp, Q_BLOCK), 1)
    bias_c = jnp.where(n_c * CMP_STRIDE + (CMP_LEN - 1) <= tq_c, 0.0, NEG_BIG)
    tq_row = c * Q_BLOCK + (lax.broadcasted_iota(jnp.int32, (1, rows), 1) & (Q_BLOCK - 1))
    p_c, o_cmp = [], []
    for g in heads:
        s = jnp.concatenate([s_c[g][:, r * Q_BLOCK:(r + 1) * Q_BLOCK] + bias_c for r in range(GQA_R)], axis=1)
        e_c = jnp.exp2(s - jnp.max(s, axis=0, keepdims=True))
        ev_c = _dot(vct_ref[g, 0].astype(jnp.bfloat16), e_c.astype(jnp.bfloat16))
        inv_c = jnp.where(tq_row >= CMP_LEN - 1, 1.0 / ev_c[HEAD_DIM:HEAD_DIM + 1], 0.0)
        p_c.append(e_c * inv_c)
        o_cmp.append(ev_c[:HEAD_DIM] * inv_c)

    kpos_w = w0 + lax.broadcasted_iota(jnp.int32, (band, Q_BLOCK), 0)
    tq_w = c * Q_BLOCK + lax.broadcasted_iota(jnp.int32, (band, Q_BLOCK), 1)
    bias_w = jnp.where((kpos_w <= tq_w) & (kpos_w > tq_w - WINDOW), 0.0, NEG_BIG).astype(SCORE_DTYPE)
    o_win = []
    for g in heads:
        s = s_w[g].astype(SCORE_DTYPE)
        s = jnp.concatenate([s[:, r * Q_BLOCK:(r + 1) * Q_BLOCK] + bias_w for r in range(GQA_R)], axis=1)
        p_w = jnp.exp2(s - jnp.max(s, axis=0, keepdims=True))
        pv_w = _dot(vwt_ref[g, 0, :, pl.ds(w0, band)], p_w.astype(jnp.bfloat16))
        o_win.append(pv_w[:HEAD_DIM] / pv_w[HEAD_DIM:HEAD_DIM + 1])

    jb = lax.broadcasted_iota(jnp.int32, (nb, ncp), 0)
    nn = lax.broadcasted_iota(jnp.int32, (nb, ncp), 1)
    overlap_t = ((nn * CMP_STRIDE < jb * SEL_BLOCK + SEL_BLOCK)
                 & (nn * CMP_STRIDE + CMP_LEN > jb * SEL_BLOCK)
                 & (nn < ncp - 1))
    overlap_t = jnp.where(overlap_t, 1.0, 0.0).astype(jnp.bfloat16)
    j_t = lax.broadcasted_iota(jnp.int32, (nb, Q_BLOCK), 0)
    tq_t = c * Q_BLOCK + lax.broadcasted_iota(jnp.int32, (nb, Q_BLOCK), 1)
    cur = tq_t >> SEL_SHIFT
    forced = (j_t == 0) | (j_t == cur) | (j_t == cur - 1)
    valid = j_t * SEL_BLOCK <= tq_t
    score = []
    for g in heads:
        p_sum = p_c[g][:, 0:Q_BLOCK]
        for r in range(1, GQA_R):
            p_sum = p_sum + p_c[g][:, r * Q_BLOCK:(r + 1) * Q_BLOCK]
        hi, mid, lo = _split3_bf16(p_sum)
        imp = _dot(overlap_t, hi) + _dot(overlap_t, mid) + _dot(overlap_t, lo)
        score.append(jnp.where(forced, jnp.inf, jnp.where(valid, imp, -jnp.inf)))

    rank_ref[...] = jnp.zeros_like(rank_ref)
    j_grp = lax.broadcasted_iota(jnp.int32, (SUBLANES, LANES), 0)

    def count_chunk(g, cand, ranked):
        groups = [slice(v, v + SUBLANES) for v in range(ranked * TOPK_CHUNK, (ranked + 1) * TOPK_CHUNK, SUBLANES)]
        for lt in range(0, Q_BLOCK, LANES):
            sc = score[g][:, lt:lt + LANES]
            ranks = [rank_ref[g, grp, lt:lt + LANES] for grp in groups]
            for jp in range(cand * TOPK_CHUNK, (cand + 1) * TOPK_CHUNK):
                row = jnp.broadcast_to(sc[jp:jp + 1, :], (SUBLANES, LANES))
                for v, grp in enumerate(groups):
                    if jp < grp.start:
                        beats = row >= sc[grp]
                    elif jp >= grp.stop:
                        beats = row > sc[grp]
                    else:
                        beats = (row > sc[grp]) | ((row == sc[grp]) & (j_grp > jp - grp.start))
                    ranks[v] = ranks[v] + jnp.where(beats, 1.0, 0.0)
            for v, grp in enumerate(groups):
                rank_ref[g, grp, lt:lt + LANES] = ranks[v]

    blocks_seen = (c + 1) * (Q_BLOCK // SEL_BLOCK)
    for shell in range(nb // TOPK_CHUNK):
        @pl.when(blocks_seen > shell * TOPK_CHUNK)
        def _():
            for g in heads:
                for other in range(shell):
                    count_chunk(g, shell, other)
                    count_chunk(g, other, shell)
                count_chunk(g, shell, shell)

    bias_ref[...] = jnp.where(rank_ref[...] < min(SEL_TOPK, nb), 0.0, NEG_BIG)

    tiles_per_q = Q_BLOCK // SEL_KEY_TILE

    @pl.when(c == 0)
    def _():
        kpos = lax.broadcasted_iota(jnp.int32, (SEL_KEY_TILE, Q_BLOCK), 0)
        qpos = lax.broadcasted_iota(jnp.int32, (SEL_KEY_TILE, Q_BLOCK), 1)
        causal_ref[0] = jnp.zeros((SEL_KEY_TILE, Q_BLOCK), jnp.float32)
        for j in range(tiles_per_q):
            causal_ref[1 + j] = jnp.where(kpos + j * SEL_KEY_TILE <= qpos, 0.0, NEG_BIG)
        causal_ref[1 + tiles_per_q] = jnp.full((SEL_KEY_TILE, Q_BLOCK), NEG_BIG, jnp.float32)

    blocks_per_tile = SEL_KEY_TILE // SEL_BLOCK

    def sel_bias(g, kt):
        blocks = [jnp.broadcast_to(bias_ref[g, pl.ds(kt * blocks_per_tile + jj, 1), :], (SEL_BLOCK, Q_BLOCK))
                  for jj in range(blocks_per_tile)]
        pattern = jnp.clip(kt - c * tiles_per_q + 1, 0, tiles_per_q + 1)
        return jnp.concatenate(blocks, axis=0) + causal_ref[pattern]

    def out_rows(g, r):
        return slice((g * GQA_R + r) * HEAD_DIM, (g * GQA_R + r + 1) * HEAD_DIM)

    for g in heads:
        for r in range(GQA_R):
            sl = slice(r * Q_BLOCK, (r + 1) * Q_BLOCK)
            o_ref[0, out_rows(g, r), :] = (gate_ref[g, 0, 3 * r:3 * r + 1, :] * o_cmp[g][:, sl]
                                           + gate_ref[g, 0, 3 * r + 2:3 * r + 3, :] * o_win[g][:, sl])

    o_slc = _sweep_t(q_ref, ks_ref, vst_ref, (c + 1) * tiles_per_q, SEL_KEY_TILE, sel_bias,
                     s_ref, p_ref, acc_ref)
    for g in heads:
        for r in range(GQA_R):
            sl = slice(r * Q_BLOCK, (r + 1) * Q_BLOCK)
            o_ref[0, out_rows(g, r), :] += gate_ref[g, 0, 3 * r + 1:3 * r + 2, :] * o_slc[g][:, sl]


def _nsa(q, kcmp, kcmp_t, ks, vst, kw, vwt, gates, bsz, t):
    nq = t // Q_BLOCK
    ncp = kcmp.shape[2]
    rows = GQA_R * Q_BLOCK
    nb = t // SEL_BLOCK
    assert t % (2 * SEL_KEY_TILE) == 0, "the selected-block sweep runs pairs of key tiles"
    assert Q_BLOCK % SEL_KEY_TILE == 0, "the causal bias of a key tile is one of a few fixed patterns"
    assert nb % TOPK_CHUNK == 0
    k_spec = pl.BlockSpec((N_KV, t, HEAD_DIM), lambda b, c: (0, b, 0))
    vt_spec = pl.BlockSpec((N_KV, 1, PV_ROWS, t), lambda b, c: (0, b, 0, 0))
    return pl.pallas_call(
        _nsa_kernel,
        grid=(bsz, nq),
        in_specs=[
            pl.BlockSpec((N_HEADS, 1, HEAD_DIM, Q_BLOCK), lambda b, c: (0, b, 0, c)),
            pl.BlockSpec((N_KV, 1, ncp, HEAD_DIM), lambda b, c: (0, b, 0, 0)),
            pl.BlockSpec((N_KV, 1, PV_ROWS, ncp), lambda b, c: (1, b, 0, 0)),
            k_spec, vt_spec, k_spec, vt_spec,
            pl.BlockSpec((N_KV, 1, LANES, Q_BLOCK), lambda b, c: (0, b, 0, c)),
        ],
        out_specs=pl.BlockSpec((1, ATTN_WIDTH, Q_BLOCK), lambda b, c: (b, 0, c)),
        out_shape=jax.ShapeDtypeStruct((bsz, ATTN_WIDTH, t), jnp.float32),
        scratch_shapes=[pltpu.VMEM((N_KV, nb, Q_BLOCK), jnp.float32),
                        pltpu.VMEM((N_KV, nb, Q_BLOCK), jnp.float32),
                        pltpu.VMEM((Q_BLOCK // SEL_KEY_TILE + 2, SEL_KEY_TILE, Q_BLOCK), jnp.float32),
                        pltpu.VMEM((N_KV, 2, SEL_KEY_TILE, rows), SCORE_DTYPE),
                        pltpu.VMEM((N_KV, 2, SEL_KEY_TILE, rows), jnp.bfloat16),
                        pltpu.VMEM((N_KV, PV_ROWS, rows), jnp.float32)],
        compiler_params=pltpu.CompilerParams(
            dimension_semantics=("parallel", "arbitrary"),
            vmem_limit_bytes=VMEM_LIMIT),
    )(q, kcmp, kcmp_t, ks, vst, kw, vwt, gates)


def _s5_kernel(u_ref, bm_ref, cm_ref, ar_ref, ai_ref, d_ref, wglu_ref, bglu_ref, o_ref,
               utb_ref, bu_ref, state_ref):
    bsz, steps, _ = u_ref.shape

    @pl.when(pl.program_id(0) == 0)
    def _():
        state_ref[...] = jnp.zeros_like(state_ref)

    n_lane_tiles = utb_ref.shape[0]
    for b in range(bsz):
        for j in range(n_lane_tiles):
            utb_ref[j, pl.ds(b, steps, stride=bsz), :] = u_ref[b, :, j * LANES:(j + 1) * LANES]
    u = jnp.concatenate([utb_ref[j] for j in range(n_lane_tiles)], axis=1)
    ub = u.astype(jnp.bfloat16)
    re = slice(0, S5_HALF_STATE)
    im = slice(S5_HALF_STATE, 2 * S5_HALF_STATE)
    part_steps = steps // S5_SCAN_PARTS
    part_rows = [slice(j * part_steps * bsz, (j + 1) * part_steps * bsz) for j in range(S5_SCAN_PARTS)]

    def scan_part(half, j, carry):
        a_re = jnp.broadcast_to(ar_ref[half], (bsz, S5_HALF_STATE))
        a_im = jnp.broadcast_to(ai_ref[half], (bsz, S5_HALF_STATE))
        xr, xi = carry
        for i in range(j * part_steps, (j + 1) * part_steps):
            rows = slice(i * bsz, (i + 1) * bsz)
            xr, xi = (a_re * xr - a_im * xi + bu_ref[half, rows, re],
                      a_re * xi + a_im * xr + bu_ref[half, rows, im])
            bu_ref[half, rows, re] = xr
            bu_ref[half, rows, im] = xi
        return xr, xi

    bu_ref[0] = _dot(ub[:, :S5_HALF], bm_ref[0])
    carry = state_ref[0, :, re], state_ref[0, :, im]
    for j in range(S5_SCAN_PARTS):
        carry = scan_part(0, j, carry)
        bu_ref[1, part_rows[j]] = _dot(ub[part_rows[j], S5_HALF:], bm_ref[1])
    state_ref[0, :, re], state_ref[0, :, im] = carry
    carry = state_ref[1, :, re], state_ref[1, :, im]
    y0 = []
    for j in range(S5_SCAN_PARTS):
        carry = scan_part(1, j, carry)
        y0.append(_dot(bu_ref[0, part_rows[j]].astype(jnp.bfloat16), cm_ref[0]))
    state_ref[1, :, re], state_ref[1, :, im] = carry
    y1 = _dot(bu_ref[1].astype(jnp.bfloat16), cm_ref[1])
    y = jnp.concatenate([jnp.concatenate(y0, axis=0), y1], axis=-1) + d_ref[...] * u
    y = jax.nn.gelu(y)
    out = y * jax.nn.sigmoid(_dot(y.astype(jnp.bfloat16), wglu_ref[...]) + bglu_ref[...])
    for j in range(n_lane_tiles):
        utb_ref[j] = out[:, j * LANES:(j + 1) * LANES]
    for b in range(bsz):
        for j in range(n_lane_tiles):
            o_ref[b, :, j * LANES:(j + 1) * LANES] = utb_ref[j, pl.ds(b, steps, stride=bsz), :]


def _s5(u, bm, cm, ar, ai, d, wglu, bglu):
    bsz, t, _ = u.shape
    rows = S5_TIME_CHUNK * bsz
    const3 = lambda i: (0, 0, 0)
    const2 = lambda i: (0, 0)
    io_spec = pl.BlockSpec((bsz, S5_TIME_CHUNK, SSM_WIDTH), lambda i: (0, i, 0))
    return pl.pallas_call(
        _s5_kernel,
        grid=(t // S5_TIME_CHUNK,),
        in_specs=[
            io_spec,
            pl.BlockSpec((2, S5_HALF, 2 * S5_HALF_STATE), const3),
            pl.BlockSpec((2, 2 * S5_HALF_STATE, S5_HALF), const3),
            pl.BlockSpec((2, 1, S5_HALF_STATE), const3),
            pl.BlockSpec((2, 1, S5_HALF_STATE), const3),
            pl.BlockSpec((1, SSM_WIDTH), const2),
            pl.BlockSpec((SSM_WIDTH, SSM_WIDTH), const2),
            pl.BlockSpec((1, SSM_WIDTH), const2),
        ],
        out_specs=io_spec,
        out_shape=jax.ShapeDtypeStruct((bsz, t, SSM_WIDTH), jnp.float32),
        scratch_shapes=[
            pltpu.VMEM((SSM_WIDTH // LANES, rows, LANES), jnp.float32),
            pltpu.VMEM((2, rows, 2 * S5_HALF_STATE), jnp.float32),
            pltpu.VMEM((2, bsz, 2 * S5_HALF_STATE), jnp.float32),
        ],
        compiler_params=pltpu.CompilerParams(
            dimension_semantics=("arbitrary",), vmem_limit_bytes=VMEM_LIMIT),
    )(u, bm, cm, ar, ai, d, wglu, bglu)


def _s5_params(lam_re, lam_im, log_dt, b_re, b_im, c_re, c_im):
    dt = jnp.exp(log_dt)[:, None]
    mag = jnp.exp(lam_re * dt)
    ab_re = mag * jnp.cos(lam_im * dt)
    ab_im = mag * jnp.sin(lam_im * dt)
    nr = ab_re - 1.0
    den = lam_re * lam_re + lam_im * lam_im
    f_re = (nr * lam_re + ab_im * lam_im) / den
    f_im = (ab_im * lam_re - nr * lam_im) / den
    bb_re = f_re[..., None] * b_re - f_im[..., None] * b_im
    bb_im = f_re[..., None] * b_im + f_im[..., None] * b_re
    gh = SSM_GROUPS // 2
    eye = jnp.eye(gh, dtype=jnp.float32)

    def in_half(bb):
        return jnp.einsum('gph,gk->ghkp', bb, eye).reshape(S5_HALF, S5_HALF_STATE)

    def out_half(cc):
        return jnp.einsum('ghp,gk->gpkh', cc, eye).reshape(S5_HALF_STATE, S5_HALF)

    bm = jnp.stack([jnp.concatenate([in_half(bb_re[s]), in_half(bb_im[s])], axis=1)
                    for s in (slice(0, gh), slice(gh, None))])
    cm = jnp.stack([jnp.concatenate([out_half(c_re[s]), -out_half(c_im[s])], axis=0)
                    for s in (slice(0, gh), slice(gh, None))])
    ar = ab_re.reshape(2, 1, S5_HALF_STATE)
    ai = ab_im.reshape(2, 1, S5_HALF_STATE)
    return bm.astype(jnp.bfloat16), cm.astype(jnp.bfloat16), ar, ai


def _out_ffn2_kernel(x1_ref, a_ref, s_ref, ga_ref, gs_ref, wo_ref, g2_ref, wg_ref, wu_ref, wd_ref,
                     gf_ref, o_ref):
    a_t = a_ref[0]
    r_a = lax.rsqrt(jnp.mean(a_t * a_t, axis=0, keepdims=True) + EPS)
    a = ((a_t * r_a).T * ga_ref[...]).astype(jnp.bfloat16)
    s = _rmsnorm(s_ref[...], gs_ref[...]).astype(jnp.bfloat16)
    x2 = x1_ref[...] + _dot(a, wo_ref[:ATTN_WIDTH, :]) + _dot(s, wo_ref[ATTN_WIDTH:, :])
    h = _rmsnorm(x2, g2_ref[...]).astype(jnp.bfloat16)
    x3 = x2 + 0.5 * _swiglu_acc(h, wg_ref, wu_ref, wd_ref)
    o_ref[...] = _rmsnorm(x3, gf_ref[...])


def _out_ffn2(x1, attn_t, ssm, ga, gs, wo, g2, wg, wu, wd, gf, bsz, t):
    n = bsz * t
    tm = min(TOKEN_TILE, t)
    nt = t // tm
    row = lambda i: (i, 0)
    return pl.pallas_call(
        _out_ffn2_kernel,
        grid=(n // tm,),
        in_specs=[
            pl.BlockSpec((tm, D_MODEL), row),
            pl.BlockSpec((1, ATTN_WIDTH, tm), lambda i: (i // nt, 0, i % nt)),
            pl.BlockSpec((tm, SSM_WIDTH), row),
            _resident((1, ATTN_WIDTH)), _resident((1, SSM_WIDTH)),
            _resident((D_MODEL, D_MODEL)),
            _resident((1, D_MODEL)),
            _resident((D_MODEL, D_FF)), _resident((D_MODEL, D_FF)), _resident((D_FF, D_MODEL)),
            _resident((1, D_MODEL)),
        ],
        out_specs=pl.BlockSpec((tm, D_MODEL), row),
        out_shape=jax.ShapeDtypeStruct((n, D_MODEL), jnp.float32),
        compiler_params=pltpu.CompilerParams(
            dimension_semantics=("parallel",), vmem_limit_bytes=VMEM_LIMIT),
    )(x1, attn_t, ssm, ga, gs, wo, g2, wg, wu, wd, gf)


def _reorder_w_in(w_in, gate_bias):
    per_g = N_GATE_COLS // N_KV
    g_lo = GATE_OFF
    pieces = [w_in[:, :g_lo]]
    bias = []
    gb = gate_bias.reshape(1, N_GATE_COLS)
    for g in range(N_KV):
        cols = w_in[:, g_lo + g * per_g:g_lo + (g + 1) * per_g]
        pieces.append(jnp.pad(cols, ((0, 0), (0, LANES - per_g))))
        bias.append(jnp.pad(gb[:, g * per_g:(g + 1) * per_g], ((0, 0), (0, LANES - per_g))))
    pieces.append(w_in[:, g_lo + N_GATE_COLS:])
    return jnp.concatenate(pieces, axis=1).astype(jnp.bfloat16), jnp.concatenate(bias, axis=1)


def kernel(x, ffn1_norm, ffn1_w_gate, ffn1_w_up, ffn1_w_down, mix_norm, w_in, gate_bias, cmp_pos_k, cmp_pos_v, cmp_k_w1, cmp_k_b1, cmp_k_w2, cmp_v_w1, cmp_v_b1, cmp_v_w2, s5_lambda_re, s5_lambda_im, s5_log_dt, s5_b_re, s5_b_im, s5_c_re, s5_c_im, s5_d, s5_w_glu, s5_b_glu, attn_out_norm, ssm_out_norm, w_out, ffn2_norm, ffn2_w_gate, ffn2_w_up, ffn2_w_down, final_norm):
    bsz, t, _ = x.shape
    n = bsz * t
    bf = jnp.bfloat16
    assert ffn1_norm.shape[0] == 1, "the final rmsnorm is fused into the single layer's last kernel"
    assert t % max(TOKEN_TILE, WINDOW) == 0 and t >= WINDOW + Q_BLOCK and x.shape[2] == D_MODEL
    xc = x.reshape(n, D_MODEL)
    for l in range(1):
        win, gb = _reorder_w_in(w_in[l], gate_bias[l])
        x1, q, kvc, ks, vst, kw, vwt, gates, u = _ffn1_proj(
            xc, ffn1_norm[l][None], ffn1_w_gate[l].astype(bf), ffn1_w_up[l].astype(bf),
            ffn1_w_down[l].astype(bf), mix_norm[l][None], win, gb, bsz, t)

        pos = jnp.stack([cmp_pos_k[l], cmp_pos_v[l]])
        w1 = jnp.stack([cmp_k_w1[l], cmp_v_w1[l]]).astype(bf)
        b1 = jnp.stack([cmp_k_b1[l], cmp_v_b1[l]])[:, None, :]
        w2 = jnp.stack([cmp_k_w2[l], cmp_v_w2[l]]).astype(bf)
        kcmp, kcmp_t = _compress(kvc, pos, w1, b1, w2, bsz)

        attn_t = _nsa(q, kcmp, kcmp_t, ks, vst, kw, vwt, gates, bsz, t)

        bm, cm, ar, ai = _s5_params(s5_lambda_re[l], s5_lambda_im[l], s5_log_dt[l], s5_b_re[l],
                                    s5_b_im[l], s5_c_re[l], s5_c_im[l])
        ssm = _s5(u.reshape(bsz, t, SSM_WIDTH), bm, cm, ar, ai,
                  s5_d[l].reshape(1, SSM_WIDTH), s5_w_glu[l].astype(bf), s5_b_glu[l][None])

        xc = _out_ffn2(x1, attn_t, ssm.reshape(n, SSM_WIDTH), attn_out_norm[l][None],
                       ssm_out_norm[l][None], w_out[l].astype(bf), ffn2_norm[l][None],
                       ffn2_w_gate[l].astype(bf), ffn2_w_up[l].astype(bf), ffn2_w_down[l].astype(bf),
                       final_norm[None], bsz, t)
    return xc.reshape(bsz, t, D_MODEL)
```

```python
import math

import jax
import jax.numpy as jnp
from jax import lax
from jax.experimental import pallas as pl
from jax.experimental.pallas import tpu as pltpu

D_MODEL = 1024
HEAD_DIM = 64
N_HEADS = 8
N_KV = 2
GQA_R = N_HEADS // N_KV
ATTN_WIDTH = N_HEADS * HEAD_DIM
CMP_LEN = 32
CMP_STRIDE = 16
CMP_HIDDEN = 256
SEL_BLOCK = 64
SEL_SHIFT = 6
SEL_TOPK = 16
WINDOW = 512
Q_BLOCK = 256
SSM_WIDTH = 512
SSM_GROUP = 16
SSM_GROUPS = SSM_WIDTH // SSM_GROUP
SSM_STATE = 64
D_FF = 2816
EPS = 1e-6

LANES = 128
SUBLANES = 8
FF_CHUNK = 256
TOKEN_TILE = 512
SEL_KEY_TILE = 256
S5_TIME_CHUNK = 128
S5_SCAN_PARTS = 2
PV_ROWS = HEAD_DIM + 16
TOPK_CHUNK = 16
S5_HALF = SSM_WIDTH // 2
S5_HALF_STATE = (SSM_GROUPS // 2) * SSM_STATE
VMEM_LIMIT = 56 * 1024 * 1024
NEG_BIG = -1e30
SCORE_DTYPE = jnp.bfloat16
Q_SCALE = HEAD_DIM ** -0.5 * math.log2(math.e)

Q_OFF = 0
KVC_OFF = 512
KVS_OFF = 768
KVW_OFF = 1024
GATE_OFF = 1280
U_OFF = 1536
IN_COLS_PADDED = 2048
N_GATE_COLS = N_HEADS * 3


def _dot(a, b):
    return jnp.dot(a, b, preferred_element_type=jnp.float32)


def _rmsnorm(x, g):
    r = lax.rsqrt(jnp.mean(x * x, axis=-1, keepdims=True) + EPS)
    return (x * r) * g


def _swiglu_acc(h, wg_ref, wu_ref, wd_ref):
    acc = jnp.zeros((h.shape[0], D_MODEL), jnp.float32)
    for f in range(0, D_FF, FF_CHUNK):
        gate = _dot(h, wg_ref[:, f:f + FF_CHUNK])
        up = _dot(h, wu_ref[:, f:f + FF_CHUNK])
        act = (gate * jax.nn.sigmoid(gate) * up).astype(jnp.bfloat16)
        acc = acc + _dot(act, wd_ref[f:f + FF_CHUNK, :])
    return acc


def _ones_row_pad(nrow, ncol, dtype):
    return jnp.where(lax.broadcasted_iota(jnp.int32, (nrow, ncol), 0) == 0, 1.0, 0.0).astype(dtype)


def _resident(shape):
    nd = len(shape)
    return pl.BlockSpec(shape, lambda *_: (0,) * nd, pipeline_mode=pl.Buffered(1))


def _ffn1_proj_kernel(x_ref, g1_ref, wg_ref, wu_ref, wd_ref, gm_ref, win_ref, gb_ref,
                      x1_ref, q_ref, kvc_ref, ks_ref, vst_ref, kw_ref, vwt_ref, gate_ref, u_ref):
    x = x_ref[...]
    h = _rmsnorm(x, g1_ref[...]).astype(jnp.bfloat16)
    x1 = x + 0.5 * _swiglu_acc(h, wg_ref, wu_ref, wd_ref)
    x1_ref[...] = x1
    h2 = _rmsnorm(x1, gm_ref[...]).astype(jnp.bfloat16)
    p = _dot(h2, win_ref[...])
    for hh in range(N_HEADS):
        lo = Q_OFF + hh * HEAD_DIM
        q_ref[hh, 0] = (p[:, lo:lo + HEAD_DIM] * Q_SCALE).T.astype(q_ref.dtype)
    kv_cols = N_KV * HEAD_DIM
    for kind in range(2):
        kvc_ref[kind] = p[:, KVC_OFF + kind * kv_cols:KVC_OFF + (kind + 1) * kv_cols]
    for g in range(N_KV):
        for off, k_ref, vt_ref in ((KVS_OFF, ks_ref, vst_ref), (KVW_OFF, kw_ref, vwt_ref)):
            k_lo = off + g * HEAD_DIM
            v_lo = off + (N_KV + g) * HEAD_DIM
            k_ref[g] = p[:, k_lo:k_lo + HEAD_DIM].astype(k_ref.dtype)
            vt_ref[g, 0, :HEAD_DIM] = p[:, v_lo:v_lo + HEAD_DIM].T.astype(vt_ref.dtype)
            vt_ref[g, 0, HEAD_DIM:] = _ones_row_pad(PV_ROWS - HEAD_DIM, p.shape[0], vt_ref.dtype)
    for g in range(N_KV):
        lo = GATE_OFF + g * LANES
        gate = jax.nn.sigmoid(p[:, lo:lo + LANES] + gb_ref[:, g * LANES:(g + 1) * LANES])
        gate_ref[g, 0] = gate.T
    u_ref[...] = p[:, U_OFF:U_OFF + SSM_WIDTH]


def _ffn1_proj(x2d, g1, wg, wu, wd, gm, win, gb, bsz, t):
    n = bsz * t
    tm = min(TOKEN_TILE, t)
    nt = t // tm
    row = lambda i: (i, 0)
    lead = lambda i: (0, i, 0)
    k_spec = pl.BlockSpec((N_KV, tm, HEAD_DIM), lead)
    vt_spec = lambda nrow: pl.BlockSpec((N_KV, 1, nrow, tm), lambda i: (0, i // nt, 0, i % nt))
    k_shape = jax.ShapeDtypeStruct((N_KV, n, HEAD_DIM), jnp.bfloat16)
    vt_shape = lambda nrow: jax.ShapeDtypeStruct((N_KV, bsz, nrow, t), jnp.bfloat16)
    return pl.pallas_call(
        _ffn1_proj_kernel,
        grid=(n // tm,),
        in_specs=[
            pl.BlockSpec((tm, D_MODEL), row),
            _resident((1, D_MODEL)),
            _resident((D_MODEL, D_FF)), _resident((D_MODEL, D_FF)), _resident((D_FF, D_MODEL)),
            _resident((1, D_MODEL)),
            _resident((D_MODEL, IN_COLS_PADDED)),
            _resident((1, N_KV * LANES)),
        ],
        out_specs=[
            pl.BlockSpec((tm, D_MODEL), row),
            pl.BlockSpec((N_HEADS, 1, HEAD_DIM, tm), lambda i: (0, i // nt, 0, i % nt)),
            pl.BlockSpec((2, tm, N_KV * HEAD_DIM), lead),
            k_spec, vt_spec(PV_ROWS), k_spec, vt_spec(PV_ROWS),
            pl.BlockSpec((N_KV, 1, LANES, tm), lambda i: (0, i // nt, 0, i % nt)),
            pl.BlockSpec((tm, SSM_WIDTH), row),
        ],
        out_shape=[
            jax.ShapeDtypeStruct((n, D_MODEL), jnp.float32),
            jax.ShapeDtypeStruct((N_HEADS, bsz, HEAD_DIM, t), jnp.bfloat16),
            jax.ShapeDtypeStruct((2, n, N_KV * HEAD_DIM), jnp.float32),
            k_shape, vt_shape(PV_ROWS), k_shape, vt_shape(PV_ROWS),
            jax.ShapeDtypeStruct((N_KV, bsz, LANES, t), jnp.float32),
            jax.ShapeDtypeStruct((n, SSM_WIDTH), jnp.float32),
        ],
        compiler_params=pltpu.CompilerParams(
            dimension_semantics=("parallel",), vmem_limit_bytes=VMEM_LIMIT),
    )(x2d, g1, wg, wu, wd, gm, win, gb)


def _compress_kernel(x_ref, pos_ref, w1_ref, b1_ref, w2_ref, o_ref, ot_ref):
    nrow = x_ref.shape[1] // CMP_STRIDE
    tops = [jnp.zeros((nrow, CMP_HIDDEN), jnp.float32) for _ in range(N_KV)]
    bots = [jnp.zeros((nrow, CMP_HIDDEN), jnp.float32) for _ in range(N_KV)]
    for l in range(CMP_STRIDE):
        xl = x_ref[0, pl.ds(l, nrow, stride=CMP_STRIDE), :]
        w_top = w1_ref[0, l * HEAD_DIM:(l + 1) * HEAD_DIM, :]
        w_bot = w1_ref[0, (CMP_STRIDE + l) * HEAD_DIM:(CMP_STRIDE + l + 1) * HEAD_DIM, :]
        for g in range(N_KV):
            xg = xl[:, g * HEAD_DIM:(g + 1) * HEAD_DIM]
            tops[g] = tops[g] + _dot((xg + pos_ref[0, l:l + 1, :]).astype(jnp.bfloat16), w_top)
            bots[g] = bots[g] + _dot((xg + pos_ref[0, CMP_STRIDE + l:CMP_STRIDE + l + 1, :])
                                     .astype(jnp.bfloat16), w_bot)
    for g in range(N_KV):
        pre = tops[g] + pltpu.roll(bots[g], nrow - 1, 0) + b1_ref[0]
        out = _dot(jax.nn.gelu(pre).astype(jnp.bfloat16), w2_ref[0])
        rid = lax.broadcasted_iota(jnp.int32, out.shape, 0)
        out = jnp.where(rid < nrow - 1, out, 0.0)
        o_ref[g, 0] = out
        ot_ref[g, 0, :HEAD_DIM] = out.T
        ot_ref[g, 0, HEAD_DIM:] = _ones_row_pad(PV_ROWS - HEAD_DIM, nrow, ot_ref.dtype)


def _compress(kvc, pos, w1, b1, w2, bsz):
    t = kvc.shape[1] // bsz
    nrow = t // CMP_STRIDE
    kind = lambda k, b: (k, 0, 0)
    return pl.pallas_call(
        _compress_kernel,
        grid=(2, bsz),
        in_specs=[
            pl.BlockSpec((1, t, N_KV * HEAD_DIM), lambda k, b: (k, b, 0)),
            pl.BlockSpec((1, CMP_LEN, HEAD_DIM), kind),
            pl.BlockSpec((1, CMP_LEN * HEAD_DIM, CMP_HIDDEN), kind),
            pl.BlockSpec((1, 1, CMP_HIDDEN), kind),
            pl.BlockSpec((1, CMP_HIDDEN, HEAD_DIM), kind),
        ],
        out_specs=[pl.BlockSpec((N_KV, 1, nrow, HEAD_DIM), lambda k, b: (k, b, 0, 0)),
                   pl.BlockSpec((N_KV, 1, PV_ROWS, nrow), lambda k, b: (k, b, 0, 0))],
        out_shape=[jax.ShapeDtypeStruct((2 * N_KV, bsz, nrow, HEAD_DIM), jnp.float32),
                   jax.ShapeDtypeStruct((2 * N_KV, bsz, PV_ROWS, nrow), jnp.float32)],
        compiler_params=pltpu.CompilerParams(dimension_semantics=("parallel", "parallel")),
    )(kvc, pos, w1, b1, w2)


def _split3_bf16(x):
    hi = x.astype(jnp.bfloat16)
    r1 = x - hi.astype(jnp.float32)
    mid = r1.astype(jnp.bfloat16)
    lo = (r1 - mid.astype(jnp.float32)).astype(jnp.bfloat16)
    return hi, mid, lo


def _q_t(q_ref, g):
    return jnp.concatenate([q_ref[g * GQA_R + r, 0] for r in range(GQA_R)], axis=1)


def _sweep_start(q_ref, k_ref, tile, s_ref, p_ref, acc_ref):
    p_ref[:, 1] = jnp.zeros_like(p_ref[:, 1])
    acc_ref[...] = jnp.zeros_like(acc_ref)
    for g in range(N_KV):
        s_ref[g, 0] = _dot(k_ref[g, 0:tile, :], _q_t(q_ref, g)).astype(SCORE_DTYPE)


def _sweep_t(q_ref, k_ref, vt_ref, n_tiles, tile, bias_fn, s_ref, p_ref, acc_ref):
    rows = acc_ref.shape[2]
    last_tile = k_ref.shape[1] // tile - 1

    def k_tile(g, i):
        return k_ref[g, pl.ds(pl.multiple_of(i * tile, tile), tile), :]

    def vt_tile(g, i):
        return vt_ref[g, 0, :, pl.ds(pl.multiple_of(i * tile, tile), tile)]

    def add_bias(s, g, i, slot):
        bias = bias_fn(g, i).astype(s.dtype)
        s = jnp.concatenate([s[:, r * Q_BLOCK:(r + 1) * Q_BLOCK] + bias
                             for r in range(rows // Q_BLOCK)], axis=1)
        s_ref[g, slot] = s
        return jnp.max(s, axis=0, keepdims=True).astype(jnp.float32)

    def phase(i, carry, g, slot):
        m, s_max = carry
        m_new = jnp.maximum(m, s_max)
        p_ref[g, slot] = jnp.exp2(s_ref[g, slot] - m_new.astype(SCORE_DTYPE)).astype(p_ref.dtype)
        pv = _dot(vt_tile(g, jnp.maximum(i - 1, 0)), p_ref[g, 1 - slot])
        nxt = jnp.minimum(i + 1, last_tile)
        s_max_next = add_bias(_dot(k_tile(g, nxt), _q_t(q_ref, g)).astype(SCORE_DTYPE), g, nxt, 1 - slot)
        acc_ref[g] = jnp.exp2(m - m_new) * (acc_ref[g] + pv)
        return m_new, s_max_next

    def body(j, carry):
        for slot in range(2):
            carry = tuple(phase(2 * j + slot, carry[g], g, slot) for g in range(N_KV))
        return carry

    init = tuple((jnp.full((1, rows), NEG_BIG, jnp.float32), add_bias(s_ref[g, 0], g, 0, 0))
                 for g in range(N_KV))
    n_pairs = n_tiles // 2
    carry = lax.fori_loop(0, n_pairs, body, init)
    for g in range(N_KV):
        acc_ref[g] = acc_ref[g] + _dot(vt_tile(g, jnp.maximum(2 * n_pairs - 1, 0)), p_ref[g, 1])

    @pl.when(n_tiles % 2 == 1)
    def _():
        for g in range(N_KV):
            m, s_max = carry[g]
            m_new = jnp.maximum(m, s_max)
            p = jnp.exp2(s_ref[g, 0] - m_new.astype(SCORE_DTYPE)).astype(p_ref.dtype)
            acc_ref[g] = jnp.exp2(m - m_new) * acc_ref[g] + _dot(vt_tile(g, n_tiles - 1), p)

    return [acc_ref[g, :HEAD_DIM] / acc_ref[g, HEAD_DIM:HEAD_DIM + 1] for g in range(N_KV)]


def _nsa_kernel(q_ref, kc_ref, vct_ref, ks_ref, vst_ref, kw_ref, vwt_ref, gate_ref, o_ref,
                bias_ref, rank_ref, causal_ref, s_ref, p_ref, acc_ref):
    c = pl.program_id(1)
    rows = GQA_R * Q_BLOCK
    heads = range(N_KV)
    ncp = kc_ref.shape[2]
    t = ks_ref.shape[1]
    nb = t // SEL_BLOCK

    q_t = [_q_t(q_ref, g) for g in heads]
    s_c = [_dot(kc_ref[g, 0].astype(jnp.bfloat16), q_t[g]) for g in heads]
    band = WINDOW + Q_BLOCK
    w0 = pl.multiple_of(jnp.maximum(c * Q_BLOCK - WINDOW, 0), Q_BLOCK)
    s_w = [_dot(kw_ref[g, pl.ds(w0, band), :], q_t[g]) for g in heads]
    _sweep_start(q_ref, ks_ref, SEL_KEY_TILE, s_ref, p_ref, acc_ref)

    n_c = lax.broadcasted_iota(jnp.int32, (ncp, Q_BLOCK), 0)
    tq_c = c * Q_BLOCK + lax.broadcasted_iota(jnp.int32, (ncp, Q_BLOCK), 1)
    bias_c = jnp.where(n_c * CMP_STRIDE + (CMP_LEN - 1) <= tq_c, 0.0, NEG_BIG)
    tq_row = c * Q_BLOCK + (lax.broadcasted_iota(jnp.int32, (1, rows), 1) & (Q_BLOCK - 1))
    p_c, o_cmp = [], []
    for g in heads:
        s = jnp.concatenate([s_c[g][:, r * Q_BLOCK:(r + 1) * Q_BLOCK] + bias_c for r in range(GQA_R)], axis=1)
        e_c = jnp.exp2(s - jnp.max(s, axis=0, keepdims=True))
        ev_c = _dot(vct_ref[g, 0].astype(jnp.bfloat16), e_c.astype(jnp.bfloat16))
        inv_c = jnp.where(tq_row >= CMP_LEN - 1, 1.0 / ev_c[HEAD_DIM:HEAD_DIM + 1], 0.0)
        p_c.append(e_c * inv_c)
        o_cmp.append(ev_c[:HEAD_DIM] * inv_c)

    kpos_w = w0 + lax.broadcasted_iota(jnp.int32, (band, Q_BLOCK), 0)
    tq_w = c * Q_BLOCK + lax.broadcasted_iota(jnp.int32, (band, Q_BLOCK), 1)
    bias_w = jnp.where((kpos_w <= tq_w) & (kpos_w > tq_w - WINDOW), 0.0, NEG_BIG).astype(SCORE_DTYPE)
    o_win = []
    for g in heads:
        s = s_w[g].astype(SCORE_DTYPE)
        s = jnp.concatenate([s[:, r * Q_BLOCK:(r + 1) * Q_BLOCK] + bias_w for r in range(GQA_R)], axis=1)
        p_w = jnp.exp2(s - jnp.max(s, axis=0, keepdims=True))
        pv_w = _dot(vwt_ref[g, 0, :, pl.ds(w0, band)], p_w.astype(jnp.bfloat16))
        o_win.append(pv_w[:HEAD_DIM] / pv_w[HEAD_DIM:HEAD_DIM + 1])

    jb = lax.broadcasted_iota(jnp.int32, (nb, ncp), 0)
    nn = lax.broadcasted_iota(jnp.int32, (nb, ncp), 1)
    overlap_t = ((nn * CMP_STRIDE < jb * SEL_BLOCK + SEL_BLOCK)
                 & (nn * CMP_STRIDE + CMP_LEN > jb * SEL_BLOCK)
                 & (nn < ncp - 1))
    overlap_t = jnp.where(overlap_t, 1.0, 0.0).astype(jnp.bfloat16)
    j_t = lax.broadcasted_iota(jnp.int32, (nb, Q_BLOCK), 0)
    tq_t = c * Q_BLOCK + lax.broadcasted_iota(jnp.int32, (nb, Q_BLOCK), 1)
    cur = tq_t >> SEL_SHIFT
    forced = (j_t == 0) | (j_t == cur) | (j_t == cur - 1)
    valid = j_t * SEL_BLOCK <= tq_t
    score = []
    for g in heads:
        p_sum = p_c[g][:, 0:Q_BLOCK]
        for r in range(1, GQA_R):
            p_sum = p_sum + p_c[g][:, r * Q_BLOCK:(r + 1) * Q_BLOCK]
        hi, mid, lo = _split3_bf16(p_sum)
        imp = _dot(overlap_t, hi) + _dot(overlap_t, mid) + _dot(overlap_t, lo)
        score.append(jnp.where(forced, jnp.inf, jnp.where(valid, imp, -jnp.inf)))

    rank_ref[...] = jnp.zeros_like(rank_ref)
    j_grp = lax.broadcasted_iota(jnp.int32, (SUBLANES, LANES), 0)

    def count_chunk(g, cand, ranked):
        groups = [slice(v, v + SUBLANES) for v in range(ranked * TOPK_CHUNK, (ranked + 1) * TOPK_CHUNK, SUBLANES)]
        for lt in range(0, Q_BLOCK, LANES):
            sc = score[g][:, lt:lt + LANES]
            ranks = [rank_ref[g, grp, lt:lt + LANES] for grp in groups]
            for jp in range(cand * TOPK_CHUNK, (cand + 1) * TOPK_CHUNK):
                row = jnp.broadcast_to(sc[jp:jp + 1, :], (SUBLANES, LANES))
                for v, grp in enumerate(groups):
                    if jp < grp.start:
                        beats = row >= sc[grp]
                    elif jp >= grp.stop:
                        beats = row > sc[grp]
                    else:
                        beats = (row > sc[grp]) | ((row == sc[grp]) & (j_grp > jp - grp.start))
                    ranks[v] = ranks[v] + jnp.where(beats, 1.0, 0.0)
            for v, grp in enumerate(groups):
                rank_ref[g, grp, lt:lt + LANES] = ranks[v]

    blocks_seen = (c + 1) * (Q_BLOCK // SEL_BLOCK)
    for shell in range(nb // TOPK_CHUNK):
        @pl.when(blocks_seen > shell * TOPK_CHUNK)
        def _():
            for g in heads:
                for other in range(shell):
                    count_chunk(g, shell, other)
                    count_chunk(g, other, shell)
                count_chunk(g, shell, shell)

    bias_ref[...] = jnp.where(rank_ref[...] < min(SEL_TOPK, nb), 0.0, NEG_BIG)

    tiles_per_q = Q_BLOCK // SEL_KEY_TILE

    @pl.when(c == 0)
    def _():
        kpos = lax.broadcasted_iota(jnp.int32, (SEL_KEY_TILE, Q_BLOCK), 0)
        qpos = lax.broadcasted_iota(jnp.int32, (SEL_KEY_TILE, Q_BLOCK), 1)
        causal_ref[0] = jnp.zeros((SEL_KEY_TILE, Q_BLOCK), jnp.float32)
        for j in range(tiles_per_q):
            causal_ref[1 + j] = jnp.where(kpos + j * SEL_KEY_TILE <= qpos, 0.0, NEG_BIG)
        causal_ref[1 + tiles_per_q] = jnp.full((SEL_KEY_TILE, Q_BLOCK), NEG_BIG, jnp.float32)

    blocks_per_tile = SEL_KEY_TILE // SEL_BLOCK

    def sel_bias(g, kt):
        blocks = [jnp.broadcast_to(bias_ref[g, pl.ds(kt * blocks_per_tile + jj, 1), :], (SEL_BLOCK, Q_BLOCK))
                  for jj in range(blocks_per_tile)]
        pattern = jnp.clip(kt - c * tiles_per_q + 1, 0, tiles_per_q + 1)
        return jnp.concatenate(blocks, axis=0) + causal_ref[pattern]

    def out_rows(g, r):
        return slice((g * GQA_R + r) * HEAD_DIM, (g * GQA_R + r + 1) * HEAD_DIM)

    for g in heads:
        for r in range(GQA_R):
            sl = slice(r * Q_BLOCK, (r + 1) * Q_BLOCK)
            o_ref[0, out_rows(g, r), :] = (gate_ref[g, 0, 3 * r:3 * r + 1, :] * o_cmp[g][:, sl]
                                           + gate_ref[g, 0, 3 * r + 2:3 * r + 3, :] * o_win[g][:, sl])

    o_slc = _sweep_t(q_ref, ks_ref, vst_ref, (c + 1) * tiles_per_q, SEL_KEY_TILE, sel_bias,
                     s_ref, p_ref, acc_ref)
    for g in heads:
        for r in range(GQA_R):
            sl = slice(r * Q_BLOCK, (r + 1) * Q_BLOCK)
            o_ref[0, out_rows(g, r), :] += gate_ref[g, 0, 3 * r + 1:3 * r + 2, :] * o_slc[g][:, sl]


def _nsa(q, kcmp, kcmp_t, ks, vst, kw, vwt, gates, bsz, t):
    nq = t // Q_BLOCK
    ncp = kcmp.shape[2]
    rows = GQA_R * Q_BLOCK
    nb = t // SEL_BLOCK
    assert t % (2 * SEL_KEY_TILE) == 0, "the selected-block sweep runs pairs of key tiles"
    assert Q_BLOCK % SEL_KEY_TILE == 0, "the causal bias of a key tile is one of a few fixed patterns"
    assert nb % TOPK_CHUNK == 0
    k_spec = pl.BlockSpec((N_KV, t, HEAD_DIM), lambda b, c: (0, b, 0))
    vt_spec = pl.BlockSpec((N_KV, 1, PV_ROWS, t), lambda b, c: (0, b, 0, 0))
    return pl.pallas_call(
        _nsa_kernel,
        grid=(bsz, nq),
        in_specs=[
            pl.BlockSpec((N_HEADS, 1, HEAD_DIM, Q_BLOCK), lambda b, c: (0, b, 0, c)),
            pl.BlockSpec((N_KV, 1, ncp, HEAD_DIM), lambda b, c: (0, b, 0, 0)),
            pl.BlockSpec((N_KV, 1, PV_ROWS, ncp), lambda b, c: (1, b, 0, 0)),
            k_spec, vt_spec, k_spec, vt_spec,
            pl.BlockSpec((N_KV, 1, LANES, Q_BLOCK), lambda b, c: (0, b, 0, c)),
        ],
        out_specs=pl.BlockSpec((1, ATTN_WIDTH, Q_BLOCK), lambda b, c: (b, 0, c)),
        out_shape=jax.ShapeDtypeStruct((bsz, ATTN_WIDTH, t), jnp.float32),
        scratch_shapes=[pltpu.VMEM((N_KV, nb, Q_BLOCK), jnp.float32),
                        pltpu.VMEM((N_KV, nb, Q_BLOCK), jnp.float32),
                        pltpu.VMEM((Q_BLOCK // SEL_KEY_TILE + 2, SEL_KEY_TILE, Q_BLOCK), jnp.float32),
                        pltpu.VMEM((N_KV, 2, SEL_KEY_TILE, rows), SCORE_DTYPE),
                        pltpu.VMEM((N_KV, 2, SEL_KEY_TILE, rows), jnp.bfloat16),
                        pltpu.VMEM((N_KV, PV_ROWS, rows), jnp.float32)],
        compiler_params=pltpu.CompilerParams(
            dimension_semantics=("parallel", "arbitrary"),
            vmem_limit_bytes=VMEM_LIMIT),
    )(q, kcmp, kcmp_t, ks, vst, kw, vwt, gates)


def _s5_kernel(u_ref, bm_ref, cm_ref, ar_ref, ai_ref, d_ref, wglu_ref, bglu_ref, o_ref,
               utb_ref, bu_ref, state_ref):
    bsz, steps, _ = u_ref.shape

    @pl.when(pl.program_id(0) == 0)
    def _():
        state_ref[...] = jnp.zeros_like(state_ref)

    n_lane_tiles = utb_ref.shape[0]
    for b in range(bsz):
        for j in range(n_lane_tiles):
            utb_ref[j, pl.ds(b, steps, stride=bsz), :] = u_ref[b, :, j * LANES:(j + 1) * LANES]
    u = jnp.concatenate([utb_ref[j] for j in range(n_lane_tiles)], axis=1)
    ub = u.astype(jnp.bfloat16)
    re = slice(0, S5_HALF_STATE)
    im = slice(S5_HALF_STATE, 2 * S5_HALF_STATE)
    part_steps = steps // S5_SCAN_PARTS
    part_rows = [slice(j * part_steps * bsz, (j + 1) * part_steps * bsz) for j in range(S5_SCAN_PARTS)]

    def scan_part(half, j, carry):
        a_re = jnp.broadcast_to(ar_ref[half], (bsz, S5_HALF_STATE))
        a_im = jnp.broadcast_to(ai_ref[half], (bsz, S5_HALF_STATE))
        xr, xi = carry
        for i in range(j * part_steps, (j + 1) * part_steps):
            rows = slice(i * bsz, (i + 1) * bsz)
            xr, xi = (a_re * xr - a_im * xi + bu_ref[half, rows, re],
                      a_re * xi + a_im * xr + bu_ref[half, rows, im])
            bu_ref[half, rows, re] = xr
            bu_ref[half, rows, im] = xi
        return xr, xi

    bu_ref[0] = _dot(ub[:, :S5_HALF], bm_ref[0])
    carry = state_ref[0, :, re], state_ref[0, :, im]
    for j in range(S5_SCAN_PARTS):
        carry = scan_part(0, j, carry)
        bu_ref[1, part_rows[j]] = _dot(ub[part_rows[j], S5_HALF:], bm_ref[1])
    state_ref[0, :, re], state_ref[0, :, im] = carry
    carry = state_ref[1, :, re], state_ref[1, :, im]
    y0 = []
    for j in range(S5_SCAN_PARTS):
        carry = scan_part(1, j, carry)
        y0.append(_dot(bu_ref[0, part_rows[j]].astype(jnp.bfloat16), cm_ref[0]))
    state_ref[1, :, re], state_ref[1, :, im] = carry
    y1 = _dot(bu_ref[1].astype(jnp.bfloat16), cm_ref[1])
    y = jnp.concatenate([jnp.concatenate(y0, axis=0), y1], axis=-1) + d_ref[...] * u
    y = jax.nn.gelu(y)
    out = y * jax.nn.sigmoid(_dot(y.astype(jnp.bfloat16), wglu_ref[...]) + bglu_ref[...])
    for j in range(n_lane_tiles):
        utb_ref[j] = out[:, j * LANES:(j + 1) * LANES]
    for b in range(bsz):
        for j in range(n_lane_tiles):
            o_ref[b, :, j * LANES:(j + 1) * LANES] = utb_ref[j, pl.ds(b, steps, stride=bsz), :]


def _s5(u, bm, cm, ar, ai, d, wglu, bglu):
    bsz, t, _ = u.shape
    rows = S5_TIME_CHUNK * bsz
    const3 = lambda i: (0, 0, 0)
    const2 = lambda i: (0, 0)
    io_spec = pl.BlockSpec((bsz, S5_TIME_CHUNK, SSM_WIDTH), lambda i: (0, i, 0))
    return pl.pallas_call(
        _s5_kernel,
        grid=(t // S5_TIME_CHUNK,),
        in_specs=[
            io_spec,
            pl.BlockSpec((2, S5_HALF, 2 * S5_HALF_STATE), const3),
            pl.BlockSpec((2, 2 * S5_HALF_STATE, S5_HALF), const3),
            pl.BlockSpec((2, 1, S5_HALF_STATE), const3),
            pl.BlockSpec((2, 1, S5_HALF_STATE), const3),
            pl.BlockSpec((1, SSM_WIDTH), const2),
            pl.BlockSpec((SSM_WIDTH, SSM_WIDTH), const2),
            pl.BlockSpec((1, SSM_WIDTH), const2),
        ],
        out_specs=io_spec,
        out_shape=jax.ShapeDtypeStruct((bsz, t, SSM_WIDTH), jnp.float32),
        scratch_shapes=[
            pltpu.VMEM((SSM_WIDTH // LANES, rows, LANES), jnp.float32),
            pltpu.VMEM((2, rows, 2 * S5_HALF_STATE), jnp.float32),
            pltpu.VMEM((2, bsz, 2 * S5_HALF_STATE), jnp.float32),
        ],
        compiler_params=pltpu.CompilerParams(
            dimension_semantics=("arbitrary",), vmem_limit_bytes=VMEM_LIMIT),
    )(u, bm, cm, ar, ai, d, wglu, bglu)


def _s5_params(lam_re, lam_im, log_dt, b_re, b_im, c_re, c_im):
    dt = jnp.exp(log_dt)[:, None]
    mag = jnp.exp(lam_re * dt)
    ab_re = mag * jnp.cos(lam_im * dt)
    ab_im = mag * jnp.sin(lam_im * dt)
    nr = ab_re - 1.0
    den = lam_re * lam_re + lam_im * lam_im
    f_re = (nr * lam_re + ab_im * lam_im) / den
    f_im = (ab_im * lam_re - nr * lam_im) / den
    bb_re = f_re[..., None] * b_re - f_im[..., None] * b_im
    bb_im = f_re[..., None] * b_im + f_im[..., None] * b_re
    gh = SSM_GROUPS // 2
    eye = jnp.eye(gh, dtype=jnp.float32)

    def in_half(bb):
        return jnp.einsum('gph,gk->ghkp', bb, eye).reshape(S5_HALF, S5_HALF_STATE)

    def out_half(cc):
        return jnp.einsum('ghp,gk->gpkh', cc, eye).reshape(S5_HALF_STATE, S5_HALF)

    bm = jnp.stack([jnp.concatenate([in_half(bb_re[s]), in_half(bb_im[s])], axis=1)
                    for s in (slice(0, gh), slice(gh, None))])
    cm = jnp.stack([jnp.concatenate([out_half(c_re[s]), -out_half(c_im[s])], axis=0)
                    for s in (slice(0, gh), slice(gh, None))])
    ar = ab_re.reshape(2, 1, S5_HALF_STATE)
    ai = ab_im.reshape(2, 1, S5_HALF_STATE)
    return bm.astype(jnp.bfloat16), cm.astype(jnp.bfloat16), ar, ai


def _out_ffn2_kernel(x1_ref, a_ref, s_ref, ga_ref, gs_ref, wo_ref, g2_ref, wg_ref, wu_ref, wd_ref,
                     gf_ref, o_ref):
    a_t = a_ref[0]
    r_a = lax.rsqrt(jnp.mean(a_t * a_t, axis=0, keepdims=True) + EPS)
    a = ((a_t * r_a).T * ga_ref[...]).astype(jnp.bfloat16)
    s = _rmsnorm(s_ref[...], gs_ref[...]).astype(jnp.bfloat16)
    x2 = x1_ref[...] + _dot(a, wo_ref[:ATTN_WIDTH, :]) + _dot(s, wo_ref[ATTN_WIDTH:, :])
    h = _rmsnorm(x2, g2_ref[...]).astype(jnp.bfloat16)
    x3 = x2 + 0.5 * _swiglu_acc(h, wg_ref, wu_ref, wd_ref)
    o_ref[...] = _rmsnorm(x3, gf_ref[...])


def _out_ffn2(x1, attn_t, ssm, ga, gs, wo, g2, wg, wu, wd, gf, bsz, t):
    n = bsz * t
    tm = min(TOKEN_TILE, t)
    nt = t // tm
    row = lambda i: (i, 0)
    return pl.pallas_call(
        _out_ffn2_kernel,
        grid=(n // tm,),
        in_specs=[
            pl.BlockSpec((tm, D_MODEL), row),
            pl.BlockSpec((1, ATTN_WIDTH, tm), lambda i: (i // nt, 0, i % nt)),
            pl.BlockSpec((tm, SSM_WIDTH), row),
            _resident((1, ATTN_WIDTH)), _resident((1, SSM_WIDTH)),
            _resident((D_MODEL, D_MODEL)),
            _resident((1, D_MODEL)),
            _resident((D_MODEL, D_FF)), _resident((D_MODEL, D_FF)), _resident((D_FF, D_MODEL)),
            _resident((1, D_MODEL)),
        ],
        out_specs=pl.BlockSpec((tm, D_MODEL), row),
        out_shape=jax.ShapeDtypeStruct((n, D_MODEL), jnp.float32),
        compiler_params=pltpu.CompilerParams(
            dimension_semantics=("parallel",), vmem_limit_bytes=VMEM_LIMIT),
    )(x1, attn_t, ssm, ga, gs, wo, g2, wg, wu, wd, gf)


def _reorder_w_in(w_in, gate_bias):
    per_g = N_GATE_COLS // N_KV
    g_lo = GATE_OFF
    pieces = [w_in[:, :g_lo]]
    bias = []
    gb = gate_bias.reshape(1, N_GATE_COLS)
    for g in range(N_KV):
        cols = w_in[:, g_lo + g * per_g:g_lo + (g + 1) * per_g]
        pieces.append(jnp.pad(cols, ((0, 0), (0, LANES - per_g))))
        bias.append(jnp.pad(gb[:, g * per_g:(g + 1) * per_g], ((0, 0), (0, LANES - per_g))))
    pieces.append(w_in[:, g_lo + N_GATE_COLS:])
    return jnp.concatenate(pieces, axis=1).astype(jnp.bfloat16), jnp.concatenate(bias, axis=1)


def kernel(x, ffn1_norm, ffn1_w_gate, ffn1_w_up, ffn1_w_down, mix_norm, w_in, gate_bias, cmp_pos_k, cmp_pos_v, cmp_k_w1, cmp_k_b1, cmp_k_w2, cmp_v_w1, cmp_v_b1, cmp_v_w2, s5_lambda_re, s5_lambda_im, s5_log_dt, s5_b_re, s5_b_im, s5_c_re, s5_c_im, s5_d, s5_w_glu, s5_b_glu, attn_out_norm, ssm_out_norm, w_out, ffn2_norm, ffn2_w_gate, ffn2_w_up, ffn2_w_down, final_norm):
    bsz, t, _ = x.shape
    n = bsz * t
    bf = jnp.bfloat16
    assert ffn1_norm.shape[0] == 1, "the final rmsnorm is fused into the single layer's last kernel"
    assert t % max(TOKEN_TILE, WINDOW) == 0 and t >= WINDOW + Q_BLOCK and x.shape[2] == D_MODEL
    xc = x.reshape(n, D_MODEL)
    for l in range(1):
        win, gb = _reorder_w_in(w_in[l], gate_bias[l])
        x1, q, kvc, ks, vst, kw, vwt, gates, u = _ffn1_proj(
            xc, ffn1_norm[l][None], ffn1_w_gate[l].astype(bf), ffn1_w_up[l].astype(bf),
            ffn1_w_down[l].astype(bf), mix_norm[l][None], win, gb, bsz, t)

        pos = jnp.stack([cmp_pos_k[l], cmp_pos_v[l]])
        w1 = jnp.stack([cmp_k_w1[l], cmp_v_w1[l]]).astype(bf)
        b1 = jnp.stack([cmp_k_b1[l], cmp_v_b1[l]])[:, None, :]
        w2 = jnp.stack([cmp_k_w2[l], cmp_v_w2[l]]).astype(bf)
        kcmp, kcmp_t = _compress(kvc, pos, w1, b1, w2, bsz)

        attn_t = _nsa(q, kcmp, kcmp_t, ks, vst, kw, vwt, gates, bsz, t)

        bm, cm, ar, ai = _s5_params(s5_lambda_re[l], s5_lambda_im[l], s5_log_dt[l], s5_b_re[l],
                                    s5_b_im[l], s5_c_re[l], s5_c_im[l])
        ssm = _s5(u.reshape(bsz, t, SSM_WIDTH), bm, cm, ar, ai,
                  s5_d[l].reshape(1, SSM_WIDTH), s5_w_glu[l].astype(bf), s5_b_glu[l][None])

        xc = _out_ffn2(x1, attn_t, ssm.reshape(n, SSM_WIDTH), attn_out_norm[l][None],
                       ssm_out_norm[l][None], w_out[l].astype(bf), ffn2_norm[l][None],
                       ffn2_w_gate[l].astype(bf), ffn2_w_up[l].astype(bf), ffn2_w_down[l].astype(bf),
                       final_norm[None], bsz, t)
    return xc.reshape(bsz, t, D_MODEL)
```
